```python
import jax, jax.numpy as jnp
from jax import lax
import numpy as np

D_MODEL = 2048
BATCH = 4
SEQ = 2048
DEPTH = 2
DEC_BATCH = 128
DEC_SEQ = 4
PAST_LEN = 2048
PAGE_SIZE = 128

N_BRANCH = 4
W_BRANCH = D_MODEL // N_BRANCH
CONV_W = 31
POOL_WINDOWS = (2, 4, 8, 16)
POOL_GROUPS = len(POOL_WINDOWS)
POOL_GC = W_BRANCH // POOL_GROUPS
POOL_KEEP = max(POOL_WINDOWS) - 1
N_HEADS = 4
HEAD_DIM = W_BRANCH // N_HEADS
ROT_DIM = HEAD_DIM // 4
ROPE_THETA = 500000.0
MOBA_BLOCK = 256
MOBA_TOPK = 3
Q_CHUNK = 128
GMLP_CHUNK = 128
GMLP_GROUPS = 4
GMLP_GC = W_BRANCH // GMLP_GROUPS
N_EXPERTS = 32
TOP_K = 4
D_FF = D_MODEL
SWIGLU_LIMIT = 7.0
SWIGLU_ALPHA = 1.702
MOE_BLOCK = 128
EPS = 1e-6
C_IN = 8 * W_BRANCH + N_BRANCH * D_MODEL

kernel_name = "hybrid_conv_pool_moba_gmlp_moe_step"

F32 = jnp.float32


def rms_norm(x, g):
    xf = x.astype(F32)
    y = xf * lax.rsqrt(jnp.mean(xf * xf, -1, keepdims=True) + EPS)
    return (y * g.astype(F32)).astype(x.dtype)


def layer_norm(x, g, b):
    xf = x.astype(F32)
    mu = jnp.mean(xf, -1, keepdims=True)
    var = jnp.mean(jnp.square(xf - mu), -1, keepdims=True)
    return ((xf - mu) * lax.rsqrt(var + EPS) * g.astype(F32) + b.astype(F32)).astype(x.dtype)


def partial_rope(x, pos):
    half = ROT_DIM // 2
    inv = jnp.power(jnp.float32(ROPE_THETA), -jnp.arange(half, dtype=F32) * 2.0 / ROT_DIM)
    ang = pos.astype(F32)[:, None] * inv[None, :]
    cos = jnp.cos(ang)[None, :, None, :]
    sin = jnp.sin(ang)[None, :, None, :]
    xr = x[..., :ROT_DIM].astype(F32)
    x1, x2 = xr[..., :half], xr[..., half:]
    rot = jnp.concatenate([x1 * cos - x2 * sin, x2 * cos + x1 * sin], -1)
    return jnp.concatenate([rot.astype(x.dtype), x[..., ROT_DIM:]], -1)


def conv_branch(a_glu, conv_prev, w_dw, b_dw, ln_g, ln_b, w_out):
    a, g = jnp.split(a_glu, 2, -1)
    z = a * jax.nn.sigmoid(g)
    z_ext = jnp.concatenate([conv_prev.astype(z.dtype), z], 1)
    y = lax.conv_general_dilated(z_ext, w_dw[:, None, :].astype(z.dtype), (1,), 'VALID',
                                 dimension_numbers=('NWC', 'WIO', 'NWC'),
                                 feature_group_count=W_BRANCH) + b_dw
    y = jax.nn.silu(layer_norm(y, ln_g, ln_b))
    return y @ w_out, z_ext[:, -(CONV_W - 1):]


def pool_branch(p_in, pool_prev, w_grp, scale, w_out):
    B, T, W = p_in.shape
    P = pool_prev.shape[1]
    ext = jnp.concatenate([pool_prev.astype(p_in.dtype), p_in], 1)
    cs = jnp.cumsum(ext.astype(F32), axis=1)
    cs = jnp.concatenate([jnp.zeros((B, 1, W), F32), cs], 1)
    end = jnp.arange(T) + P + 1
    means = []
    for gi, win in enumerate(POOL_WINDOWS):
        start = jnp.maximum(end - win, 0)
        sl = cs[..., gi * POOL_GC:(gi + 1) * POOL_GC]
        cnt = (end - start).astype(F32)[None, :, None]
        means.append((sl[:, end] - sl[:, start]) / cnt)
    mean = jnp.concatenate(means, -1).astype(p_in.dtype)
    d = (mean - p_in).reshape(B, T, POOL_GROUPS, POOL_GC)
    y = jnp.einsum('btgc,gcd->btgd', d, w_grp).reshape(B, T, W) * scale
    return y @ w_out, ext[:, -POOL_KEEP:]


def gmlp_branch(uv, ln_g, ln_b, w_sp, b_sp, w_out):
    u, v = jnp.split(uv, 2, -1)
    v = layer_norm(v, ln_g, ln_b)
    B, T, W = v.shape
    n_chunks = -(-T // GMLP_CHUNK)
    Tp = n_chunks * GMLP_CHUNK
    vp = jnp.pad(v, ((0, 0), (0, Tp - T), (0, 0))).reshape(B, n_chunks, GMLP_CHUNK, GMLP_GROUPS, GMLP_GC)
    mask = jnp.tril(jnp.ones((GMLP_CHUNK, GMLP_CHUNK), bool))
    ws = jnp.where(mask[None], w_sp, jnp.zeros_like(w_sp))
    mixed = jnp.einsum('gts,bnsgc->bntgc', ws, vp) + jnp.transpose(b_sp)[None, None, :, :, None]
    mixed = mixed.reshape(B, Tp, W)[:, :T]
    return (u * mixed) @ w_out, v[:, (n_chunks - 1) * GMLP_CHUNK:]


def moba_attention(q, k, v, q_pos):
    B, T, H, HD = q.shape
    L = k.shape[1]
    nb = -(-L // MOBA_BLOCK)
    padl = nb * MOBA_BLOCK - L
    kb = jnp.pad(k, ((0, 0), (0, padl), (0, 0), (0, 0))).reshape(B, nb, MOBA_BLOCK, H, HD)
    vb = jnp.pad(v, ((0, 0), (0, padl), (0, 0), (0, 0))).reshape(B, nb, MOBA_BLOCK, H, HD)
    kmean = jnp.mean(kb.astype(F32), axis=2)
    n_sel = min(MOBA_TOPK, nb)
    qc = min(Q_CHUNK, T)
    nq = -(-T // qc)
    Tp = nq * qc
    qs = jnp.pad(q, ((0, 0), (0, Tp - T), (0, 0), (0, 0))).reshape(B, nq, qc, H, HD)
    pos_c = jnp.pad(q_pos, (0, Tp - T), mode='edge').reshape(nq, qc)
    scale = HEAD_DIM ** -0.5
    blk_ids = jnp.arange(nb)
    off = jnp.arange(MOBA_BLOCK)
    hidx = jnp.arange(H)[None, :, None]

    def one_seq(args):
        q_s, kb_s, vb_s, km_s = args

        def one_chunk(cargs):
            q_c, p_c = cargs
            own = p_c // MOBA_BLOCK
            qf = q_c.astype(F32)
            bs = jnp.einsum('thd,nhd->thn', qf, km_s)
            past = blk_ids[None, None, :] < own[:, None, None]
            bs = jnp.where(past, bs, -jnp.inf)
            _, sel = lax.top_k(bs, n_sel)
            own_b = jnp.broadcast_to(own[:, None, None], (qc, H, 1))
            idx = jnp.concatenate([sel, own_b], -1)
            ok = jnp.concatenate([sel < own_b, jnp.ones((qc, H, 1), bool)], -1)
            kg = kb_s[idx, :, hidx, :]
            vg = vb_s[idx, :, hidx, :]
            s = jnp.einsum('thd,thjkd->thjk', qf, kg.astype(F32)) * scale
            kpos = idx[..., None] * MOBA_BLOCK + off
            mask = ok[..., None] & (kpos <= p_c[:, None, None, None])
            s = jnp.where(mask, s, -jnp.inf).reshape(qc, H, -1)
            pr = jax.nn.softmax(s, -1).reshape(qc, H, n_sel + 1, MOBA_BLOCK)
            return jnp.einsum('thjk,thjkd->thd', pr.astype(vg.dtype), vg).astype(q.dtype)

        return lax.map(one_chunk, (q_s, pos_c))

    out = lax.map(one_seq, (qs, kb, vb, kmean))
    return out.reshape(B, Tp, H, HD)[:, :T]


def moe(h, router_w, router_b, w1, b1, w2, b2):
    B, T, D = h.shape
    xf = h.reshape(-1, D)
    M = xf.shape[0]
    logits = (xf @ router_w + router_b).astype(F32)
    top_val, top_idx = lax.top_k(logits, TOP_K)
    gate = jax.nn.softmax(top_val, -1)
    rows = M * TOP_K
    e_flat = top_idx.reshape(-1)
    order = jnp.argsort(e_flat)
    e_sorted = e_flat[order]
    tok_sorted = order // TOP_K
    g_sorted = gate.reshape(-1)[order]
    counts = jnp.bincount(e_flat, length=N_EXPERTS)
    padded = ((counts + MOE_BLOCK - 1) // MOE_BLOCK) * MOE_BLOCK
    grp_start = jnp.cumsum(counts) - counts
    pad_end = jnp.cumsum(padded)
    pad_start = pad_end - padded
    dest = pad_start[e_sorted] + jnp.arange(rows) - grp_start[e_sorted]
    n_blocks = -(-rows // MOE_BLOCK) + N_EXPERTS
    cap = n_blocks * MOE_BLOCK
    slot_tok = jnp.full((cap,), M, jnp.int32).at[dest].set(tok_sorted.astype(jnp.int32))
    slot_gate = jnp.zeros((cap,), F32).at[dest].set(g_sorted)
    blk_expert = jnp.minimum(jnp.searchsorted(pad_end, jnp.arange(n_blocks) * MOE_BLOCK, side='right'),
                             N_EXPERTS - 1)
    x_pad = jnp.concatenate([xf, jnp.zeros((1, D), xf.dtype)], 0)
    xb = x_pad[slot_tok].reshape(n_blocks, MOE_BLOCK, D)

    def expert_block(args):
        xblk, e = args
        hu = xblk @ w1[e] + b1[e]
        hg, hl = jnp.split(hu, 2, -1)
        hg = jnp.minimum(hg, SWIGLU_LIMIT)
        hl = jnp.clip(hl, -SWIGLU_LIMIT, SWIGLU_LIMIT)
        act = hg * jax.nn.sigmoid(SWIGLU_ALPHA * hg) * (hl + 1.0)
        return act @ w2[e] + b2[e]

    yb = lax.map(expert_block, (xb, blk_expert)).reshape(cap, D)
    y = jnp.zeros((M + 1, D), h.dtype).at[slot_tok].add(yb * slot_gate[:, None].astype(yb.dtype))
    return y[:M].reshape(B, T, D)


def decoder_layer(x, c, pos0, k_past, v_past, conv_prev, pool_prev, p):
    B, T, _ = x.shape
    W = W_BRANCH
    mod = jax.nn.silu(c) @ p['w_ada'] + p['b_ada']
    sh1, sc1, g1, sh2, sc2, g2 = [m[:, None, :] for m in jnp.split(mod, 6, -1)]
    h = rms_norm(x, p['norm1_g']) * (1.0 + sc1) + sh1
    proj = h @ p['w_in']
    y_a, conv_new = conv_branch(proj[..., 0:2 * W], conv_prev, p['conv_w'], p['conv_b'],
                                p['conv_ln_g'], p['conv_ln_b'], p['conv_out'])
    y_b, pool_new = pool_branch(proj[..., 2 * W:3 * W], pool_prev, p['pool_w'], p['pool_scale'], p['pool_out'])
    pos = jnp.arange(T) + pos0
    q = proj[..., 3 * W:4 * W].reshape(B, T, N_HEADS, HEAD_DIM)
    k = proj[..., 4 * W:5 * W].reshape(B, T, N_HEADS, HEAD_DIM)
    v = proj[..., 5 * W:6 * W].reshape(B, T, N_HEADS, HEAD_DIM)
    q = partial_rope(rms_norm(q, p['q_norm_g']), pos)
    k = partial_rope(rms_norm(k, p['k_norm_g']), pos)
    if k_past is None:
        k_all, v_all = k, v
    else:
        k_all = jnp.concatenate([k_past.astype(k.dtype), k], 1)
        v_all = jnp.concatenate([v_past.astype(v.dtype), v], 1)
    att = moba_attention(q, k_all, v_all, pos).reshape(B, T, W)
    y_c = att @ p['attn_out']
    y_d, gv_new = gmlp_branch(proj[..., 6 * W:8 * W], p['gmlp_ln_g'], p['gmlp_ln_b'],
                              p['gmlp_ws'], p['gmlp_b'], p['gmlp_out'])
    gates = jax.nn.sigmoid(proj[..., 8 * W:].reshape(B, T, N_BRANCH, D_MODEL))
    branches = jnp.stack([y_a, y_b, y_c, y_d], 2)
    merged = jnp.sum(gates * branches, 2)
    x = x + g1 * (merged @ p['w_o'])
    h2 = rms_norm(x, p['norm2_g']) * (1.0 + sc2) + sh2
    x = x + g2 * moe(h2, p['router_w'], p['router_b'], p['exp_w1'], p['exp_b1'], p['exp_w2'], p['exp_b2'])
    return x, (k, v, conv_new, pool_new, gv_new)


def setup_inputs(seed: int = 0) -> dict:
    key = jax.random.key(seed)
    ks = iter(jax.random.split(key, 48))
    W = W_BRANCH

    def nrm(shape, std):
        return jax.random.normal(next(ks), shape, F32) * std

    n_pages = PAST_LEN // PAGE_SIZE
    n_used = DEC_BATCH * n_pages
    n_pool = n_used + max(n_used // 4, 1)
    page_table = jax.random.permutation(next(ks), n_pool)[:n_used].reshape(DEC_BATCH, n_pages).astype(jnp.int32)
    return {
        'x_prompt': nrm((BATCH, SEQ, D_MODEL), 1.0),
        'x_sample': nrm((DEC_BATCH, DEC_SEQ, D_MODEL), 1.0),
        'cache_k': nrm((DEPTH, n_pool, PAGE_SIZE, N_HEADS, HEAD_DIM), 1.0),
        'cache_v': nrm((DEPTH, n_pool, PAGE_SIZE, N_HEADS, HEAD_DIM), 1.0),
        'state_conv': nrm((DEPTH, DEC_BATCH, CONV_W - 1, W), 0.5),
        'state_pool': nrm((DEPTH, DEC_BATCH, POOL_KEEP, W), 1.0),
        'page_table': page_table,
        'c_prompt': nrm((BATCH, D_MODEL), 1.0),
        'c_sample': nrm((DEC_BATCH, D_MODEL), 1.0),
        'w_ada': nrm((DEPTH, D_MODEL, 6 * D_MODEL), 0.5 * D_MODEL ** -0.5),
        'b_ada': nrm((DEPTH, 6 * D_MODEL), 0.02),
        'norm1_g': 1.0 + nrm((DEPTH, D_MODEL), 0.02),
        'norm2_g': 1.0 + nrm((DEPTH, D_MODEL), 0.02),
        'w_in': nrm((DEPTH, D_MODEL, C_IN), D_MODEL ** -0.5),
        'conv_w': nrm((DEPTH, CONV_W, W), CONV_W ** -0.5),
        'conv_b': nrm((DEPTH, W), 0.02),
        'conv_ln_g': 1.0 + nrm((DEPTH, W), 0.02),
        'conv_ln_b': nrm((DEPTH, W), 0.02),
        'conv_out': nrm((DEPTH, W, D_MODEL), W ** -0.5),
        'pool_w': nrm((DEPTH, POOL_GROUPS, POOL_GC, POOL_GC), POOL_GC ** -0.5),
        'pool_scale': 1.0 + nrm((DEPTH, W), 0.1),
        'pool_out': nrm((DEPTH, W, D_MODEL), W ** -0.5),
        'q_norm_g': 1.0 + nrm((DEPTH, HEAD_DIM), 0.02),
        'k_norm_g': 1.0 + nrm((DEPTH, HEAD_DIM), 0.02),
        'attn_out': nrm((DEPTH, W, D_MODEL), W ** -0.5),
        'gmlp_ln_g': 1.0 + nrm((DEPTH, W), 0.02),
        'gmlp_ln_b': nrm((DEPTH, W), 0.02),
        'gmlp_ws': nrm((DEPTH, GMLP_GROUPS, GMLP_CHUNK, GMLP_CHUNK), GMLP_CHUNK ** -0.5),
        'gmlp_b': 1.0 + nrm((DEPTH, GMLP_GROUPS, GMLP_CHUNK), 0.02),
        'gmlp_out': nrm((DEPTH, W, D_MODEL), W ** -0.5),
        'w_o': nrm((DEPTH, D_MODEL, D_MODEL), D_MODEL ** -0.5),
        'router_w': nrm((DEPTH, D_MODEL, N_EXPERTS), D_MODEL ** -0.5),
        'router_b': nrm((DEPTH, N_EXPERTS), 0.01),
        'exp_w1': nrm((DEPTH, N_EXPERTS, D_MODEL, 2 * D_FF), D_MODEL ** -0.5),
        'exp_b1': nrm((DEPTH, N_EXPERTS, 2 * D_FF), 0.02),
        'exp_w2': nrm((DEPTH, N_EXPERTS, D_FF, D_MODEL), D_FF ** -0.5),
        'exp_b2': nrm((DEPTH, N_EXPERTS, D_MODEL), 0.02),
    }


def reference(x_prompt, x_sample, cache_k, cache_v, state_conv, state_pool, page_table, c_prompt, c_sample,
              w_ada, b_ada, norm1_g, norm2_g, w_in, conv_w, conv_b, conv_ln_g, conv_ln_b, conv_out,
              pool_w, pool_scale, pool_out, q_norm_g, k_norm_g, attn_out, gmlp_ln_g, gmlp_ln_b,
              gmlp_ws, gmlp_b, gmlp_out, w_o, router_w, router_b, exp_w1, exp_b1, exp_w2, exp_b2):
    n_pages = PAST_LEN // PAGE_SIZE
    n_dec = x_sample.shape[0]
    n_pr = x_prompt.shape[0]
    xp, xs = x_prompt, x_sample
    st_p, st_s = [], []
    for l in range(DEPTH):
        p = {'w_ada': w_ada[l], 'b_ada': b_ada[l], 'norm1_g': norm1_g[l], 'norm2_g': norm2_g[l],
             'w_in': w_in[l], 'conv_w': conv_w[l], 'conv_b': conv_b[l], 'conv_ln_g': conv_ln_g[l],
             'conv_ln_b': conv_ln_b[l], 'conv_out': conv_out[l], 'pool_w': pool_w[l],
             'pool_scale': pool_scale[l], 'pool_out': pool_out[l], 'q_norm_g': q_norm_g[l],
             'k_norm_g': k_norm_g[l], 'attn_out': attn_out[l], 'gmlp_ln_g': gmlp_ln_g[l],
             'gmlp_ln_b': gmlp_ln_b[l], 'gmlp_ws': gmlp_ws[l], 'gmlp_b': gmlp_b[l],
             'gmlp_out': gmlp_out[l], 'w_o': w_o[l], 'router_w': router_w[l], 'router_b': router_b[l],
             'exp_w1': exp_w1[l], 'exp_b1': exp_b1[l], 'exp_w2': exp_w2[l], 'exp_b2': exp_b2[l]}
        conv0 = jnp.zeros((n_pr, CONV_W - 1, W_BRANCH), xp.dtype)
        pool0 = jnp.zeros((n_pr, 0, W_BRANCH), xp.dtype)
        xp, sp = decoder_layer(xp, c_prompt, 0, None, None, conv0, pool0, p)
        k_past = cache_k[l, page_table].reshape(n_dec, n_pages * PAGE_SIZE, N_HEADS, HEAD_DIM)
        v_past = cache_v[l, page_table].reshape(n_dec, n_pages * PAGE_SIZE, N_HEADS, HEAD_DIM)
        xs, ss = decoder_layer(xs, c_sample, PAST_LEN, k_past, v_past, state_conv[l], state_pool[l], p)
        st_p.append(sp)
        st_s.append(ss)
    k_prompt = jnp.stack([s[0] for s in st_p])
    v_prompt = jnp.stack([s[1] for s in st_p])
    conv_prompt = jnp.stack([s[2] for s in st_p])
    pool_prompt = jnp.stack([s[3] for s in st_p])
    gmlp_v_prompt = jnp.stack([s[4] for s in st_p])
    k_sample = jnp.stack([s[0] for s in st_s])
    v_sample = jnp.stack([s[1] for s in st_s])
    conv_sample = jnp.stack([s[2] for s in st_s])
    pool_sample = jnp.stack([s[3] for s in st_s])
    gmlp_v_sample = jnp.stack([s[4] for s in st_s])
    return (xp, xs, k_prompt, v_prompt, k_sample, v_sample, conv_prompt, conv_sample,
            pool_prompt, pool_sample, gmlp_v_prompt, gmlp_v_sample)
```

```python
import functools

import jax
import jax.numpy as jnp
from jax import lax
from jax.experimental import pallas as pl
from jax.experimental.pallas import tpu as pltpu

F32 = jnp.float32
BF16 = jnp.bfloat16

N_BRANCH = 4
CONV_W = 31
POOL_WINDOWS = (2, 4, 8, 16)
POOL_KEEP = max(POOL_WINDOWS) - 1
N_HEADS = 4
ROPE_THETA = 500000.0
MOBA_BLOCK = 256
MOBA_TOPK = 3
GMLP_CHUNK = 128
GMLP_GROUPS = 4
TOP_K = 4
SWIGLU_LIMIT = 7.0
SWIGLU_ALPHA = 1.702
EPS = 1e-6

LANE = 128
TM = 256
CONV_HIST = 32
POOL_HIST = 16
MOE_BM = 256
MIB = 1024 * 1024


def _cp(sem, vmem_mib):
    return pltpu.CompilerParams(dimension_semantics=sem, vmem_limit_bytes=vmem_mib * MIB)


def _sigmoid(x):
    return 1.0 / (1.0 + jnp.exp(-x))


def _layer_norm(x, g, b):
    mu = jnp.mean(x, -1, keepdims=True)
    xc = x - mu
    var = jnp.mean(xc * xc, -1, keepdims=True)
    return xc * lax.rsqrt(var + EPS) * g + b


def _dot(a, b):
    return jnp.dot(a, b, preferred_element_type=F32)


def _dot_nt(a, b, precision=None):
    return lax.dot_general(a, b, (((1,), (1,)), ((), ())), precision=precision,
                           preferred_element_type=F32)


def _ada_kernel(c_ref, w_ref, b_ref, o_ref):
    c = c_ref[...]
    s = (c * _sigmoid(c)).astype(BF16)
    o_ref[...] = _dot(s, w_ref[...].astype(BF16)) + b_ref[...]


def _ada(c_all, w_ada, b_ada):
    L, D, N = w_ada.shape
    R = c_all.shape[0]
    tn = 1024
    return pl.pallas_call(
        _ada_kernel,
        grid=(L, N // tn),
        in_specs=[pl.BlockSpec((R, D), lambda l, n: (0, 0)),
                  pl.BlockSpec((None, D, tn), lambda l, n: (l, 0, n)),
                  pl.BlockSpec((None, 1, tn), lambda l, n: (l, 0, n))],
        out_specs=pl.BlockSpec((None, R, tn), lambda l, n: (l, 0, n)),
        out_shape=jax.ShapeDtypeStruct((L, R, N), F32),
        compiler_params=_cp(("arbitrary", "arbitrary"), 40),
        name="ada",
    )(c_all, w_ada, b_ada.reshape(L, 1, N))


def _norm_mod_kernel(x_ref, g_ref, sc_ref, sh_ref, o_ref):
    x = x_ref[...]
    ms = jnp.mean(x * x, -1, keepdims=True)
    y = x * lax.rsqrt(ms + EPS) * g_ref[...]
    o_ref[...] = (y * (1.0 + sc_ref[...]) + sh_ref[...]).astype(o_ref.dtype)


def _norm_mod(x, g, mr, sc_chunk, sh_chunk, seq_of_tile, out_dtype):
    M, D = x.shape
    return pl.pallas_call(
        _norm_mod_kernel,
        grid=(M // TM,),
        in_specs=[pl.BlockSpec((TM, D), lambda i: (i, 0)),
                  pl.BlockSpec((1, D), lambda i: (0, 0)),
                  pl.BlockSpec((None, TM, D), lambda i: (seq_of_tile(i), 0, sc_chunk)),
                  pl.BlockSpec((None, TM, D), lambda i: (seq_of_tile(i), 0, sh_chunk))],
        out_specs=pl.BlockSpec((TM, D), lambda i: (i, 0)),
        out_shape=jax.ShapeDtypeStruct((M, D), out_dtype),
        compiler_params=_cp(("arbitrary",), 40),
        name="norm_mod",
    )(x, g, mr, mr)


def _panel_mm_kernel(a_ref, w_ref, o_ref, wbf_ref):
    @pl.when(pl.program_id(1) == 0)
    def _():
        wbf_ref[...] = w_ref[...].astype(BF16)

    o_ref[...] = _dot(a_ref[...], wbf_ref[...])


def _panel_mm(a, w, l, tm, tn):
    M, K = a.shape
    N = w.shape[2]
    return pl.pallas_call(
        _panel_mm_kernel,
        grid=(N // tn, M // tm),
        in_specs=[pl.BlockSpec((tm, K), lambda n, m: (m, 0)),
                  pl.BlockSpec((None, K, tn), lambda n, m: (l, 0, n))],
        out_specs=pl.BlockSpec((tm, tn), lambda n, m: (m, n)),
        out_shape=jax.ShapeDtypeStruct((M, N), F32),
        scratch_shapes=[pltpu.VMEM((K, tn), BF16)],
        compiler_params=_cp(("arbitrary", "arbitrary"), 48),
        name="in_proj",
    )(a, w)


def _conv_p_kernel(a_ref, g_ref, w_ref, cb_ref, lg_ref, lb_ref, y_ref, cn_ref, zbuf, ybuf):
    i = pl.program_id(1)
    W = a_ref.shape[1]

    @pl.when(i == 0)
    def _():
        zbuf[0:CONV_HIST, :] = jnp.zeros((CONV_HIST, W), F32)

    @pl.when(i > 0)
    def _():
        zbuf[0:CONV_HIST, :] = zbuf[TM:TM + CONV_HIST, :]

    zbuf[CONV_HIST:CONV_HIST + TM, :] = a_ref[...] * _sigmoid(g_ref[...])
    base = CONV_HIST - (CONV_W - 1)
    rb = 128
    for c0 in range(0, W, LANE):
        for r0 in range(0, TM, rb):
            acc = jnp.broadcast_to(cb_ref[:, c0:c0 + LANE], (rb, LANE))
            for j in range(CONV_W):
                acc = acc + w_ref[j:j + 1, c0:c0 + LANE] * zbuf[pl.ds(base + r0 + j, rb), c0:c0 + LANE]
            ybuf[r0:r0 + rb, c0:c0 + LANE] = acc
    y = _layer_norm(ybuf[...], lg_ref[...], lb_ref[...])
    y_ref[...] = (y * _sigmoid(y)).astype(y_ref.dtype)

    @pl.when(i == pl.num_programs(1) - 1)
    def _():
        cn_ref[...] = zbuf[CONV_HIST + TM - (CONV_W - 1):CONV_HIST + TM, :]


def _conv_prompt(proj, B, T, W, cw, cb, lg, lb):
    nt = T // TM
    return pl.pallas_call(
        _conv_p_kernel,
        grid=(B, nt),
        in_specs=[pl.BlockSpec((TM, W), lambda b, i: (b * nt + i, 0)),
                  pl.BlockSpec((TM, W), lambda b, i: (b * nt + i, 1)),
                  pl.BlockSpec((CONV_W, W), lambda b, i: (0, 0)),
                  pl.BlockSpec((1, W), lambda b, i: (0, 0)),
                  pl.BlockSpec((1, W), lambda b, i: (0, 0)),
                  pl.BlockSpec((1, W), lambda b, i: (0, 0))],
        out_specs=[pl.BlockSpec((TM, W), lambda b, i: (b * nt + i, 0)),
                   pl.BlockSpec((None, CONV_W - 1, W), lambda b, i: (b, 0, 0))],
        out_shape=[jax.ShapeDtypeStruct((B * T, W), BF16),
                   jax.ShapeDtypeStruct((B, CONV_W - 1, W), F32)],
        scratch_shapes=[pltpu.VMEM((CONV_HIST + TM, W), F32), pltpu.VMEM((TM, W), F32)],
        compiler_params=_cp(("arbitrary", "arbitrary"), 32),
        name="conv_prompt",
    )(proj, proj, cw, cb, lg, lb)


def _pool_p_kernel(p_ref, pw_ref, ps_ref, y_ref, pn_ref, ebuf):
    i = pl.program_id(1)
    W = p_ref.shape[1]

    @pl.when(i == 0)
    def _():
        ebuf[0:POOL_HIST, :] = jnp.zeros((POOL_HIST, W), F32)

    @pl.when(i > 0)
    def _():
        ebuf[0:POOL_HIST, :] = ebuf[TM:TM + POOL_HIST, :]

    ebuf[POOL_HIST:POOL_HIST + TM, :] = p_ref[...]
    t_abs = i * TM + lax.broadcasted_iota(jnp.int32, (TM, 1), 0)
    for gi, win in enumerate(POOL_WINDOWS):
        c0 = gi * LANE
        s = ebuf[POOL_HIST:POOL_HIST + TM, c0:c0 + LANE]
        for k in range(1, win):
            s = s + ebuf[pl.ds(POOL_HIST - k, TM), c0:c0 + LANE]
        cnt = jnp.minimum(t_abs + 1, win).astype(F32)
        d = s / cnt - p_ref[:, c0:c0 + LANE]
        yg = _dot(d.astype(BF16), pw_ref[gi].astype(BF16)) * ps_ref[:, c0:c0 + LANE]
        y_ref[:, c0:c0 + LANE] = yg.astype(y_ref.dtype)

    @pl.when(i == pl.num_programs(1) - 1)
    def _():
        pn_ref[...] = ebuf[POOL_HIST + TM - POOL_KEEP:POOL_HIST + TM, :]


def _pool_prompt(proj, B, T, W, pw, ps):
    nt = T // TM
    G = len(POOL_WINDOWS)
    return pl.pallas_call(
        _pool_p_kernel,
        grid=(B, nt),
        in_specs=[pl.BlockSpec((TM, W), lambda b, i: (b * nt + i, 2)),
                  pl.BlockSpec((G, LANE, LANE), lambda b, i: (0, 0, 0)),
                  pl.BlockSpec((1, W), lambda b, i: (0, 0))],
        out_specs=[pl.BlockSpec((TM, W), lambda b, i: (b * nt + i, 0)),
                   pl.BlockSpec((None, POOL_KEEP, W), lambda b, i: (b, 0, 0))],
        out_shape=[jax.ShapeDtypeStruct((B * T, W), BF16),
                   jax.ShapeDtypeStruct((B, POOL_KEEP, W), F32)],
        scratch_shapes=[pltpu.VMEM((POOL_HIST + TM, W), F32)],
        compiler_params=_cp(("arbitrary", "arbitrary"), 32),
        name="pool_prompt",
    )(proj, pw, ps)


def _norm_rope_head(xh, g, cos, sa, sb):
    ms = jnp.mean(xh * xh, -1, keepdims=True)
    xn = xh * lax.rsqrt(ms + EPS) * g
    hd = xh.shape[1]
    half = hd // 8
    up = pltpu.roll(xn, hd - half, axis=1)
    dn = pltpu.roll(xn, half, axis=1)
    return xn * cos + up * sa + dn * sb


def _qkv_p_kernel(q_ref, k_ref, v_ref, qg_ref, kg_ref, cos_ref, sa_ref, sb_ref,
                  qo_ref, ko_ref, vo_ref, kb_ref, vb_ref, km_ref):
    i = pl.program_id(1)
    cos, sa, sb = cos_ref[...], sa_ref[...], sb_ref[...]

    @pl.when(i == 0)
    def _():
        km_ref[...] = jnp.zeros(km_ref.shape, F32)

    blk_row = lax.broadcasted_iota(jnp.int32, (km_ref.shape[0], LANE), 0)
    for h in range(N_HEADS):
        c0 = h * LANE
        qo_ref[:, c0:c0 + LANE] = _norm_rope_head(q_ref[:, c0:c0 + LANE], qg_ref[...], cos, sa, sb)
        kh = _norm_rope_head(k_ref[:, c0:c0 + LANE], kg_ref[...], cos, sa, sb)
        ko_ref[:, c0:c0 + LANE] = kh
        kb_ref[:, c0:c0 + LANE] = kh.astype(BF16)
        km_ref[:, c0:c0 + LANE] = jnp.where(blk_row == i, jnp.mean(kh, axis=0, keepdims=True),
                                            km_ref[:, c0:c0 + LANE])
    v = v_ref[...]
    vo_ref[...] = v
    vb_ref[...] = v.astype(BF16)


def _qkv_prompt(proj, B, T, W, qg, kg, cos, sa, sb):
    nt = T // TM
    row = lambda b, i: (b * nt + i, 0)
    return pl.pallas_call(
        _qkv_p_kernel,
        grid=(B, nt),
        in_specs=[pl.BlockSpec((TM, W), lambda b, i: (b * nt + i, 3)),
                  pl.BlockSpec((TM, W), lambda b, i: (b * nt + i, 4)),
                  pl.BlockSpec((TM, W), lambda b, i: (b * nt + i, 5)),
                  pl.BlockSpec((1, LANE), lambda b, i: (0, 0)),
                  pl.BlockSpec((1, LANE), lambda b, i: (0, 0)),
                  pl.BlockSpec((TM, LANE), lambda b, i: (i, 0)),
                  pl.BlockSpec((TM, LANE), lambda b, i: (i, 0)),
                  pl.BlockSpec((TM, LANE), lambda b, i: (i, 0))],
        out_specs=[pl.BlockSpec((TM, W), row), pl.BlockSpec((TM, W), row), pl.BlockSpec((TM, W), row),
                   pl.BlockSpec((TM, W), row), pl.BlockSpec((TM, W), row),
                   pl.BlockSpec((nt, W), lambda b, i: (b, 0))],
        out_shape=[jax.ShapeDtypeStruct((B * T, W), F32), jax.ShapeDtypeStruct((B * T, W), F32),
                   jax.ShapeDtypeStruct((B * T, W), F32), jax.ShapeDtypeStruct((B * T, W), BF16),
                   jax.ShapeDtypeStruct((B * T, W), BF16), jax.ShapeDtypeStruct((B * nt, W), F32)],
        compiler_params=_cp(("arbitrary", "arbitrary"), 32),
        name="qkv_prompt",
    )(proj, proj, proj, qg, kg, cos, sa, sb)


def _topk_block_cols(bs, past):
    nb = bs.shape[1]
    jidx = lax.broadcasted_iota(jnp.int32, bs.shape, 1)
    cols = []
    for n in range(nb):
        bn = bs[:, n:n + 1]
        beats = ((bs > bn) | ((bs == bn) & (jidx < n))) & past
        rank = jnp.sum(beats.astype(F32), axis=-1, keepdims=True)
        cols.append(rank < (MOBA_TOPK - 0.5))
    return cols


def _attn_p_kernel(q_ref, k_ref, v_ref, km_ref, o_ref):
    i = pl.program_id(2)
    nb = km_ref.shape[0]
    q = q_ref[...]
    bs = _dot_nt(q, km_ref[...], precision=lax.Precision.HIGHEST)
    blk = lax.broadcasted_iota(jnp.int32, bs.shape, 1)
    sel = _topk_block_cols(bs, blk < i)
    s = _dot_nt(q.astype(BF16), k_ref[...]) * (q.shape[1] ** -0.5)
    row = lax.broadcasted_iota(jnp.int32, (TM, MOBA_BLOCK), 0)
    col = lax.broadcasted_iota(jnp.int32, (TM, MOBA_BLOCK), 1)
    tri = col <= row
    pieces = []
    for j in range(nb):
        jv = jnp.full((TM, 1), j, jnp.int32)
        mj = ((jv == i) & tri) | ((jv < i) & sel[j])
        pieces.append(jnp.where(mj, s[:, j * MOBA_BLOCK:(j + 1) * MOBA_BLOCK], -jnp.inf))
    s = jnp.concatenate(pieces, axis=-1)
    m = jnp.max(s, -1, keepdims=True)
    p = jnp.exp(s - m)
    den = jnp.sum(p, -1, keepdims=True)
    o = _dot(p.astype(BF16), v_ref[...]) / den
    o_ref[...] = o.astype(o_ref.dtype)


def _attn_prompt(q, kb, vb, kmean, B, T, W):
    nt = T // TM
    return pl.pallas_call(
        _attn_p_kernel,
        grid=(B, N_HEADS, nt),
        in_specs=[pl.BlockSpec((TM, LANE), lambda b, h, i: (b * nt + i, h)),
                  pl.BlockSpec((T, LANE), lambda b, h, i: (b, h)),
                  pl.BlockSpec((T, LANE), lambda b, h, i: (b, h)),
                  pl.BlockSpec((nt, LANE), lambda b, h, i: (b, h))],
        out_specs=pl.BlockSpec((TM, LANE), lambda b, h, i: (b * nt + i, h)),
        out_shape=jax.ShapeDtypeStruct((B * T, W), BF16),
        compiler_params=_cp(("arbitrary", "arbitrary", "arbitrary"), 40),
        name="attn_prompt",
    )(q, kb, vb, kmean)


def _gmlp_p_kernel(u_ref, v_ref, lg_ref, lb_ref, ws_ref, bt_ref, y_ref, gv_ref, vbuf):
    i = pl.program_id(1)
    vbuf[...] = _layer_norm(v_ref[...], lg_ref[...], lb_ref[...])
    row = lax.broadcasted_iota(jnp.int32, (GMLP_CHUNK, GMLP_CHUNK), 0)
    col = lax.broadcasted_iota(jnp.int32, (GMLP_CHUNK, GMLP_CHUNK), 1)
    tri = col <= row
    for g in range(GMLP_GROUPS):
        c0 = g * LANE
        w = jnp.where(tri, ws_ref[g], 0.0).astype(BF16)
        for r0 in range(0, TM, GMLP_CHUNK):
            mixed = _dot(w, vbuf[r0:r0 + GMLP_CHUNK, c0:c0 + LANE].astype(BF16)) + bt_ref[:, g:g + 1]
            y_ref[r0:r0 + GMLP_CHUNK, c0:c0 + LANE] = (
                u_ref[r0:r0 + GMLP_CHUNK, c0:c0 + LANE] * mixed).astype(y_ref.dtype)

    @pl.when(i == pl.num_programs(1) - 1)
    def _():
        gv_ref[...] = vbuf[TM - GMLP_CHUNK:TM, :]


def _gmlp_prompt(proj, B, T, W, lg, lb, ws, bt):
    nt = T // TM
    return pl.pallas_call(
        _gmlp_p_kernel,
        grid=(B, nt),
        in_specs=[pl.BlockSpec((TM, W), lambda b, i: (b * nt + i, 6)),
                  pl.BlockSpec((TM, W), lambda b, i: (b * nt + i, 7)),
                  pl.BlockSpec((1, W), lambda b, i: (0, 0)),
                  pl.BlockSpec((1, W), lambda b, i: (0, 0)),
                  pl.BlockSpec((GMLP_GROUPS, GMLP_CHUNK, GMLP_CHUNK), lambda b, i: (0, 0, 0)),
                  pl.BlockSpec((GMLP_CHUNK, GMLP_GROUPS), lambda b, i: (0, 0))],
        out_specs=[pl.BlockSpec((TM, W), lambda b, i: (b * nt + i, 0)),
                   pl.BlockSpec((None, GMLP_CHUNK, W), lambda b, i: (b, 0, 0))],
        out_shape=[jax.ShapeDtypeStruct((B * T, W), BF16),
                   jax.ShapeDtypeStruct((B, GMLP_CHUNK, W), F32)],
        scratch_shapes=[pltpu.VMEM((TM, W), F32)],
        compiler_params=_cp(("arbitrary", "arbitrary"), 32),
        name="gmlp_prompt",
    )(proj, proj, lg, lb, ws, bt)


def _sample_mix_kernel(p_ref, sc_ref, sp_ref, cw_ref, cb_ref, clg_ref, clb_ref, pw_ref, ps_ref,
                       qg_ref, kg_ref, cos_ref, sa_ref, sb_ref, glg_ref, glb_ref, ws_ref, gb_ref,
                       ya_ref, yb_ref, yd_ref, q_ref, k_ref, v_ref, cn_ref, pn_ref, gv_ref):
    ts, sb_rows, _ = p_ref.shape
    W = ya_ref.shape[2]
    n_conv = CONV_W - 1

    def col(c):
        return slice(c * W, (c + 1) * W)

    z = [p_ref[t, :, col(0)] * _sigmoid(p_ref[t, :, col(1)]) for t in range(ts)]

    def zext(r):
        return sc_ref[r] if r < n_conv else z[r - n_conv]

    for t in range(ts):
        acc = jnp.broadcast_to(cb_ref[...], (sb_rows, W))
        for j in range(CONV_W):
            acc = acc + cw_ref[j:j + 1, :] * zext(t + j)
        y = _layer_norm(acc, clg_ref[...], clb_ref[...])
        ya_ref[t] = (y * _sigmoid(y)).astype(ya_ref.dtype)
    for r in range(n_conv):
        cn_ref[r] = zext(r + ts)

    def pext(r, c0):
        if r < POOL_KEEP:
            return sp_ref[r, :, c0:c0 + LANE]
        return p_ref[r - POOL_KEEP, :, 2 * W + c0:2 * W + c0 + LANE]

    for t in range(ts):
        for gi, win in enumerate(POOL_WINDOWS):
            c0 = gi * LANE
            s = pext(POOL_KEEP + t, c0)
            for k in range(1, win):
                s = s + pext(POOL_KEEP + t - k, c0)
            d = s / float(win) - pext(POOL_KEEP + t, c0)
            yg = _dot(d.astype(BF16), pw_ref[gi].astype(BF16)) * ps_ref[:, c0:c0 + LANE]
            yb_ref[t, :, c0:c0 + LANE] = yg.astype(yb_ref.dtype)
    for r in range(POOL_KEEP):
        pn_ref[r] = sp_ref[r + ts] if r + ts < POOL_KEEP else p_ref[r + ts - POOL_KEEP, :, col(2)]

    for t in range(ts):
        cos = cos_ref[t:t + 1, :]
        sa = sa_ref[t:t + 1, :]
        sb = sb_ref[t:t + 1, :]
        for h in range(N_HEADS):
            c0 = h * LANE
            q_ref[t, :, c0:c0 + LANE] = _norm_rope_head(
                p_ref[t, :, 3 * W + c0:3 * W + c0 + LANE], qg_ref[...], cos, sa, sb)
            k_ref[t, :, c0:c0 + LANE] = _norm_rope_head(
                p_ref[t, :, 4 * W + c0:4 * W + c0 + LANE], kg_ref[...], cos, sa, sb)
        v_ref[t] = p_ref[t, :, col(5)]

    vn = [_layer_norm(p_ref[t, :, col(7)], glg_ref[...], glb_ref[...]) for t in range(ts)]
    for t in range(ts):
        gv_ref[t] = vn[t]
        for g in range(GMLP_GROUPS):
            c0 = g * LANE
            mixed = jnp.broadcast_to(gb_ref[g:g + 1, t:t + 1], (sb_rows, LANE))
            for s_ in range(t + 1):
                mixed = mixed + ws_ref[g, t:t + 1, s_:s_ + 1] * vn[s_][:, c0:c0 + LANE]
            yd_ref[t, :, c0:c0 + LANE] = (
                p_ref[t, :, 6 * W + c0:6 * W + c0 + LANE] * mixed).astype(yd_ref.dtype)


SAMPLE_SEQ_BLOCK = 32


def _sample_mix(proj_s, nb, ts, W, sc_t, sp_t, cw, cb, clg, clb, pw, ps, qg, kg, cos, sa, sb,
                glg, glb, ws, gb):
    G = len(POOL_WINDOWS)
    sblk = SAMPLE_SEQ_BLOCK
    full2 = lambda shape: pl.BlockSpec(shape, lambda i: (0, 0))
    full3 = lambda shape: pl.BlockSpec(shape, lambda i: (0, 0, 0))
    seq3 = lambda rows, c: pl.BlockSpec((rows, sblk, c), lambda i: (0, i, 0))
    tok = lambda dt: jax.ShapeDtypeStruct((ts, nb, W), dt)
    return pl.pallas_call(
        _sample_mix_kernel,
        grid=(nb // sblk,),
        in_specs=[seq3(ts, 8 * W), seq3(CONV_W - 1, W), seq3(POOL_KEEP, W),
                  full2((CONV_W, W)), full2((1, W)), full2((1, W)), full2((1, W)),
                  full3((G, LANE, LANE)), full2((1, W)),
                  full2((1, LANE)), full2((1, LANE)),
                  full2((8, LANE)), full2((8, LANE)), full2((8, LANE)),
                  full2((1, W)), full2((1, W)),
                  full3((GMLP_GROUPS, GMLP_CHUNK, GMLP_CHUNK)), full2((GMLP_GROUPS, GMLP_CHUNK))],
        out_specs=[seq3(ts, W)] * 6 + [seq3(CONV_W - 1, W), seq3(POOL_KEEP, W), seq3(ts, W)],
        out_shape=[tok(BF16), tok(BF16), tok(BF16), tok(F32), tok(F32), tok(F32),
                   jax.ShapeDtypeStruct((CONV_W - 1, nb, W), F32),
                   jax.ShapeDtypeStruct((POOL_KEEP, nb, W), F32), tok(F32)],
        compiler_params=_cp(("arbitrary",), 40),
        name="sample_mix",
    )(proj_s, sc_t, sp_t, cw, cb, clg, clb, pw, ps, qg, kg, cos, sa, sb, glg, glb, ws, gb)


def _attn_s_kernel(pt_ref, q_ref, kn_ref, vn_ref, *refs, n_pages, ts):
    kp = refs[:n_pages]
    vp = refs[n_pages:2 * n_pages]
    o_ref = refs[2 * n_pages]
    page = kp[0].shape[0]
    W = q_ref.shape[1]
    hd = W // N_HEADS
    R = q_ref.shape[0]
    ppb = MOBA_BLOCK // page
    nblk = n_pages // ppb
    row = lax.broadcasted_iota(jnp.int32, (R, W), 0)
    lane = lax.broadcasted_iota(jnp.int32, (R, W), 1)
    head_mask = (lane >= (row >> 3) * hd) & (lane < ((row >> 3) + 1) * hd)
    qbd = jnp.where(head_mask, q_ref[...], 0.0)
    kms = []
    for j in range(nblk):
        acc = jnp.sum(kp[j * ppb][...], axis=0, keepdims=True)
        for u in range(1, ppb):
            acc = acc + jnp.sum(kp[j * ppb + u][...], axis=0, keepdims=True)
        kms.append(acc * (1.0 / MOBA_BLOCK))
    km = jnp.concatenate(kms, axis=0)
    bs = _dot_nt(qbd, km, precision=lax.Precision.HIGHEST)
    sel = _topk_block_cols(bs, lax.broadcasted_iota(jnp.int32, bs.shape, 1) >= 0)
    scale = hd ** -0.5
    qb = qbd.astype(BF16)
    pieces = []
    for pg in range(n_pages):
        s = _dot_nt(qb, kp[pg][...].astype(BF16)) * scale
        pieces.append(jnp.where(sel[pg // ppb], s, -jnp.inf))
    s_past = jnp.concatenate(pieces, axis=-1)
    t_row = lax.broadcasted_iota(jnp.int32, (R, 1), 0) & 7
    s_own = []
    for u in range(ts):
        su = jnp.sum(qbd * kn_ref[u:u + 1, :], axis=-1, keepdims=True) * scale
        s_own.append(jnp.where(t_row >= u, su, -jnp.inf))
    m = jnp.max(s_past, -1, keepdims=True)
    for su in s_own:
        m = jnp.maximum(m, su)
    p_past = jnp.exp(s_past - m)
    den = jnp.sum(p_past, -1, keepdims=True)
    acc = jnp.zeros((R, W), F32)
    for pg in range(n_pages):
        acc = acc + _dot(p_past[:, pg * page:(pg + 1) * page].astype(BF16), vp[pg][...].astype(BF16))
    for u in range(ts):
        pu = jnp.exp(s_own[u] - m)
        den = den + pu
        acc = acc + pu * vn_ref[u:u + 1, :]
    acc = jnp.where(head_mask, acc / den, 0.0)
    out = acc[0:8]
    for h in range(1, N_HEADS):
        out = out + acc[8 * h:8 * h + 8]
    o_ref[...] = out


def _attn_sample(pt_flat, q_rep, k_new, v_new, ck, cv, l, n_pool, nb, ts, n_pages, W):
    page = ck.shape[1]
    R = q_rep.shape[1]

    def page_spec(j):
        return pl.BlockSpec((None, page, W), lambda b, pt: (l * n_pool + pt[b * n_pages + j], 0, 0))

    in_specs = ([pl.BlockSpec((None, R, W), lambda b, pt: (b, 0, 0)),
                 pl.BlockSpec((None, 8, W), lambda b, pt: (b, 0, 0)),
                 pl.BlockSpec((None, 8, W), lambda b, pt: (b, 0, 0))]
                + [page_spec(j) for j in range(n_pages)] + [page_spec(j) for j in range(n_pages)])
    return pl.pallas_call(
        functools.partial(_attn_s_kernel, n_pages=n_pages, ts=ts),
        grid_spec=pltpu.PrefetchScalarGridSpec(
            num_scalar_prefetch=1, grid=(nb,), in_specs=in_specs,
            out_specs=pl.BlockSpec((None, 8, W), lambda b, pt: (b, 0, 0))),
        out_shape=jax.ShapeDtypeStruct((nb, 8, W), F32),
        compiler_params=_cp(("arbitrary",), 48),
        name="attn_sample",
    )(pt_flat, q_rep, k_new, v_new, *([ck] * n_pages), *([cv] * n_pages))


def _merge_kernel(ya_ref, yb_ref, yc_ref, yd_ref, wa_ref, wb_ref, wc_ref, wd_ref,
                  ga_ref, gb_ref, gc_ref, gd_ref, o_ref, wbf_ref):
    w_refs = (wa_ref, wb_ref, wc_ref, wd_ref)

    @pl.when(pl.program_id(1) == 0)
    def _():
        for b in range(N_BRANCH):
            wbf_ref[b] = w_refs[b][...].astype(BF16)

    acc = None
    for b, (y_ref, g_ref) in enumerate(zip((ya_ref, yb_ref, yc_ref, yd_ref),
                                           (ga_ref, gb_ref, gc_ref, gd_ref))):
        term = _sigmoid(g_ref[...]) * _dot(y_ref[...], wbf_ref[b])
        acc = term if acc is None else acc + term
    o_ref[...] = acc.astype(o_ref.dtype)


def _merge(ys, ws, proj, l, W, D):
    M = proj.shape[0]
    tn = W
    npb = D // tn
    gate0 = 8 * W // tn
    y_spec = pl.BlockSpec((TM, W), lambda n, m: (m, 0))
    w_spec = pl.BlockSpec((None, W, tn), lambda n, m: (l, 0, n))
    g_specs = [pl.BlockSpec((TM, tn), functools.partial(lambda n, m, b: (m, gate0 + b * npb + n), b=b))
               for b in range(N_BRANCH)]
    return pl.pallas_call(
        _merge_kernel,
        grid=(npb, M // TM),
        in_specs=[y_spec] * 4 + [w_spec] * 4 + g_specs,
        out_specs=pl.BlockSpec((TM, tn), lambda n, m: (m, n)),
        out_shape=jax.ShapeDtypeStruct((M, D), BF16),
        scratch_shapes=[pltpu.VMEM((N_BRANCH, W, tn), BF16)],
        compiler_params=_cp(("arbitrary", "arbitrary"), 32),
        name="merge",
    )(*ys, *ws, proj, proj, proj, proj)


def _wo_kernel(a_ref, w_ref, x_ref, g_ref, o_ref, wbf_ref):
    @pl.when(pl.program_id(1) == 0)
    def _():
        wbf_ref[...] = w_ref[...].astype(BF16)

    o_ref[...] = x_ref[...] + g_ref[...] * _dot(a_ref[...], wbf_ref[...])


def _wo_residual(merged, w_o, l, x, mr, gate_chunk, seq_of_tile):
    M, D = x.shape
    tn = 1024
    ncb = D // tn
    return pl.pallas_call(
        _wo_kernel,
        grid=(ncb, M // TM),
        in_specs=[pl.BlockSpec((TM, D), lambda n, m: (m, 0)),
                  pl.BlockSpec((None, D, tn), lambda n, m: (l, 0, n)),
                  pl.BlockSpec((TM, tn), lambda n, m: (m, n)),
                  pl.BlockSpec((None, TM, tn), lambda n, m: (seq_of_tile(m), 0, gate_chunk * ncb + n))],
        out_specs=pl.BlockSpec((TM, tn), lambda n, m: (m, n)),
        out_shape=jax.ShapeDtypeStruct((M, D), F32),
        scratch_shapes=[pltpu.VMEM((D, tn), BF16)],
        compiler_params=_cp(("arbitrary", "arbitrary"), 40),
        name="wo_residual",
    )(merged, w_o, x, mr)


def _norm_router_kernel(x_ref, g_ref, sc_ref, sh_ref, rw_ref, rb_ref, h_ref, lg_ref):
    x = x_ref[...]
    ms = jnp.mean(x * x, -1, keepdims=True)
    y = x * lax.rsqrt(ms + EPS) * g_ref[...]
    h = y * (1.0 + sc_ref[...]) + sh_ref[...]
    h_ref[...] = h
    lg_ref[...] = jnp.dot(h, rw_ref[...], precision=lax.Precision.HIGHEST,
                          preferred_element_type=F32) + rb_ref[...]


def _norm_router(x, g, mr, sc_chunk, sh_chunk, seq_of_tile, rw, rb):
    M, D = x.shape
    E = rw.shape[1]
    return pl.pallas_call(
        _norm_router_kernel,
        grid=(M // TM,),
        in_specs=[pl.BlockSpec((TM, D), lambda i: (i, 0)),
                  pl.BlockSpec((1, D), lambda i: (0, 0)),
                  pl.BlockSpec((None, TM, D), lambda i: (seq_of_tile(i), 0, sc_chunk)),
                  pl.BlockSpec((None, TM, D), lambda i: (seq_of_tile(i), 0, sh_chunk)),
                  pl.BlockSpec((D, E), lambda i: (0, 0)),
                  pl.BlockSpec((1, E), lambda i: (0, 0))],
        out_specs=[pl.BlockSpec((TM, D), lambda i: (i, 0)),
                   pl.BlockSpec((TM, E), lambda i: (i, 0))],
        out_shape=[jax.ShapeDtypeStruct((M, D), F32), jax.ShapeDtypeStruct((M, E), F32)],
        compiler_params=_cp(("arbitrary",), 40),
        name="norm_router",
    )(x, g, mr, mr, rw, rb)


def _row_copy(src_hbm, dst, src_row, dst_row, sem):
    return pltpu.make_async_copy(src_hbm.at[pl.ds(src_row, 1)], dst.at[pl.ds(dst_row, 1)], sem)


def _dispatch_kernel(tok_ref, nu_ref, h_hbm, o_ref, buf, sem):
    m = pl.program_id(0)

    @pl.when(m < nu_ref[0])
    def _():
        def issue(r, c):
            _row_copy(h_hbm, buf, tok_ref[m * MOE_BM + r], r, sem).start()
            return c

        lax.fori_loop(0, MOE_BM, issue, 0)

        def drain(r, c):
            _row_copy(h_hbm, buf, 0, r, sem).wait()
            return c

        lax.fori_loop(0, MOE_BM, drain, 0)
        o_ref[...] = buf[...].astype(o_ref.dtype)

    @pl.when(m >= nu_ref[0])
    def _():
        o_ref[...] = jnp.zeros(o_ref.shape, o_ref.dtype)


def _dispatch(slot_tok, n_used, h2, n_blocks):
    M, D = h2.shape
    return pl.pallas_call(
        _dispatch_kernel,
        grid_spec=pltpu.PrefetchScalarGridSpec(
            num_scalar_prefetch=2, grid=(n_blocks,),
            in_specs=[pl.BlockSpec(memory_space=pl.ANY)],
            out_specs=pl.BlockSpec((MOE_BM, D), lambda m, tok, nu: (m, 0)),
            scratch_shapes=[pltpu.VMEM((MOE_BM, D), F32), pltpu.SemaphoreType.DMA(())]),
        out_shape=jax.ShapeDtypeStruct((n_blocks * MOE_BM, D), BF16),
        compiler_params=_cp(("arbitrary",), 32),
        name="moe_dispatch",
    )(slot_tok, n_used, h2)


def _moe_up_kernel(be_ref, nu_ref, x_ref, wg_ref, wl_ref, bg_ref, bl_ref, o_ref, wgb, wlb):
    m = pl.program_id(1)
    e = be_ref[m]
    prev = be_ref[jnp.maximum(m - 1, 0)]

    @pl.when((m == 0) | (e != prev))
    def _():
        wgb[...] = wg_ref[...].astype(BF16)
        wlb[...] = wl_ref[...].astype(BF16)

    @pl.when(m < nu_ref[0])
    def _():
        x = x_ref[...]
        hg = jnp.minimum(_dot(x, wgb[...]) + bg_ref[...], SWIGLU_LIMIT)
        hl = jnp.clip(_dot(x, wlb[...]) + bl_ref[...], -SWIGLU_LIMIT, SWIGLU_LIMIT)
        o_ref[...] = (hg * _sigmoid(SWIGLU_ALPHA * hg) * (hl + 1.0)).astype(o_ref.dtype)

    @pl.when(m >= nu_ref[0])
    def _():
        o_ref[...] = jnp.zeros(o_ref.shape, o_ref.dtype)


def _moe_up(blk_expert, n_used, xb, w1, b1, l, E):
    cap, D = xb.shape
    F = w1.shape[2] // 2
    tn = 512
    nf = F // tn
    n_blocks = cap // MOE_BM
    return pl.pallas_call(
        _moe_up_kernel,
        grid_spec=pltpu.PrefetchScalarGridSpec(
            num_scalar_prefetch=2, grid=(nf, n_blocks),
            in_specs=[pl.BlockSpec((MOE_BM, D), lambda n, m, be, nu: (m, 0)),
                      pl.BlockSpec((None, D, tn), lambda n, m, be, nu: (l * E + be[m], 0, n)),
                      pl.BlockSpec((None, D, tn), lambda n, m, be, nu: (l * E + be[m], 0, nf + n)),
                      pl.BlockSpec((None, 1, tn), lambda n, m, be, nu: (l * E + be[m], 0, n)),
                      pl.BlockSpec((None, 1, tn), lambda n, m, be, nu: (l * E + be[m], 0, nf + n))],
            out_specs=pl.BlockSpec((MOE_BM, tn), lambda n, m, be, nu: (m, n)),
            scratch_shapes=[pltpu.VMEM((D, tn), BF16), pltpu.VMEM((D, tn), BF16)]),
        out_shape=jax.ShapeDtypeStruct((cap, F), BF16),
        compiler_params=_cp(("arbitrary", "arbitrary"), 40),
        name="moe_up",
    )(blk_expert, n_used, xb, w1, w1, b1, b1)


def _moe_down_kernel(be_ref, nu_ref, a_ref, w_ref, b_ref, g_ref, o_ref, wb):
    m = pl.program_id(1)
    e = be_ref[m]
    prev = be_ref[jnp.maximum(m - 1, 0)]

    @pl.when((m == 0) | (e != prev))
    def _():
        wb[...] = w_ref[...].astype(BF16)

    @pl.when(m < nu_ref[0])
    def _():
        o_ref[...] = (_dot(a_ref[...], wb[...]) + b_ref[...]) * g_ref[...]

    @pl.when(m >= nu_ref[0])
    def _():
        o_ref[...] = jnp.zeros(o_ref.shape, o_ref.dtype)


def _moe_down(blk_expert, n_used, act, w2, b2, slot_gate, l, E):
    cap, F = act.shape
    D = w2.shape[2]
    tn = 1024
    n_blocks = cap // MOE_BM
    return pl.pallas_call(
        _moe_down_kernel,
        grid_spec=pltpu.PrefetchScalarGridSpec(
            num_scalar_prefetch=2, grid=(D // tn, n_blocks),
            in_specs=[pl.BlockSpec((MOE_BM, F), lambda n, m, be, nu: (m, 0)),
                      pl.BlockSpec((None, F, tn), lambda n, m, be, nu: (l * E + be[m], 0, n)),
                      pl.BlockSpec((None, 1, tn), lambda n, m, be, nu: (l * E + be[m], 0, n)),
                      pl.BlockSpec((MOE_BM, 1), lambda n, m, be, nu: (m, 0))],
            out_specs=pl.BlockSpec((MOE_BM, tn), lambda n, m, be, nu: (m, n)),
            scratch_shapes=[pltpu.VMEM((F, tn), BF16)]),
        out_shape=jax.ShapeDtypeStruct((cap, D), F32),
        compiler_params=_cp(("arbitrary", "arbitrary"), 40),
        name="moe_down",
    )(blk_expert, n_used, act, w2, b2, slot_gate)


COMBINE_TM = 128


def _combine_kernel(slot_ref, y_hbm, x_ref, g_ref, o_ref, buf, sem):
    i = pl.program_id(0)

    def issue(r, c):
        for k in range(TOP_K):
            _row_copy(y_hbm, buf.at[k], slot_ref[(i * COMBINE_TM + r) * TOP_K + k], r, sem).start()
        return c

    lax.fori_loop(0, COMBINE_TM, issue, 0)

    def drain(r, c):
        for k in range(TOP_K):
            _row_copy(y_hbm, buf.at[k], 0, r, sem).wait()
        return c

    lax.fori_loop(0, COMBINE_TM, drain, 0)
    y = buf[0] + buf[1]
    for k in range(2, TOP_K):
        y = y + buf[k]
    o_ref[...] = x_ref[...] + g_ref[...] * y


def _combine(slot_of, yb, x, mr, gate_chunk, seq_of_tile):
    M, D = x.shape
    per = TM // COMBINE_TM
    return pl.pallas_call(
        _combine_kernel,
        grid_spec=pltpu.PrefetchScalarGridSpec(
            num_scalar_prefetch=1, grid=(M // COMBINE_TM,),
            in_specs=[pl.BlockSpec(memory_space=pl.ANY),
                      pl.BlockSpec((COMBINE_TM, D), lambda i, s: (i, 0)),
                      pl.BlockSpec((None, COMBINE_TM, D), lambda i, s: (seq_of_tile(i // per), 0, gate_chunk))],
            out_specs=pl.BlockSpec((COMBINE_TM, D), lambda i, s: (i, 0)),
            scratch_shapes=[pltpu.VMEM((TOP_K, COMBINE_TM, D), F32), pltpu.SemaphoreType.DMA(())]),
        out_shape=jax.ShapeDtypeStruct((M, D), F32),
        compiler_params=_cp(("arbitrary",), 32),
        name="moe_combine",
    )(slot_of, yb, x, mr)


def _route(logits, E):
    M = logits.shape[0]
    rows = M * TOP_K
    top_val, top_idx = lax.top_k(logits, TOP_K)
    gate = jax.nn.softmax(top_val, -1)
    e_flat = top_idx.reshape(-1).astype(jnp.int32)
    order = jnp.argsort(e_flat)
    e_sorted = e_flat[order]
    counts = jnp.bincount(e_flat, length=E)
    padded = ((counts + MOE_BM - 1) // MOE_BM) * MOE_BM
    grp_start = jnp.cumsum(counts) - counts
    pad_end = jnp.cumsum(padded)
    pad_start = pad_end - padded
    dest = (pad_start[e_sorted] + jnp.arange(rows) - grp_start[e_sorted]).astype(jnp.int32)
    n_blocks = -(-rows // MOE_BM) + E
    cap = n_blocks * MOE_BM
    slot_tok = jnp.zeros((cap,), jnp.int32).at[dest].set((order // TOP_K).astype(jnp.int32))
    slot_gate = jnp.zeros((cap,), F32).at[dest].set(gate.reshape(-1)[order])
    slot_of = jnp.zeros((rows,), jnp.int32).at[order].set(dest)
    blk_expert = jnp.minimum(
        jnp.searchsorted(pad_end, jnp.arange(n_blocks) * MOE_BM, side='right'), E - 1).astype(jnp.int32)
    n_used = (pad_end[-1] // MOE_BM).astype(jnp.int32).reshape(1)
    return slot_tok, slot_gate.reshape(cap, 1), slot_of, blk_expert, n_used, n_blocks


def _rope_tables(pos, hd):
    rot = hd // 4
    half = rot // 2
    inv = jnp.power(jnp.float32(ROPE_THETA), -jnp.arange(half, dtype=F32) * 2.0 / rot)
    ang = pos.astype(F32)[:, None] * inv[None, :]
    cos, sin = jnp.cos(ang), jnp.sin(ang)
    n = pos.shape[0]
    c = jnp.concatenate([cos, cos, jnp.ones((n, hd - rot), F32)], 1)
    sa = jnp.concatenate([-sin, jnp.zeros((n, hd - half), F32)], 1)
    sb = jnp.concatenate([jnp.zeros((n, half), F32), sin, jnp.zeros((n, hd - rot), F32)], 1)
    return c, sa, sb


def kernel(x_prompt, x_sample, cache_k, cache_v, state_conv, state_pool, page_table, c_prompt, c_sample,
           w_ada, b_ada, norm1_g, norm2_g, w_in, conv_w, conv_b, conv_ln_g, conv_ln_b, conv_out,
           pool_w, pool_scale, pool_out, q_norm_g, k_norm_g, attn_out, gmlp_ln_g, gmlp_ln_b,
           gmlp_ws, gmlp_b, gmlp_out, w_o, router_w, router_b, exp_w1, exp_b1, exp_w2, exp_b2):
    B, T, D = x_prompt.shape
    NB, TS, _ = x_sample.shape
    L = w_ada.shape[0]
    W = D // N_BRANCH
    HD = W // N_HEADS
    E = router_w.shape[2]
    n_pool, PAGE = cache_k.shape[1], cache_k.shape[2]
    n_pages = page_table.shape[1]
    past = n_pages * PAGE
    M_p, M_s = B * T, NB * TS
    M = M_p + M_s
    assert HD == LANE and W == N_HEADS * LANE and T % TM == 0 and TM % NB == 0 and NB % 8 == 0
    assert M % 512 == 0 and M_s % TM == 0 and TS <= 8 and past % MOBA_BLOCK == 0
    tiles_per_seq = T // TM

    def seq_of_tile(i):
        return jnp.minimum(i // tiles_per_seq, B)

    x = jnp.concatenate([x_prompt.reshape(M_p, D), x_sample.transpose(1, 0, 2).reshape(M_s, D)], 0)
    n_c = B + NB
    c_all = jnp.concatenate([c_prompt, c_sample, jnp.zeros((-n_c % 8, D), F32)], 0)
    mod_all = _ada(c_all, w_ada, b_ada)

    cos_p, sa_p, sb_p = _rope_tables(jnp.arange(T), HD)
    cos_s, sa_s, sb_s = [jnp.pad(t, ((0, 8 - TS), (0, 0))) for t in _rope_tables(past + jnp.arange(TS), HD)]
    ck = cache_k.reshape(L * n_pool, PAGE, W)
    cv = cache_v.reshape(L * n_pool, PAGE, W)
    pt_flat = page_table.reshape(-1).astype(jnp.int32)
    w1 = exp_w1.reshape(L * E, D, exp_w1.shape[3])
    b1 = exp_b1.reshape(L * E, 1, exp_b1.shape[2])
    w2 = exp_w2.reshape(L * E, exp_w2.shape[2], D)
    b2 = exp_b2.reshape(L * E, 1, D)

    def to_bt(a_t):
        return a_t.reshape(TS, NB, -1).transpose(1, 0, 2)

    outs = {k: [] for k in ('kp', 'vp', 'ks', 'vs', 'cp', 'cs', 'pp', 'ps', 'gp', 'gs')}
    for l in range(L):
        mod = mod_all[l]
        mr = jnp.concatenate([jnp.broadcast_to(mod[:B, None, :], (B, TM, 6 * D)),
                              jnp.tile(mod[B:B + NB], (TM // NB, 1))[None]], 0)
        h = _norm_mod(x, norm1_g[l][None], mr, 1, 0, seq_of_tile, BF16)
        proj = _panel_mm(h, w_in, l, 512, 1024)

        row = lambda v: v[l][None]
        ya_p, conv_p = _conv_prompt(proj, B, T, W, conv_w[l], row(conv_b), row(conv_ln_g), row(conv_ln_b))
        yb_p, pool_p = _pool_prompt(proj, B, T, W, pool_w[l], row(pool_scale))
        q_p, k_p, v_p, kb_p, vb_p, kmean_p = _qkv_prompt(proj, B, T, W, row(q_norm_g), row(k_norm_g),
                                                        cos_p, sa_p, sb_p)
        yc_p = _attn_prompt(q_p, kb_p, vb_p, kmean_p, B, T, W)
        yd_p, gv_p = _gmlp_prompt(proj, B, T, W, row(gmlp_ln_g), row(gmlp_ln_b), gmlp_ws[l],
                                  jnp.transpose(gmlp_b[l]))

        proj_s = proj[M_p:, :8 * W].reshape(TS, NB, 8 * W)
        (ya_s, yb_s, yd_s, q_s, k_s, v_s, conv_s, pool_s, gv_s) = _sample_mix(
            proj_s, NB, TS, W, state_conv[l].transpose(1, 0, 2), state_pool[l].transpose(1, 0, 2),
            conv_w[l], row(conv_b), row(conv_ln_g), row(conv_ln_b), pool_w[l], row(pool_scale),
            row(q_norm_g), row(k_norm_g), cos_s, sa_s, sb_s, row(gmlp_ln_g), row(gmlp_ln_b),
            gmlp_ws[l], gmlp_b[l])
        q_bt = jnp.pad(to_bt(q_s), ((0, 0), (0, 8 - TS), (0, 0)))
        q_rep = jnp.tile(q_bt, (1, N_HEADS, 1))
        k_bt = jnp.pad(to_bt(k_s), ((0, 0), (0, 8 - TS), (0, 0)))
        v_bt = jnp.pad(to_bt(v_s), ((0, 0), (0, 8 - TS), (0, 0)))
        att_s = _attn_sample(pt_flat, q_rep, k_bt, v_bt, ck, cv, l, n_pool, NB, TS, n_pages, W)
        yc_s = att_s[:, :TS].transpose(1, 0, 2).reshape(M_s, W).astype(BF16)

        ys = [jnp.concatenate([p_, s_.reshape(M_s, W)], 0) for p_, s_ in
              ((ya_p, ya_s), (yb_p, yb_s), (yc_p, yc_s), (yd_p, yd_s))]
        merged = _merge(ys, (conv_out, pool_out, attn_out, gmlp_out), proj, l, W, D)
        x = _wo_residual(merged, w_o, l, x, mr, 2, seq_of_tile)
        h2, logits = _norm_router(x, norm2_g[l][None], mr, 4, 3, seq_of_tile, router_w[l], router_b[l][None])
        slot_tok, slot_gate, slot_of, blk_expert, n_used, n_blocks = _route(logits, E)
        xb = _dispatch(slot_tok, n_used, h2, n_blocks)
        act = _moe_up(blk_expert, n_used, xb, w1, b1, l, E)
        yb = _moe_down(blk_expert, n_used, act, w2, b2, slot_gate, l, E)
        x = _combine(slot_of, yb, x, mr, 5, seq_of_tile)

        outs['kp'].append(k_p.reshape(B, T, N_HEADS, HD))
        outs['vp'].append(v_p.reshape(B, T, N_HEADS, HD))
        outs['ks'].append(to_bt(k_s).reshape(NB, TS, N_HEADS, HD))
        outs['vs'].append(to_bt(v_s).reshape(NB, TS, N_HEADS, HD))
        outs['cp'].append(conv_p)
        outs['cs'].append(conv_s.transpose(1, 0, 2))
        outs['pp'].append(pool_p)
        outs['ps'].append(pool_s.transpose(1, 0, 2))
        outs['gp'].append(gv_p)
        outs['gs'].append(to_bt(gv_s))

    y_prompt = x[:M_p].reshape(B, T, D)
    y_sample = to_bt(x[M_p:])
    st = lambda k: jnp.stack(outs[k])
    return (y_prompt, y_sample, st('kp'), st('vp'), st('ks'), st('vs'), st('cp'), st('cs'),
            st('pp'), st('ps'), st('gp'), st('gs'))
```

```python
import functools

import jax
import jax.numpy as jnp
from jax import lax
from jax.experimental import pallas as pl
from jax.experimental.pallas import tpu as pltpu

F32 = jnp.float32
BF16 = jnp.bfloat16

N_BRANCH = 4
CONV_W = 31
POOL_WINDOWS = (2, 4, 8, 16)
POOL_KEEP = max(POOL_WINDOWS) - 1
N_HEADS = 4
ROPE_THETA = 500000.0
MOBA_BLOCK = 256
MOBA_TOPK = 3
GMLP_CHUNK = 128
GMLP_GROUPS = 4
TOP_K = 4
SWIGLU_LIMIT = 7.0
SWIGLU_ALPHA = 1.702
EPS = 1e-6

LANE = 128
TM = 256
CONV_HIST = 32
POOL_HIST = 16
MOE_BM = 256
MIB = 1024 * 1024


def _cp(sem, vmem_mib):
    return pltpu.CompilerParams(dimension_semantics=sem, vmem_limit_bytes=vmem_mib * MIB)


def _sigmoid(x):
    return 1.0 / (1.0 + jnp.exp(-x))


def _layer_norm(x, g, b):
    mu = jnp.mean(x, -1, keepdims=True)
    xc = x - mu
    var = jnp.mean(xc * xc, -1, keepdims=True)
    return xc * lax.rsqrt(var + EPS) * g + b


def _dot(a, b):
    return jnp.dot(a, b, preferred_element_type=F32)


def _dot_nt(a, b, precision=None):
    return lax.dot_general(a, b, (((1,), (1,)), ((), ())), precision=precision,
                           preferred_element_type=F32)


def _ada_kernel(c_ref, w_ref, b_ref, o_ref):
    c = c_ref[...]
    s = (c * _sigmoid(c)).astype(BF16)
    o_ref[...] = _dot(s, w_ref[...].astype(BF16)) + b_ref[...]


def _ada(c_all, w_ada, b_ada):
    L, D, N = w_ada.shape
    R = c_all.shape[0]
    tn = 1024
    return pl.pallas_call(
        _ada_kernel,
        grid=(L, N // tn),
        in_specs=[pl.BlockSpec((R, D), lambda l, n: (0, 0)),
                  pl.BlockSpec((None, D, tn), lambda l, n: (l, 0, n)),
                  pl.BlockSpec((None, 1, tn), lambda l, n: (l, 0, n))],
        out_specs=pl.BlockSpec((None, R, tn), lambda l, n: (l, 0, n)),
        out_shape=jax.ShapeDtypeStruct((L, R, N), F32),
        compiler_params=_cp(("arbitrary", "arbitrary"), 40),
        name="ada",
    )(c_all, w_ada, b_ada.reshape(L, 1, N))


def _norm_mod_kernel(x_ref, g_ref, sc_ref, sh_ref, o_ref):
    x = x_ref[...]
    ms = jnp.mean(x * x, -1, keepdims=True)
    y = x * lax.rsqrt(ms + EPS) * g_ref[...]
    o_ref[...] = (y * (1.0 + sc_ref[...]) + sh_ref[...]).astype(o_ref.dtype)


def _norm_mod(x, g, mr, sc_chunk, sh_chunk, seq_of_tile, out_dtype):
    M, D = x.shape
    return pl.pallas_call(
        _norm_mod_kernel,
        grid=(M // TM,),
        in_specs=[pl.BlockSpec((TM, D), lambda i: (i, 0)),
                  pl.BlockSpec((1, D), lambda i: (0, 0)),
                  pl.BlockSpec((None, TM, D), lambda i: (seq_of_tile(i), 0, sc_chunk)),
                  pl.BlockSpec((None, TM, D), lambda i: (seq_of_tile(i), 0, sh_chunk))],
        out_specs=pl.BlockSpec((TM, D), lambda i: (i, 0)),
        out_shape=jax.ShapeDtypeStruct((M, D), out_dtype),
        compiler_params=_cp(("arbitrary",), 40),
        name="norm_mod",
    )(x, g, mr, mr)


def _panel_mm_kernel(a_ref, w_ref, o_ref, wbf_ref):
    @pl.when(pl.program_id(1) == 0)
    def _():
        wbf_ref[...] = w_ref[...].astype(BF16)

    o_ref[...] = _dot(a_ref[...], wbf_ref[...])


def _panel_mm(a, w, l, tm, tn):
    M, K = a.shape
    N = w.shape[2]
    return pl.pallas_call(
        _panel_mm_kernel,
        grid=(N // tn, M // tm),
        in_specs=[pl.BlockSpec((tm, K), lambda n, m: (m, 0)),
                  pl.BlockSpec((None, K, tn), lambda n, m: (l, 0, n))],
        out_specs=pl.BlockSpec((tm, tn), lambda n, m: (m, n)),
        out_shape=jax.ShapeDtypeStruct((M, N), F32),
        scratch_shapes=[pltpu.VMEM((K, tn), BF16)],
        compiler_params=_cp(("arbitrary", "arbitrary"), 48),
        name="in_proj",
    )(a, w)


def _conv_p_kernel(a_ref, g_ref, w_ref, cb_ref, lg_ref, lb_ref, y_ref, cn_ref, zbuf, ybuf):
    i = pl.program_id(1)
    W = a_ref.shape[1]

    @pl.when(i == 0)
    def _():
        zbuf[0:CONV_HIST, :] = jnp.zeros((CONV_HIST, W), F32)

    @pl.when(i > 0)
    def _():
        zbuf[0:CONV_HIST, :] = zbuf[TM:TM + CONV_HIST, :]

    zbuf[CONV_HIST:CONV_HIST + TM, :] = a_ref[...] * _sigmoid(g_ref[...])
    base = CONV_HIST - (CONV_W - 1)
    rb = 128
    for c0 in range(0, W, LANE):
        for r0 in range(0, TM, rb):
            acc = jnp.broadcast_to(cb_ref[:, c0:c0 + LANE], (rb, LANE))
            for j in range(CONV_W):
                acc = acc + w_ref[j:j + 1, c0:c0 + LANE] * zbuf[pl.ds(base + r0 + j, rb), c0:c0 + LANE]
            ybuf[r0:r0 + rb, c0:c0 + LANE] = acc
    y = _layer_norm(ybuf[...], lg_ref[...], lb_ref[...])
    y_ref[...] = (y * _sigmoid(y)).astype(y_ref.dtype)

    @pl.when(i == pl.num_programs(1) - 1)
    def _():
        cn_ref[...] = zbuf[CONV_HIST + TM - (CONV_W - 1):CONV_HIST + TM, :]


def _conv_prompt(proj, B, T, W, cw, cb, lg, lb):
    nt = T // TM
    return pl.pallas_call(
        _conv_p_kernel,
        grid=(B, nt),
        in_specs=[pl.BlockSpec((TM, W), lambda b, i: (b * nt + i, 0)),
                  pl.BlockSpec((TM, W), lambda b, i: (b * nt + i, 1)),
                  pl.BlockSpec((CONV_W, W), lambda b, i: (0, 0)),
                  pl.BlockSpec((1, W), lambda b, i: (0, 0)),
                  pl.BlockSpec((1, W), lambda b, i: (0, 0)),
                  pl.BlockSpec((1, W), lambda b, i: (0, 0))],
        out_specs=[pl.BlockSpec((TM, W), lambda b, i: (b * nt + i, 0)),
                   pl.BlockSpec((None, CONV_W - 1, W), lambda b, i: (b, 0, 0))],
        out_shape=[jax.ShapeDtypeStruct((B * T, W), BF16),
                   jax.ShapeDtypeStruct((B, CONV_W - 1, W), F32)],
        scratch_shapes=[pltpu.VMEM((CONV_HIST + TM, W), F32), pltpu.VMEM((TM, W), F32)],
        compiler_params=_cp(("arbitrary", "arbitrary"), 32),
        name="conv_prompt",
    )(proj, proj, cw, cb, lg, lb)


def _pool_p_kernel(p_ref, pw_ref, ps_ref, y_ref, pn_ref, ebuf):
    i = pl.program_id(1)
    W = p_ref.shape[1]

    @pl.when(i == 0)
    def _():
        ebuf[0:POOL_HIST, :] = jnp.zeros((POOL_HIST, W), F32)

    @pl.when(i > 0)
    def _():
        ebuf[0:POOL_HIST, :] = ebuf[TM:TM + POOL_HIST, :]

    ebuf[POOL_HIST:POOL_HIST + TM, :] = p_ref[...]
    t_abs = i * TM + lax.broadcasted_iota(jnp.int32, (TM, 1), 0)
    for gi, win in enumerate(POOL_WINDOWS):
        c0 = gi * LANE
        s = ebuf[POOL_HIST:POOL_HIST + TM, c0:c0 + LANE]
        for k in range(1, win):
            s = s + ebuf[pl.ds(POOL_HIST - k, TM), c0:c0 + LANE]
        cnt = jnp.minimum(t_abs + 1, win).astype(F32)
        d = s / cnt - p_ref[:, c0:c0 + LANE]
        yg = _dot(d.astype(BF16), pw_ref[gi].astype(BF16)) * ps_ref[:, c0:c0 + LANE]
        y_ref[:, c0:c0 + LANE] = yg.astype(y_ref.dtype)

    @pl.when(i == pl.num_programs(1) - 1)
    def _():
        pn_ref[...] = ebuf[POOL_HIST + TM - POOL_KEEP:POOL_HIST + TM, :]


def _pool_prompt(proj, B, T, W, pw, ps):
    nt = T // TM
    G = len(POOL_WINDOWS)
    return pl.pallas_call(
        _pool_p_kernel,
        grid=(B, nt),
        in_specs=[pl.BlockSpec((TM, W), lambda b, i: (b * nt + i, 2)),
                  pl.BlockSpec((G, LANE, LANE), lambda b, i: (0, 0, 0)),
                  pl.BlockSpec((1, W), lambda b, i: (0, 0))],
        out_specs=[pl.BlockSpec((TM, W), lambda b, i: (b * nt + i, 0)),
                   pl.BlockSpec((None, POOL_KEEP, W), lambda b, i: (b, 0, 0))],
        out_shape=[jax.ShapeDtypeStruct((B * T, W), BF16),
                   jax.ShapeDtypeStruct((B, POOL_KEEP, W), F32)],
        scratch_shapes=[pltpu.VMEM((POOL_HIST + TM, W), F32)],
        compiler_params=_cp(("arbitrary", "arbitrary"), 32),
        name="pool_prompt",
    )(proj, pw, ps)


def _norm_rope_head(xh, g, cos, sa, sb):
    ms = jnp.mean(xh * xh, -1, keepdims=True)
    xn = xh * lax.rsqrt(ms + EPS) * g
    hd = xh.shape[1]
    half = hd // 8
    up = pltpu.roll(xn, hd - half, axis=1)
    dn = pltpu.roll(xn, half, axis=1)
    return xn * cos + up * sa + dn * sb


def _qkv_p_kernel(q_ref, k_ref, v_ref, qg_ref, kg_ref, cos_ref, sa_ref, sb_ref,
                  qo_ref, ko_ref, vo_ref, kb_ref, vb_ref, km_ref):
    i = pl.program_id(1)
    cos, sa, sb = cos_ref[...], sa_ref[...], sb_ref[...]

    @pl.when(i == 0)
    def _():
        km_ref[...] = jnp.zeros(km_ref.shape, F32)

    blk_row = lax.broadcasted_iota(jnp.int32, (km_ref.shape[0], LANE), 0)
    for h in range(N_HEADS):
        c0 = h * LANE
        qo_ref[:, c0:c0 + LANE] = _norm_rope_head(q_ref[:, c0:c0 + LANE], qg_ref[...], cos, sa, sb)
        kh = _norm_rope_head(k_ref[:, c0:c0 + LANE], kg_ref[...], cos, sa, sb)
        ko_ref[:, c0:c0 + LANE] = kh
        kb_ref[:, c0:c0 + LANE] = kh.astype(BF16)
        km_ref[:, c0:c0 + LANE] = jnp.where(blk_row == i, jnp.mean(kh, axis=0, keepdims=True),
                                            km_ref[:, c0:c0 + LANE])
    v = v_ref[...]
    vo_ref[...] = v
    vb_ref[...] = v.astype(BF16)


def _qkv_prompt(proj, B, T, W, qg, kg, cos, sa, sb):
    nt = T // TM
    row = lambda b, i: (b * nt + i, 0)
    return pl.pallas_call(
        _qkv_p_kernel,
        grid=(B, nt),
        in_specs=[pl.BlockSpec((TM, W), lambda b, i: (b * nt + i, 3)),
                  pl.BlockSpec((TM, W), lambda b, i: (b * nt + i, 4)),
                  pl.BlockSpec((TM, W), lambda b, i: (b * nt + i, 5)),
                  pl.BlockSpec((1, LANE), lambda b, i: (0, 0)),
                  pl.BlockSpec((1, LANE), lambda b, i: (0, 0)),
                  pl.BlockSpec((TM, LANE), lambda b, i: (i, 0)),
                  pl.BlockSpec((TM, LANE), lambda b, i: (i, 0)),
                  pl.BlockSpec((TM, LANE), lambda b, i: (i, 0))],
        out_specs=[pl.BlockSpec((TM, W), row), pl.BlockSpec((TM, W), row), pl.BlockSpec((TM, W), row),
                   pl.BlockSpec((TM, W), row), pl.BlockSpec((TM, W), row),
                   pl.BlockSpec((nt, W), lambda b, i: (b, 0))],
        out_shape=[jax.ShapeDtypeStruct((B * T, W), F32), jax.ShapeDtypeStruct((B * T, W), F32),
                   jax.ShapeDtypeStruct((B * T, W), F32), jax.ShapeDtypeStruct((B * T, W), BF16),
                   jax.ShapeDtypeStruct((B * T, W), BF16), jax.ShapeDtypeStruct((B * nt, W), F32)],
        compiler_params=_cp(("arbitrary", "arbitrary"), 32),
        name="qkv_prompt",
    )(proj, proj, proj, qg, kg, cos, sa, sb)


def _topk_block_cols(bs, past):
    nb = bs.shape[1]
    jidx = lax.broadcasted_iota(jnp.int32, bs.shape, 1)
    cols = []
    for n in range(nb):
        bn = bs[:, n:n + 1]
        beats = ((bs > bn) | ((bs == bn) & (jidx < n))) & past
        rank = jnp.sum(beats.astype(F32), axis=-1, keepdims=True)
        cols.append(rank < (MOBA_TOPK - 0.5))
    return cols


def _attn_p_kernel(q_ref, k_ref, v_ref, km_ref, o_ref):
    i = pl.program_id(1)
    nb = km_ref.shape[0]
    scale = LANE ** -0.5
    row = lax.broadcasted_iota(jnp.int32, (TM, MOBA_BLOCK), 0)
    col = lax.broadcasted_iota(jnp.int32, (TM, MOBA_BLOCK), 1)
    tri = col <= row
    blk = lax.broadcasted_iota(jnp.int32, (TM, nb), 1)
    own0 = pl.multiple_of(i * MOBA_BLOCK, MOBA_BLOCK)
    for h in range(N_HEADS):
        hs = slice(h * LANE, (h + 1) * LANE)
        q = q_ref[:, hs]
        bs = _dot_nt(q, km_ref[:, hs], precision=lax.Precision.HIGHEST)
        sel = _topk_block_cols(bs, blk < i)
        selm = jnp.zeros((TM, nb), F32)
        for n in range(nb):
            selm = jnp.where((blk == n) & sel[n], 1.0, selm)
        qb = q.astype(BF16)
        s = _dot_nt(qb, k_ref[pl.ds(own0, MOBA_BLOCK), hs]) * scale
        s = jnp.where(tri, s, -jnp.inf)
        m = jnp.max(s, -1, keepdims=True)
        p = jnp.exp(s - m)
        den = jnp.sum(p, -1, keepdims=True)
        acc = _dot(p.astype(BF16), v_ref[pl.ds(own0, MOBA_BLOCK), hs])

        def past_block(j, carry, hs=hs, qb=qb, selm=selm):
            m, den, acc = carry
            j0 = pl.multiple_of(j * MOBA_BLOCK, MOBA_BLOCK)
            s = _dot_nt(qb, k_ref[pl.ds(j0, MOBA_BLOCK), hs]) * scale
            sel_j = jnp.sum(jnp.where(blk == j, selm, 0.0), -1, keepdims=True) > 0.5
            s = jnp.where(sel_j, s, -jnp.inf)
            m_new = jnp.maximum(m, jnp.max(s, -1, keepdims=True))
            a = jnp.exp(m - m_new)
            p = jnp.exp(s - m_new)
            return (m_new, den * a + jnp.sum(p, -1, keepdims=True),
                    acc * a + _dot(p.astype(BF16), v_ref[pl.ds(j0, MOBA_BLOCK), hs]))

        m, den, acc = lax.fori_loop(0, i, past_block, (m, den, acc))
        o_ref[:, hs] = (acc / den).astype(o_ref.dtype)


def _attn_prompt(q, kb, vb, kmean, B, T, W):
    nt = T // TM
    assert TM == MOBA_BLOCK
    return pl.pallas_call(
        _attn_p_kernel,
        grid=(B, nt),
        in_specs=[pl.BlockSpec((TM, W), lambda b, i: (b * nt + i, 0)),
                  pl.BlockSpec((T, W), lambda b, i: (b, 0)),
                  pl.BlockSpec((T, W), lambda b, i: (b, 0)),
                  pl.BlockSpec((nt, W), lambda b, i: (b, 0))],
        out_specs=pl.BlockSpec((TM, W), lambda b, i: (b * nt + i, 0)),
        out_shape=jax.ShapeDtypeStruct((B * T, W), BF16),
        compiler_params=_cp(("arbitrary", "arbitrary"), 40),
        name="attn_prompt",
    )(q, kb, vb, kmean)


def _gmlp_p_kernel(u_ref, v_ref, lg_ref, lb_ref, ws_ref, bt_ref, y_ref, gv_ref, vbuf):
    i = pl.program_id(1)
    vbuf[...] = _layer_norm(v_ref[...], lg_ref[...], lb_ref[...])
    row = lax.broadcasted_iota(jnp.int32, (GMLP_CHUNK, GMLP_CHUNK), 0)
    col = lax.broadcasted_iota(jnp.int32, (GMLP_CHUNK, GMLP_CHUNK), 1)
    tri = col <= row
    for g in range(GMLP_GROUPS):
        c0 = g * LANE
        w = jnp.where(tri, ws_ref[g], 0.0).astype(BF16)
        for r0 in range(0, TM, GMLP_CHUNK):
            mixed = _dot(w, vbuf[r0:r0 + GMLP_CHUNK, c0:c0 + LANE].astype(BF16)) + bt_ref[:, g:g + 1]
            y_ref[r0:r0 + GMLP_CHUNK, c0:c0 + LANE] = (
                u_ref[r0:r0 + GMLP_CHUNK, c0:c0 + LANE] * mixed).astype(y_ref.dtype)

    @pl.when(i == pl.num_programs(1) - 1)
    def _():
        gv_ref[...] = vbuf[TM - GMLP_CHUNK:TM, :]


def _gmlp_prompt(proj, B, T, W, lg, lb, ws, bt):
    nt = T // TM
    return pl.pallas_call(
        _gmlp_p_kernel,
        grid=(B, nt),
        in_specs=[pl.BlockSpec((TM, W), lambda b, i: (b * nt + i, 6)),
                  pl.BlockSpec((TM, W), lambda b, i: (b * nt + i, 7)),
                  pl.BlockSpec((1, W), lambda b, i: (0, 0)),
                  pl.BlockSpec((1, W), lambda b, i: (0, 0)),
                  pl.BlockSpec((GMLP_GROUPS, GMLP_CHUNK, GMLP_CHUNK), lambda b, i: (0, 0, 0)),
                  pl.BlockSpec((GMLP_CHUNK, GMLP_GROUPS), lambda b, i: (0, 0))],
        out_specs=[pl.BlockSpec((TM, W), lambda b, i: (b * nt + i, 0)),
                   pl.BlockSpec((None, GMLP_CHUNK, W), lambda b, i: (b, 0, 0))],
        out_shape=[jax.ShapeDtypeStruct((B * T, W), BF16),
                   jax.ShapeDtypeStruct((B, GMLP_CHUNK, W), F32)],
        scratch_shapes=[pltpu.VMEM((TM, W), F32)],
        compiler_params=_cp(("arbitrary", "arbitrary"), 32),
        name="gmlp_prompt",
    )(proj, proj, lg, lb, ws, bt)


def _sample_mix_kernel(p_ref, sc_ref, sp_ref, cw_ref, cb_ref, clg_ref, clb_ref, pw_ref, ps_ref,
                       qg_ref, kg_ref, cos_ref, sa_ref, sb_ref, glg_ref, glb_ref, ws_ref, gb_ref,
                       ya_ref, yb_ref, yd_ref, q_ref, k_ref, v_ref, cn_ref, pn_ref, gv_ref):
    ts, sb_rows, _ = p_ref.shape
    W = ya_ref.shape[2]
    n_conv = CONV_W - 1

    def col(c):
        return slice(c * W, (c + 1) * W)

    z = [p_ref[t, :, col(0)] * _sigmoid(p_ref[t, :, col(1)]) for t in range(ts)]

    def zext(r):
        return sc_ref[r] if r < n_conv else z[r - n_conv]

    for t in range(ts):
        acc = jnp.broadcast_to(cb_ref[...], (sb_rows, W))
        for j in range(CONV_W):
            acc = acc + cw_ref[j:j + 1, :] * zext(t + j)
        y = _layer_norm(acc, clg_ref[...], clb_ref[...])
        ya_ref[t] = (y * _sigmoid(y)).astype(ya_ref.dtype)
    for r in range(n_conv):
        cn_ref[r] = zext(r + ts)

    def pext(r, c0):
        if r < POOL_KEEP:
            return sp_ref[r, :, c0:c0 + LANE]
        return p_ref[r - POOL_KEEP, :, 2 * W + c0:2 * W + c0 + LANE]

    for t in range(ts):
        for gi, win in enumerate(POOL_WINDOWS):
            c0 = gi * LANE
            s = pext(POOL_KEEP + t, c0)
            for k in range(1, win):
                s = s + pext(POOL_KEEP + t - k, c0)
            d = s / float(win) - pext(POOL_KEEP + t, c0)
            yg = _dot(d.astype(BF16), pw_ref[gi].astype(BF16)) * ps_ref[:, c0:c0 + LANE]
            yb_ref[t, :, c0:c0 + LANE] = yg.astype(yb_ref.dtype)
    for r in range(POOL_KEEP):
        pn_ref[r] = sp_ref[r + ts] if r + ts < POOL_KEEP else p_ref[r + ts - POOL_KEEP, :, col(2)]

    for t in range(ts):
        cos = cos_ref[t:t + 1, :]
        sa = sa_ref[t:t + 1, :]
        sb = sb_ref[t:t + 1, :]
        for h in range(N_HEADS):
            c0 = h * LANE
            q_ref[t, :, c0:c0 + LANE] = _norm_rope_head(
                p_ref[t, :, 3 * W + c0:3 * W + c0 + LANE], qg_ref[...], cos, sa, sb)
            k_ref[t, :, c0:c0 + LANE] = _norm_rope_head(
                p_ref[t, :, 4 * W + c0:4 * W + c0 + LANE], kg_ref[...], cos, sa, sb)
        v_ref[t] = p_ref[t, :, col(5)]

    vn = [_layer_norm(p_ref[t, :, col(7)], glg_ref[...], glb_ref[...]) for t in range(ts)]
    for t in range(ts):
        gv_ref[t] = vn[t]
        for g in range(GMLP_GROUPS):
            c0 = g * LANE
            mixed = jnp.broadcast_to(gb_ref[g:g + 1, t:t + 1], (sb_rows, LANE))
            for s_ in range(t + 1):
                mixed = mixed + ws_ref[g, t:t + 1, s_:s_ + 1] * vn[s_][:, c0:c0 + LANE]
            yd_ref[t, :, c0:c0 + LANE] = (
                p_ref[t, :, 6 * W + c0:6 * W + c0 + LANE] * mixed).astype(yd_ref.dtype)


SAMPLE_SEQ_BLOCK = 32


def _sample_mix(proj_s, nb, ts, W, sc_t, sp_t, cw, cb, clg, clb, pw, ps, qg, kg, cos, sa, sb,
                glg, glb, ws, gb):
    G = len(POOL_WINDOWS)
    sblk = SAMPLE_SEQ_BLOCK
    full2 = lambda shape: pl.BlockSpec(shape, lambda i: (0, 0))
    full3 = lambda shape: pl.BlockSpec(shape, lambda i: (0, 0, 0))
    seq3 = lambda rows, c: pl.BlockSpec((rows, sblk, c), lambda i: (0, i, 0))
    tok = lambda dt: jax.ShapeDtypeStruct((ts, nb, W), dt)
    return pl.pallas_call(
        _sample_mix_kernel,
        grid=(nb // sblk,),
        in_specs=[seq3(ts, 8 * W), seq3(CONV_W - 1, W), seq3(POOL_KEEP, W),
                  full2((CONV_W, W)), full2((1, W)), full2((1, W)), full2((1, W)),
                  full3((G, LANE, LANE)), full2((1, W)),
                  full2((1, LANE)), full2((1, LANE)),
                  full2((8, LANE)), full2((8, LANE)), full2((8, LANE)),
                  full2((1, W)), full2((1, W)),
                  full3((GMLP_GROUPS, GMLP_CHUNK, GMLP_CHUNK)), full2((GMLP_GROUPS, GMLP_CHUNK))],
        out_specs=[seq3(ts, W)] * 6 + [seq3(CONV_W - 1, W), seq3(POOL_KEEP, W), seq3(ts, W)],
        out_shape=[tok(BF16), tok(BF16), tok(BF16), tok(F32), tok(F32), tok(F32),
                   jax.ShapeDtypeStruct((CONV_W - 1, nb, W), F32),
                   jax.ShapeDtypeStruct((POOL_KEEP, nb, W), F32), tok(F32)],
        compiler_params=_cp(("arbitrary",), 40),
        name="sample_mix",
    )(proj_s, sc_t, sp_t, cw, cb, clg, clb, pw, ps, qg, kg, cos, sa, sb, glg, glb, ws, gb)


def _attn_s_kernel(pt_ref, q_ref, kn_ref, vn_ref, *refs, n_pages, ts):
    kp = refs[:n_pages]
    vp = refs[n_pages:2 * n_pages]
    o_ref = refs[2 * n_pages]
    kbuf, vbuf = refs[2 * n_pages + 1:]
    page = kp[0].shape[0]
    W = q_ref.shape[1]
    hd = W // N_HEADS
    R = q_ref.shape[0]
    ppb = MOBA_BLOCK // page
    nblk = n_pages // ppb
    row = lax.broadcasted_iota(jnp.int32, (R, W), 0)
    lane = lax.broadcasted_iota(jnp.int32, (R, W), 1)
    head_mask = (lane >= (row >> 3) * hd) & (lane < ((row >> 3) + 1) * hd)
    qbd = jnp.where(head_mask, q_ref[...], 0.0)
    kms = []
    for j in range(nblk):
        heads = []
        for h in range(N_HEADS):
            acc = None
            for u in range(ppb):
                pg = j * ppb + u
                kh = kp[pg][:, h, :]
                kbuf[pg * page:(pg + 1) * page, h * hd:(h + 1) * hd] = kh.astype(BF16)
                vbuf[pg * page:(pg + 1) * page, h * hd:(h + 1) * hd] = vp[pg][:, h, :].astype(BF16)
                part = jnp.sum(kh, axis=0, keepdims=True)
                acc = part if acc is None else acc + part
            heads.append(acc * (1.0 / MOBA_BLOCK))
        kms.append(jnp.concatenate(heads, axis=1))
    km = jnp.concatenate(kms, axis=0)
    bs = _dot_nt(qbd, km, precision=lax.Precision.HIGHEST)
    sel = _topk_block_cols(bs, lax.broadcasted_iota(jnp.int32, bs.shape, 1) >= 0)
    scale = hd ** -0.5
    qb = qbd.astype(BF16)
    s_all = _dot_nt(qb, kbuf[...]) * scale
    s_past = jnp.concatenate(
        [jnp.where(sel[j], s_all[:, j * MOBA_BLOCK:(j + 1) * MOBA_BLOCK], -jnp.inf) for j in range(nblk)],
        axis=-1)
    t_row = lax.broadcasted_iota(jnp.int32, (R, 1), 0) & 7
    s_own = []
    for u in range(ts):
        su = jnp.sum(qbd * kn_ref[u:u + 1, :], axis=-1, keepdims=True) * scale
        s_own.append(jnp.where(t_row >= u, su, -jnp.inf))
    m = jnp.max(s_past, -1, keepdims=True)
    for su in s_own:
        m = jnp.maximum(m, su)
    p_past = jnp.exp(s_past - m)
    den = jnp.sum(p_past, -1, keepdims=True)
    acc = _dot(p_past.astype(BF16), vbuf[...])
    for u in range(ts):
        pu = jnp.exp(s_own[u] - m)
        den = den + pu
        acc = acc + pu * vn_ref[u:u + 1, :]
    acc = jnp.where(head_mask, acc / den, 0.0)
    out = acc[0:8]
    for h in range(1, N_HEADS):
        out = out + acc[8 * h:8 * h + 8]
    o_ref[...] = out


def _attn_sample(pt_flat, q_rep, k_new, v_new, ck, cv, l, nb, ts, n_pages, W):
    page, nh, hd = ck.shape[2:]
    R = q_rep.shape[1]

    def page_spec(j):
        return pl.BlockSpec((None, None, page, nh, hd), lambda b, pt: (l, pt[b * n_pages + j], 0, 0, 0))

    in_specs = ([pl.BlockSpec((None, R, W), lambda b, pt: (b, 0, 0)),
                 pl.BlockSpec((None, 8, W), lambda b, pt: (b, 0, 0)),
                 pl.BlockSpec((None, 8, W), lambda b, pt: (b, 0, 0))]
                + [page_spec(j) for j in range(n_pages)] + [page_spec(j) for j in range(n_pages)])
    return pl.pallas_call(
        functools.partial(_attn_s_kernel, n_pages=n_pages, ts=ts),
        grid_spec=pltpu.PrefetchScalarGridSpec(
            num_scalar_prefetch=1, grid=(nb,), in_specs=in_specs,
            out_specs=pl.BlockSpec((None, 8, W), lambda b, pt: (b, 0, 0)),
            scratch_shapes=[pltpu.VMEM((n_pages * page, W), BF16), pltpu.VMEM((n_pages * page, W), BF16)]),
        out_shape=jax.ShapeDtypeStruct((nb, 8, W), F32),
        compiler_params=_cp(("arbitrary",), 48),
        name="attn_sample",
    )(pt_flat, q_rep, k_new, v_new, *([ck] * n_pages), *([cv] * n_pages))


def _merge_kernel(ya_ref, yb_ref, yc_ref, yd_ref, wa_ref, wb_ref, wc_ref, wd_ref,
                  ga_ref, gb_ref, gc_ref, gd_ref, o_ref, wbf_ref):
    w_refs = (wa_ref, wb_ref, wc_ref, wd_ref)

    @pl.when(pl.program_id(1) == 0)
    def _():
        for b in range(N_BRANCH):
            wbf_ref[b] = w_refs[b][...].astype(BF16)

    acc = None
    for b, (y_ref, g_ref) in enumerate(zip((ya_ref, yb_ref, yc_ref, yd_ref),
                                           (ga_ref, gb_ref, gc_ref, gd_ref))):
        term = _sigmoid(g_ref[...]) * _dot(y_ref[...], wbf_ref[b])
        acc = term if acc is None else acc + term
    o_ref[...] = acc.astype(o_ref.dtype)


def _merge(ys, ws, proj, l, W, D):
    M = proj.shape[0]
    tm, tn = 512, 1024
    npb = D // tn
    gate0 = 8 * W // tn
    y_spec = pl.BlockSpec((tm, W), lambda n, m: (m, 0))
    w_spec = pl.BlockSpec((None, W, tn), lambda n, m: (l, 0, n))
    g_specs = [pl.BlockSpec((tm, tn), functools.partial(lambda n, m, b: (m, gate0 + b * npb + n), b=b))
               for b in range(N_BRANCH)]
    return pl.pallas_call(
        _merge_kernel,
        grid=(npb, M // tm),
        in_specs=[y_spec] * 4 + [w_spec] * 4 + g_specs,
        out_specs=pl.BlockSpec((tm, tn), lambda n, m: (m, n)),
        out_shape=jax.ShapeDtypeStruct((M, D), BF16),
        scratch_shapes=[pltpu.VMEM((N_BRANCH, W, tn), BF16)],
        compiler_params=_cp(("arbitrary", "arbitrary"), 48),
        name="merge",
    )(*ys, *ws, proj, proj, proj, proj)


def _wo_kernel(a_ref, w_ref, x_ref, g_ref, o_ref, wbf_ref):
    @pl.when(pl.program_id(1) == 0)
    def _():
        wbf_ref[...] = w_ref[...].astype(BF16)

    o_ref[...] = x_ref[...] + g_ref[...] * _dot(a_ref[...], wbf_ref[...])


def _wo_residual(merged, w_o, l, x, mr, gate_chunk, seq_of_tile):
    M, D = x.shape
    tn = 1024
    ncb = D // tn
    return pl.pallas_call(
        _wo_kernel,
        grid=(ncb, M // TM),
        in_specs=[pl.BlockSpec((TM, D), lambda n, m: (m, 0)),
                  pl.BlockSpec((None, D, tn), lambda n, m: (l, 0, n)),
                  pl.BlockSpec((TM, tn), lambda n, m: (m, n)),
                  pl.BlockSpec((None, TM, tn), lambda n, m: (seq_of_tile(m), 0, gate_chunk * ncb + n))],
        out_specs=pl.BlockSpec((TM, tn), lambda n, m: (m, n)),
        out_shape=jax.ShapeDtypeStruct((M, D), F32),
        scratch_shapes=[pltpu.VMEM((D, tn), BF16)],
        compiler_params=_cp(("arbitrary", "arbitrary"), 40),
        name="wo_residual",
    )(merged, w_o, x, mr)


def _pack_bf16_pairs(x):
    n = x.shape[1] // 2
    lo = lax.bitcast_convert_type(x[:, :n].astype(BF16).astype(F32), jnp.uint32)
    hi = lax.bitcast_convert_type(x[:, n:].astype(BF16).astype(F32), jnp.uint32)
    return hi | (lo >> 16)


def _unpack_bf16_pairs(w):
    lo = lax.bitcast_convert_type(w << 16, F32).astype(BF16)
    hi = lax.bitcast_convert_type(w & jnp.uint32(0xFFFF0000), F32).astype(BF16)
    return lo, hi


def _norm_router_kernel(x_ref, g_ref, sc_ref, sh_ref, rw_ref, rb_ref,
                        h_ref, idx_ref, gate_ref, rank_ref, cnt_ref, run_ref):
    i = pl.program_id(0)
    E = rw_ref.shape[1]

    @pl.when(i == 0)
    def _():
        run_ref[...] = jnp.zeros(run_ref.shape, F32)

    x = x_ref[...]
    ms = jnp.mean(x * x, -1, keepdims=True)
    y = x * lax.rsqrt(ms + EPS) * g_ref[...]
    h = y * (1.0 + sc_ref[...]) + sh_ref[...]
    h_ref[...] = _pack_bf16_pairs(h)
    logits = jnp.dot(h, rw_ref[...], precision=lax.Precision.HIGHEST,
                     preferred_element_type=F32) + rb_ref[...]

    eidx = lax.broadcasted_iota(jnp.int32, (TM, E), 1).astype(F32)
    kcol = lax.broadcasted_iota(jnp.int32, (TM, TOP_K), 1)
    r_i = lax.broadcasted_iota(jnp.int32, (TM, TM), 0)
    c_i = lax.broadcasted_iota(jnp.int32, (TM, TM), 1)
    before = jnp.where(c_i < r_i, 1.0, 0.0).astype(BF16)
    work = logits
    vals, hots = [], []
    idx_out = jnp.zeros((TM, TOP_K), F32)
    rank_out = jnp.zeros((TM, TOP_K), F32)
    run = run_ref[...]
    for k in range(TOP_K):
        mx = jnp.max(work, -1, keepdims=True)
        am = jnp.min(jnp.where(work == mx, eidx, float(E)), -1, keepdims=True)
        hot = eidx == am
        work = jnp.where(hot, -jnp.inf, work)
        hot_f = jnp.where(hot, 1.0, 0.0)
        earlier = _dot(before, hot_f.astype(BF16))
        rank = jnp.sum(hot_f * (earlier + run), -1, keepdims=True)
        run = run + jnp.sum(hot_f, axis=0, keepdims=True)
        vals.append(mx)
        idx_out = jnp.where(kcol == k, am, idx_out)
        rank_out = jnp.where(kcol == k, rank, rank_out)
    run_ref[...] = run
    cnt_ref[...] = run
    den = jnp.zeros((TM, 1), F32)
    gate_out = jnp.zeros((TM, TOP_K), F32)
    ex = [jnp.exp(v - vals[0]) for v in vals]
    for e_ in ex:
        den = den + e_
    for k in range(TOP_K):
        gate_out = jnp.where(kcol == k, ex[k] / den, gate_out)
    idx_ref[...] = idx_out.astype(jnp.int32)
    rank_ref[...] = rank_out.astype(jnp.int32)
    gate_ref[...] = gate_out


def _norm_router(x, g, mr, sc_chunk, sh_chunk, seq_of_tile, rw, rb):
    M, D = x.shape
    E = rw.shape[1]
    tok = lambda c: pl.BlockSpec((TM, c), lambda i: (i, 0))
    return pl.pallas_call(
        _norm_router_kernel,
        grid=(M // TM,),
        in_specs=[tok(D),
                  pl.BlockSpec((1, D), lambda i: (0, 0)),
                  pl.BlockSpec((None, TM, D), lambda i: (seq_of_tile(i), 0, sc_chunk)),
                  pl.BlockSpec((None, TM, D), lambda i: (seq_of_tile(i), 0, sh_chunk)),
                  pl.BlockSpec((D, E), lambda i: (0, 0)),
                  pl.BlockSpec((1, E), lambda i: (0, 0))],
        out_specs=[tok(D // 2), tok(TOP_K), tok(TOP_K), tok(TOP_K), pl.BlockSpec((1, E), lambda i: (0, 0))],
        out_shape=[jax.ShapeDtypeStruct((M, D // 2), jnp.uint32), jax.ShapeDtypeStruct((M, TOP_K), jnp.int32),
                   jax.ShapeDtypeStruct((M, TOP_K), F32), jax.ShapeDtypeStruct((M, TOP_K), jnp.int32),
                   jax.ShapeDtypeStruct((1, E), F32)],
        scratch_shapes=[pltpu.VMEM((1, E), F32)],
        compiler_params=_cp(("arbitrary",), 40),
        name="norm_router",
    )(x, g, mr, mr, rw, rb)


def _row_copy(src_hbm, dst, src_row, dst_row, sem):
    return pltpu.make_async_copy(src_hbm.at[pl.ds(src_row, 1)], dst.at[pl.ds(dst_row, 1)], sem)


DISPATCH_TM = 512


def _dispatch_kernel(dest_ref, h_hbm, xb_init_hbm, xb_hbm, sem):
    del xb_init_hbm
    i = pl.program_id(0)

    def issue(r, c):
        tok = i * DISPATCH_TM + r
        for k in range(TOP_K):
            _row_copy(h_hbm, xb_hbm, tok, dest_ref[tok * TOP_K + k], sem).start()
        return c

    lax.fori_loop(0, DISPATCH_TM, issue, 0)

    def drain(r, c):
        for k in range(TOP_K):
            _row_copy(h_hbm, xb_hbm, 0, 0, sem).wait()
        return c

    lax.fori_loop(0, DISPATCH_TM, drain, 0)


def _dispatch(dest_flat, h2p, cap):
    M, Dp = h2p.shape
    any_spec = pl.BlockSpec(memory_space=pl.ANY)
    return pl.pallas_call(
        _dispatch_kernel,
        grid_spec=pltpu.PrefetchScalarGridSpec(
            num_scalar_prefetch=1, grid=(M // DISPATCH_TM,),
            in_specs=[any_spec, any_spec], out_specs=any_spec,
            scratch_shapes=[pltpu.SemaphoreType.DMA(())]),
        out_shape=jax.ShapeDtypeStruct((cap, Dp), jnp.uint32),
        input_output_aliases={2: 0},
        compiler_params=_cp(("arbitrary",), 16),
        name="moe_dispatch",
    )(dest_flat, h2p, jnp.zeros((cap, Dp), jnp.uint32))


def _moe_up_kernel(be_ref, nu_ref, x_ref, wg_ref, wl_ref, bg_ref, bl_ref, o_ref, wgb, wlb):
    m = pl.program_id(1)
    e = be_ref[m]
    prev = be_ref[jnp.maximum(m - 1, 0)]

    @pl.when((m == 0) | (e != prev))
    def _():
        wgb[...] = wg_ref[...].astype(BF16)
        wlb[...] = wl_ref[...].astype(BF16)

    @pl.when(m < nu_ref[0])
    def _():
        x_lo, x_hi = _unpack_bf16_pairs(x_ref[...])
        n = x_lo.shape[1]
        hg = _dot(x_lo, wgb[0:n, :]) + _dot(x_hi, wgb[n:2 * n, :]) + bg_ref[...]
        hl = _dot(x_lo, wlb[0:n, :]) + _dot(x_hi, wlb[n:2 * n, :]) + bl_ref[...]
        hg = jnp.minimum(hg, SWIGLU_LIMIT)
        hl = jnp.clip(hl, -SWIGLU_LIMIT, SWIGLU_LIMIT)
        o_ref[...] = (hg * _sigmoid(SWIGLU_ALPHA * hg) * (hl + 1.0)).astype(o_ref.dtype)

    @pl.when(m >= nu_ref[0])
    def _():
        o_ref[...] = jnp.zeros(o_ref.shape, o_ref.dtype)


def _moe_up(blk_expert, n_used, xb, w1, b1, l, E):
    cap, Dp = xb.shape
    D = w1.shape[1]
    F = w1.shape[2] // 2
    tn = 1024
    nf = F // tn
    n_blocks = cap // MOE_BM
    return pl.pallas_call(
        _moe_up_kernel,
        grid_spec=pltpu.PrefetchScalarGridSpec(
            num_scalar_prefetch=2, grid=(nf, n_blocks),
            in_specs=[pl.BlockSpec((MOE_BM, Dp), lambda n, m, be, nu: (m, 0)),
                      pl.BlockSpec((None, D, tn), lambda n, m, be, nu: (l * E + be[m], 0, n)),
                      pl.BlockSpec((None, D, tn), lambda n, m, be, nu: (l * E + be[m], 0, nf + n)),
                      pl.BlockSpec((None, 1, tn), lambda n, m, be, nu: (l * E + be[m], 0, n)),
                      pl.BlockSpec((None, 1, tn), lambda n, m, be, nu: (l * E + be[m], 0, nf + n))],
            out_specs=pl.BlockSpec((MOE_BM, tn), lambda n, m, be, nu: (m, n)),
            scratch_shapes=[pltpu.VMEM((D, tn), BF16), pltpu.VMEM((D, tn), BF16)]),
        out_shape=jax.ShapeDtypeStruct((cap, F), BF16),
        compiler_params=_cp(("arbitrary", "arbitrary"), 56),
        name="moe_up",
    )(blk_expert, n_used, xb, w1, w1, b1, b1)


def _moe_down_kernel(be_ref, nu_ref, a_ref, w_ref, b_ref, o_ref, wb):
    m = pl.program_id(1)
    e = be_ref[m]
    prev = be_ref[jnp.maximum(m - 1, 0)]

    @pl.when((m == 0) | (e != prev))
    def _():
        wb[...] = w_ref[...].astype(BF16)

    @pl.when(m < nu_ref[0])
    def _():
        o_ref[...] = _dot(a_ref[...], wb[...]) + b_ref[...]

    @pl.when(m >= nu_ref[0])
    def _():
        o_ref[...] = jnp.zeros(o_ref.shape, o_ref.dtype)


def _moe_down(blk_expert, n_used, act, w2, b2, l, E):
    cap, F = act.shape
    D = w2.shape[2]
    tn = D
    n_blocks = cap // MOE_BM
    return pl.pallas_call(
        _moe_down_kernel,
        grid_spec=pltpu.PrefetchScalarGridSpec(
            num_scalar_prefetch=2, grid=(D // tn, n_blocks),
            in_specs=[pl.BlockSpec((MOE_BM, F), lambda n, m, be, nu: (m, 0)),
                      pl.BlockSpec((None, F, tn), lambda n, m, be, nu: (l * E + be[m], 0, n)),
                      pl.BlockSpec((None, 1, tn), lambda n, m, be, nu: (l * E + be[m], 0, n))],
            out_specs=pl.BlockSpec((MOE_BM, tn), lambda n, m, be, nu: (m, n)),
            scratch_shapes=[pltpu.VMEM((F, tn), BF16)]),
        out_shape=jax.ShapeDtypeStruct((cap, D), F32),
        compiler_params=_cp(("arbitrary", "arbitrary"), 56),
        name="moe_down",
    )(blk_expert, n_used, act, w2, b2)


COMBINE_TM = 128


def _combine_kernel(slot_ref, y_hbm, x_ref, g_ref, gate_ref, o_ref, buf, sem):
    i = pl.program_id(0)

    def issue(r, c):
        for k in range(TOP_K):
            _row_copy(y_hbm, buf.at[k], slot_ref[(i * COMBINE_TM + r) * TOP_K + k], r, sem).start()
        return c

    lax.fori_loop(0, COMBINE_TM, issue, 0)

    def drain(r, c):
        for k in range(TOP_K):
            _row_copy(y_hbm, buf.at[k], 0, r, sem).wait()
        return c

    lax.fori_loop(0, COMBINE_TM, drain, 0)
    y = gate_ref[:, 0:1] * buf[0]
    for k in range(1, TOP_K):
        y = y + gate_ref[:, k:k + 1] * buf[k]
    o_ref[...] = x_ref[...] + g_ref[...] * y


def _combine(slot_of, yb, x, mr, gate_chunk, seq_of_tile, gate):
    M, D = x.shape
    per = TM // COMBINE_TM
    return pl.pallas_call(
        _combine_kernel,
        grid_spec=pltpu.PrefetchScalarGridSpec(
            num_scalar_prefetch=1, grid=(M // COMBINE_TM,),
            in_specs=[pl.BlockSpec(memory_space=pl.ANY),
                      pl.BlockSpec((COMBINE_TM, D), lambda i, s: (i, 0)),
                      pl.BlockSpec((None, COMBINE_TM, D), lambda i, s: (seq_of_tile(i // per), 0, gate_chunk)),
                      pl.BlockSpec((COMBINE_TM, TOP_K), lambda i, s: (i, 0))],
            out_specs=pl.BlockSpec((COMBINE_TM, D), lambda i, s: (i, 0)),
            scratch_shapes=[pltpu.VMEM((TOP_K, COMBINE_TM, D), F32), pltpu.SemaphoreType.DMA(())]),
        out_shape=jax.ShapeDtypeStruct((M, D), F32),
        compiler_params=_cp(("arbitrary",), 32),
        name="moe_combine",
    )(slot_of, yb, x, mr, gate)


def _slot_tables(top_idx, rank, counts, n_blocks):
    E = counts.shape[0]
    padded = ((counts + MOE_BM - 1) // MOE_BM) * MOE_BM
    pad_end = jnp.cumsum(padded)
    pad_start = pad_end - padded
    onehot = top_idx[..., None] == jnp.arange(E, dtype=jnp.int32)
    dest = jnp.sum(jnp.where(onehot, pad_start, 0), -1) + rank
    blk_start = jnp.arange(n_blocks, dtype=jnp.int32) * MOE_BM
    blk_expert = jnp.minimum(jnp.sum(blk_start[:, None] >= pad_end[None, :], -1), E - 1).astype(jnp.int32)
    n_used = (pad_end[-1] // MOE_BM).astype(jnp.int32).reshape(1)
    return dest.reshape(-1).astype(jnp.int32), blk_expert, n_used


def _rope_tables(pos, hd):
    rot = hd // 4
    half = rot // 2
    inv = jnp.power(jnp.float32(ROPE_THETA), -jnp.arange(half, dtype=F32) * 2.0 / rot)
    ang = pos.astype(F32)[:, None] * inv[None, :]
    cos, sin = jnp.cos(ang), jnp.sin(ang)
    n = pos.shape[0]
    c = jnp.concatenate([cos, cos, jnp.ones((n, hd - rot), F32)], 1)
    sa = jnp.concatenate([-sin, jnp.zeros((n, hd - half), F32)], 1)
    sb = jnp.concatenate([jnp.zeros((n, half), F32), sin, jnp.zeros((n, hd - rot), F32)], 1)
    return c, sa, sb


def kernel(x_prompt, x_sample, cache_k, cache_v, state_conv, state_pool, page_table, c_prompt, c_sample,
           w_ada, b_ada, norm1_g, norm2_g, w_in, conv_w, conv_b, conv_ln_g, conv_ln_b, conv_out,
           pool_w, pool_scale, pool_out, q_norm_g, k_norm_g, attn_out, gmlp_ln_g, gmlp_ln_b,
           gmlp_ws, gmlp_b, gmlp_out, w_o, router_w, router_b, exp_w1, exp_b1, exp_w2, exp_b2):
    B, T, D = x_prompt.shape
    NB, TS, _ = x_sample.shape
    L = w_ada.shape[0]
    W = D // N_BRANCH
    HD = W // N_HEADS
    E = router_w.shape[2]
    n_pool, PAGE = cache_k.shape[1], cache_k.shape[2]
    n_pages = page_table.shape[1]
    past = n_pages * PAGE
    M_p, M_s = B * T, NB * TS
    M = M_p + M_s
    assert HD == LANE and W == N_HEADS * LANE and T % TM == 0 and TM % NB == 0 and NB % 8 == 0
    assert M % 512 == 0 and M_s % TM == 0 and TS <= 8 and past % MOBA_BLOCK == 0
    tiles_per_seq = T // TM

    def seq_of_tile(i):
        return jnp.minimum(i // tiles_per_seq, B)

    x = jnp.concatenate([x_prompt.reshape(M_p, D), x_sample.transpose(1, 0, 2).reshape(M_s, D)], 0)
    n_c = B + NB
    c_all = jnp.concatenate([c_prompt, c_sample, jnp.zeros((-n_c % 8, D), F32)], 0)
    mod_all = _ada(c_all, w_ada, b_ada)

    cos_p, sa_p, sb_p = _rope_tables(jnp.arange(T), HD)
    cos_s, sa_s, sb_s = [jnp.pad(t, ((0, 8 - TS), (0, 0))) for t in _rope_tables(past + jnp.arange(TS), HD)]
    pt_flat = page_table.reshape(-1).astype(jnp.int32)
    w1 = exp_w1.reshape(L * E, D, exp_w1.shape[3])
    b1 = exp_b1.reshape(L * E, 1, exp_b1.shape[2])
    w2 = exp_w2.reshape(L * E, exp_w2.shape[2], D)
    b2 = exp_b2.reshape(L * E, 1, D)

    def to_bt(a_t):
        return a_t.reshape(TS, NB, -1).transpose(1, 0, 2)

    outs = {k: [] for k in ('kp', 'vp', 'ks', 'vs', 'cp', 'cs', 'pp', 'ps', 'gp', 'gs')}
    for l in range(L):
        mod = mod_all[l]
        mr = jnp.concatenate([jnp.broadcast_to(mod[:B, None, :], (B, TM, 6 * D)),
                              jnp.tile(mod[B:B + NB], (TM // NB, 1))[None]], 0)
        h = _norm_mod(x, norm1_g[l][None], mr, 1, 0, seq_of_tile, BF16)
        proj = _panel_mm(h, w_in, l, 512, 1024)

        row = lambda v: v[l][None]
        ya_p, conv_p = _conv_prompt(proj, B, T, W, conv_w[l], row(conv_b), row(conv_ln_g), row(conv_ln_b))
        yb_p, pool_p = _pool_prompt(proj, B, T, W, pool_w[l], row(pool_scale))
        q_p, k_p, v_p, kb_p, vb_p, kmean_p = _qkv_prompt(proj, B, T, W, row(q_norm_g), row(k_norm_g),
                                                        cos_p, sa_p, sb_p)
        yc_p = _attn_prompt(q_p, kb_p, vb_p, kmean_p, B, T, W)
        yd_p, gv_p = _gmlp_prompt(proj, B, T, W, row(gmlp_ln_g), row(gmlp_ln_b), gmlp_ws[l],
                                  jnp.transpose(gmlp_b[l]))

        proj_s = proj[M_p:, :8 * W].reshape(TS, NB, 8 * W)
        (ya_s, yb_s, yd_s, q_s, k_s, v_s, conv_s, pool_s, gv_s) = _sample_mix(
            proj_s, NB, TS, W, state_conv[l].transpose(1, 0, 2), state_pool[l].transpose(1, 0, 2),
            conv_w[l], row(conv_b), row(conv_ln_g), row(conv_ln_b), pool_w[l], row(pool_scale),
            row(q_norm_g), row(k_norm_g), cos_s, sa_s, sb_s, row(gmlp_ln_g), row(gmlp_ln_b),
            gmlp_ws[l], gmlp_b[l])
        q_bt = jnp.pad(to_bt(q_s), ((0, 0), (0, 8 - TS), (0, 0)))
        q_rep = jnp.tile(q_bt, (1, N_HEADS, 1))
        k_bt = jnp.pad(to_bt(k_s), ((0, 0), (0, 8 - TS), (0, 0)))
        v_bt = jnp.pad(to_bt(v_s), ((0, 0), (0, 8 - TS), (0, 0)))
        att_s = _attn_sample(pt_flat, q_rep, k_bt, v_bt, cache_k, cache_v, l, NB, TS, n_pages, W)
        yc_s = att_s[:, :TS].transpose(1, 0, 2).reshape(M_s, W).astype(BF16)

        ys = [jnp.concatenate([p_, s_.reshape(M_s, W)], 0) for p_, s_ in
              ((ya_p, ya_s), (yb_p, yb_s), (yc_p, yc_s), (yd_p, yd_s))]
        merged = _merge(ys, (conv_out, pool_out, attn_out, gmlp_out), proj, l, W, D)
        x = _wo_residual(merged, w_o, l, x, mr, 2, seq_of_tile)
        h2p, top_idx, gate, rank, counts = _norm_router(x, norm2_g[l][None], mr, 4, 3, seq_of_tile,
                                                         router_w[l], router_b[l][None])
        n_blocks = -(-M * TOP_K // MOE_BM) + E
        dest, blk_expert, n_used = _slot_tables(top_idx, rank, counts[0].astype(jnp.int32), n_blocks)
        xb = _dispatch(dest, h2p, n_blocks * MOE_BM)
        act = _moe_up(blk_expert, n_used, xb, w1, b1, l, E)
        yb = _moe_down(blk_expert, n_used, act, w2, b2, l, E)
        x = _combine(dest, yb, x, mr, 5, seq_of_tile, gate)

        outs['kp'].append(k_p.reshape(B, T, N_HEADS, HD))
        outs['vp'].append(v_p.reshape(B, T, N_HEADS, HD))
        outs['ks'].append(to_bt(k_s).reshape(NB, TS, N_HEADS, HD))
        outs['vs'].append(to_bt(v_s).reshape(NB, TS, N_HEADS, HD))
        outs['cp'].append(conv_p)
        outs['cs'].append(conv_s.transpose(1, 0, 2))
        outs['pp'].append(pool_p)
        outs['ps'].append(pool_s.transpose(1, 0, 2))
        outs['gp'].append(gv_p)
        outs['gs'].append(to_bt(gv_s))

    y_prompt = x[:M_p].reshape(B, T, D)
    y_sample = to_bt(x[M_p:])
    st = lambda k: jnp.stack(outs[k])
    return (y_prompt, y_sample, st('kp'), st('vp'), st('ks'), st('vs'), st('cp'), st('cs'),
            st('pp'), st('ps'), st('gp'), st('gs'))
```

```python
import functools

import jax
import jax.numpy as jnp
from jax import lax
from jax.experimental import pallas as pl
from jax.experimental.pallas import tpu as pltpu

F32 = jnp.float32
BF16 = jnp.bfloat16

N_BRANCH = 4
CONV_W = 31
POOL_WINDOWS = (2, 4, 8, 16)
POOL_KEEP = max(POOL_WINDOWS) - 1
N_HEADS = 4
ROPE_THETA = 500000.0
MOBA_BLOCK = 256
MOBA_TOPK = 3
GMLP_CHUNK = 128
GMLP_GROUPS = 4
TOP_K = 4
SWIGLU_LIMIT = 7.0
SWIGLU_ALPHA = 1.702
EPS = 1e-6

LANE = 128
TM = 256
CONV_HIST = 32
POOL_HIST = 16
MOE_BM = 256
MIB = 1024 * 1024


def _cp(sem, vmem_mib):
    return pltpu.CompilerParams(dimension_semantics=sem, vmem_limit_bytes=vmem_mib * MIB)


def _sigmoid(x):
    return 1.0 / (1.0 + jnp.exp(-x))


def _layer_norm(x, g, b):
    mu = jnp.mean(x, -1, keepdims=True)
    xc = x - mu
    var = jnp.mean(xc * xc, -1, keepdims=True)
    return xc * lax.rsqrt(var + EPS) * g + b


def _dot(a, b):
    return jnp.dot(a, b, preferred_element_type=F32)


def _dot_nt(a, b, precision=None):
    return lax.dot_general(a, b, (((1,), (1,)), ((), ())), precision=precision,
                           preferred_element_type=F32)


def _ada_kernel(c_ref, w_ref, b_ref, o_ref):
    c = c_ref[...]
    s = (c * _sigmoid(c)).astype(BF16)
    o_ref[...] = _dot(s, w_ref[...].astype(BF16)) + b_ref[...]


def _ada(c_all, w_ada, b_ada):
    L, D, N = w_ada.shape
    R = c_all.shape[0]
    tn = 1024
    return pl.pallas_call(
        _ada_kernel,
        grid=(L, N // tn),
        in_specs=[pl.BlockSpec((R, D), lambda l, n: (0, 0)),
                  pl.BlockSpec((None, D, tn), lambda l, n: (l, 0, n)),
                  pl.BlockSpec((None, 1, tn), lambda l, n: (l, 0, n))],
        out_specs=pl.BlockSpec((None, R, tn), lambda l, n: (l, 0, n)),
        out_shape=jax.ShapeDtypeStruct((L, R, N), F32),
        compiler_params=_cp(("arbitrary", "arbitrary"), 40),
        name="ada",
    )(c_all, w_ada, b_ada.reshape(L, 1, N))


def _norm_mod_kernel(x_ref, g_ref, sc_ref, sh_ref, o_ref):
    x = x_ref[...]
    ms = jnp.mean(x * x, -1, keepdims=True)
    y = x * lax.rsqrt(ms + EPS) * g_ref[...]
    o_ref[...] = (y * (1.0 + sc_ref[...]) + sh_ref[...]).astype(o_ref.dtype)


def _norm_mod(x, g, mr, sc_chunk, sh_chunk, seq_of_tile, out_dtype):
    M, D = x.shape
    return pl.pallas_call(
        _norm_mod_kernel,
        grid=(M // TM,),
        in_specs=[pl.BlockSpec((TM, D), lambda i: (i, 0)),
                  pl.BlockSpec((1, D), lambda i: (0, 0)),
                  pl.BlockSpec((None, TM, D), lambda i: (seq_of_tile(i), 0, sc_chunk)),
                  pl.BlockSpec((None, TM, D), lambda i: (seq_of_tile(i), 0, sh_chunk))],
        out_specs=pl.BlockSpec((TM, D), lambda i: (i, 0)),
        out_shape=jax.ShapeDtypeStruct((M, D), out_dtype),
        compiler_params=_cp(("arbitrary",), 40),
        name="norm_mod",
    )(x, g, mr, mr)


def _panel_mm_kernel(a_ref, w_ref, o_ref, wbf_ref):
    @pl.when(pl.program_id(1) == 0)
    def _():
        wbf_ref[...] = w_ref[...].astype(BF16)

    o_ref[...] = _dot(a_ref[...], wbf_ref[...])


def _panel_mm(a, w, l, tm, tn):
    M, K = a.shape
    N = w.shape[2]
    return pl.pallas_call(
        _panel_mm_kernel,
        grid=(N // tn, M // tm),
        in_specs=[pl.BlockSpec((tm, K), lambda n, m: (m, 0)),
                  pl.BlockSpec((None, K, tn), lambda n, m: (l, 0, n))],
        out_specs=pl.BlockSpec((tm, tn), lambda n, m: (m, n)),
        out_shape=jax.ShapeDtypeStruct((M, N), F32),
        scratch_shapes=[pltpu.VMEM((K, tn), BF16)],
        compiler_params=_cp(("arbitrary", "arbitrary"), 48),
        name="in_proj",
    )(a, w)


def _conv_p_kernel(a_ref, g_ref, w_ref, cb_ref, lg_ref, lb_ref, y_ref, cn_ref, zbuf, ybuf):
    i = pl.program_id(1)
    W = a_ref.shape[1]

    @pl.when(i == 0)
    def _():
        zbuf[0:CONV_HIST, :] = jnp.zeros((CONV_HIST, W), F32)

    @pl.when(i > 0)
    def _():
        zbuf[0:CONV_HIST, :] = zbuf[TM:TM + CONV_HIST, :]

    zbuf[CONV_HIST:CONV_HIST + TM, :] = a_ref[...] * _sigmoid(g_ref[...])
    base = CONV_HIST - (CONV_W - 1)
    rb = 128
    for c0 in range(0, W, LANE):
        for r0 in range(0, TM, rb):
            acc = jnp.broadcast_to(cb_ref[:, c0:c0 + LANE], (rb, LANE))
            for j in range(CONV_W):
                acc = acc + w_ref[j:j + 1, c0:c0 + LANE] * zbuf[pl.ds(base + r0 + j, rb), c0:c0 + LANE]
            ybuf[r0:r0 + rb, c0:c0 + LANE] = acc
    y = _layer_norm(ybuf[...], lg_ref[...], lb_ref[...])
    y_ref[...] = (y * _sigmoid(y)).astype(y_ref.dtype)

    @pl.when(i == pl.num_programs(1) - 1)
    def _():
        cn_ref[...] = zbuf[CONV_HIST + TM - (CONV_W - 1):CONV_HIST + TM, :]


def _conv_prompt(proj, B, T, W, cw, cb, lg, lb):
    nt = T // TM
    return pl.pallas_call(
        _conv_p_kernel,
        grid=(B, nt),
        in_specs=[pl.BlockSpec((TM, W), lambda b, i: (b * nt + i, 0)),
                  pl.BlockSpec((TM, W), lambda b, i: (b * nt + i, 1)),
                  pl.BlockSpec((CONV_W, W), lambda b, i: (0, 0)),
                  pl.BlockSpec((1, W), lambda b, i: (0, 0)),
                  pl.BlockSpec((1, W), lambda b, i: (0, 0)),
                  pl.BlockSpec((1, W), lambda b, i: (0, 0))],
        out_specs=[pl.BlockSpec((TM, W), lambda b, i: (b * nt + i, 0)),
                   pl.BlockSpec((None, CONV_W - 1, W), lambda b, i: (b, 0, 0))],
        out_shape=[jax.ShapeDtypeStruct((B * T, W), BF16),
                   jax.ShapeDtypeStruct((B, CONV_W - 1, W), F32)],
        scratch_shapes=[pltpu.VMEM((CONV_HIST + TM, W), F32), pltpu.VMEM((TM, W), F32)],
        compiler_params=_cp(("arbitrary", "arbitrary"), 32),
        name="conv_prompt",
    )(proj, proj, cw, cb, lg, lb)


def _pool_p_kernel(p_ref, pw_ref, ps_ref, y_ref, pn_ref, ebuf):
    i = pl.program_id(1)
    W = p_ref.shape[1]

    @pl.when(i == 0)
    def _():
        ebuf[0:POOL_HIST, :] = jnp.zeros((POOL_HIST, W), F32)

    @pl.when(i > 0)
    def _():
        ebuf[0:POOL_HIST, :] = ebuf[TM:TM + POOL_HIST, :]

    ebuf[POOL_HIST:POOL_HIST + TM, :] = p_ref[...]
    t_abs = i * TM + lax.broadcasted_iota(jnp.int32, (TM, 1), 0)
    for gi, win in enumerate(POOL_WINDOWS):
        c0 = gi * LANE
        s = ebuf[POOL_HIST:POOL_HIST + TM, c0:c0 + LANE]
        for k in range(1, win):
            s = s + ebuf[pl.ds(POOL_HIST - k, TM), c0:c0 + LANE]
        cnt = jnp.minimum(t_abs + 1, win).astype(F32)
        d = s / cnt - p_ref[:, c0:c0 + LANE]
        yg = _dot(d.astype(BF16), pw_ref[gi].astype(BF16)) * ps_ref[:, c0:c0 + LANE]
        y_ref[:, c0:c0 + LANE] = yg.astype(y_ref.dtype)

    @pl.when(i == pl.num_programs(1) - 1)
    def _():
        pn_ref[...] = ebuf[POOL_HIST + TM - POOL_KEEP:POOL_HIST + TM, :]


def _pool_prompt(proj, B, T, W, pw, ps):
    nt = T // TM
    G = len(POOL_WINDOWS)
    return pl.pallas_call(
        _pool_p_kernel,
        grid=(B, nt),
        in_specs=[pl.BlockSpec((TM, W), lambda b, i: (b * nt + i, 2)),
                  pl.BlockSpec((G, LANE, LANE), lambda b, i: (0, 0, 0)),
                  pl.BlockSpec((1, W), lambda b, i: (0, 0))],
        out_specs=[pl.BlockSpec((TM, W), lambda b, i: (b * nt + i, 0)),
                   pl.BlockSpec((None, POOL_KEEP, W), lambda b, i: (b, 0, 0))],
        out_shape=[jax.ShapeDtypeStruct((B * T, W), BF16),
                   jax.ShapeDtypeStruct((B, POOL_KEEP, W), F32)],
        scratch_shapes=[pltpu.VMEM((POOL_HIST + TM, W), F32)],
        compiler_params=_cp(("arbitrary", "arbitrary"), 32),
        name="pool_prompt",
    )(proj, pw, ps)


def _norm_rope_head(xh, g, cos, sa, sb):
    ms = jnp.mean(xh * xh, -1, keepdims=True)
    xn = xh * lax.rsqrt(ms + EPS) * g
    hd = xh.shape[1]
    half = hd // 8
    up = pltpu.roll(xn, hd - half, axis=1)
    dn = pltpu.roll(xn, half, axis=1)
    return xn * cos + up * sa + dn * sb


def _qkv_p_kernel(q_ref, k_ref, v_ref, qg_ref, kg_ref, cos_ref, sa_ref, sb_ref,
                  qo_ref, ko_ref, vo_ref, kb_ref, vb_ref, km_ref):
    i = pl.program_id(1)
    cos, sa, sb = cos_ref[...], sa_ref[...], sb_ref[...]

    @pl.when(i == 0)
    def _():
        km_ref[...] = jnp.zeros(km_ref.shape, F32)

    blk_row = lax.broadcasted_iota(jnp.int32, (km_ref.shape[0], LANE), 0)
    for h in range(N_HEADS):
        c0 = h * LANE
        qo_ref[:, c0:c0 + LANE] = _norm_rope_head(q_ref[:, c0:c0 + LANE], qg_ref[...], cos, sa, sb)
        kh = _norm_rope_head(k_ref[:, c0:c0 + LANE], kg_ref[...], cos, sa, sb)
        ko_ref[:, c0:c0 + LANE] = kh
        kb_ref[:, c0:c0 + LANE] = kh.astype(BF16)
        km_ref[:, c0:c0 + LANE] = jnp.where(blk_row == i, jnp.mean(kh, axis=0, keepdims=True),
                                            km_ref[:, c0:c0 + LANE])
    v = v_ref[...]
    vo_ref[...] = v
    vb_ref[...] = v.astype(BF16)


def _qkv_prompt(proj, B, T, W, qg, kg, cos, sa, sb):
    nt = T // TM
    row = lambda b, i: (b * nt + i, 0)
    return pl.pallas_call(
        _qkv_p_kernel,
        grid=(B, nt),
        in_specs=[pl.BlockSpec((TM, W), lambda b, i: (b * nt + i, 3)),
                  pl.BlockSpec((TM, W), lambda b, i: (b * nt + i, 4)),
                  pl.BlockSpec((TM, W), lambda b, i: (b * nt + i, 5)),
                  pl.BlockSpec((1, LANE), lambda b, i: (0, 0)),
                  pl.BlockSpec((1, LANE), lambda b, i: (0, 0)),
                  pl.BlockSpec((TM, LANE), lambda b, i: (i, 0)),
                  pl.BlockSpec((TM, LANE), lambda b, i: (i, 0)),
                  pl.BlockSpec((TM, LANE), lambda b, i: (i, 0))],
        out_specs=[pl.BlockSpec((TM, W), row), pl.BlockSpec((TM, W), row), pl.BlockSpec((TM, W), row),
                   pl.BlockSpec((TM, W), row), pl.BlockSpec((TM, W), row),
                   pl.BlockSpec((nt, W), lambda b, i: (b, 0))],
        out_shape=[jax.ShapeDtypeStruct((B * T, W), F32), jax.ShapeDtypeStruct((B * T, W), F32),
                   jax.ShapeDtypeStruct((B * T, W), F32), jax.ShapeDtypeStruct((B * T, W), BF16),
                   jax.ShapeDtypeStruct((B * T, W), BF16), jax.ShapeDtypeStruct((B * nt, W), F32)],
        compiler_params=_cp(("arbitrary", "arbitrary"), 32),
        name="qkv_prompt",
    )(proj, proj, proj, qg, kg, cos, sa, sb)


def _topk_block_cols(bs, past):
    nb = bs.shape[1]
    jidx = lax.broadcasted_iota(jnp.int32, bs.shape, 1)
    cols = []
    for n in range(nb):
        bn = bs[:, n:n + 1]
        beats = ((bs > bn) | ((bs == bn) & (jidx < n))) & past
        rank = jnp.sum(beats.astype(F32), axis=-1, keepdims=True)
        cols.append(rank < (MOBA_TOPK - 0.5))
    return cols


def _attn_p_kernel(q_ref, k_ref, v_ref, km_ref, o_ref):
    i = pl.program_id(1)
    nb = km_ref.shape[0]
    scale = LANE ** -0.5
    row = lax.broadcasted_iota(jnp.int32, (TM, MOBA_BLOCK), 0)
    col = lax.broadcasted_iota(jnp.int32, (TM, MOBA_BLOCK), 1)
    tri = col <= row
    blk = lax.broadcasted_iota(jnp.int32, (TM, nb), 1)

    def attend(n_keys_blocks):
        nk = n_keys_blocks * MOBA_BLOCK
        for h in range(N_HEADS):
            hs = slice(h * LANE, (h + 1) * LANE)
            q = q_ref[:, hs]
            bs = _dot_nt(q, km_ref[:, hs], precision=lax.Precision.HIGHEST)
            sel = _topk_block_cols(bs, blk < i)
            s = _dot_nt(q.astype(BF16), k_ref[0:nk, hs]) * scale
            pieces = []
            for j in range(n_keys_blocks):
                jv = jnp.full((TM, 1), j, jnp.int32)
                mj = ((jv == i) & tri) | ((jv < i) & sel[j])
                pieces.append(jnp.where(mj, s[:, j * MOBA_BLOCK:(j + 1) * MOBA_BLOCK], -jnp.inf))
            s = jnp.concatenate(pieces, axis=-1)
            m = jnp.max(s, -1, keepdims=True)
            p = jnp.exp(s - m)
            den = jnp.sum(p, -1, keepdims=True)
            o_ref[:, hs] = (_dot(p.astype(BF16), v_ref[0:nk, hs]) / den).astype(o_ref.dtype)

    half = max(nb // 2, 1)
    if half < nb:
        pl.when(i < half)(lambda: attend(half))
        pl.when(i >= half)(lambda: attend(nb))
    else:
        attend(nb)


def _attn_prompt(q, kb, vb, kmean, B, T, W):
    nt = T // TM
    assert TM == MOBA_BLOCK
    return pl.pallas_call(
        _attn_p_kernel,
        grid=(B, nt),
        in_specs=[pl.BlockSpec((TM, W), lambda b, i: (b * nt + i, 0)),
                  pl.BlockSpec((T, W), lambda b, i: (b, 0)),
                  pl.BlockSpec((T, W), lambda b, i: (b, 0)),
                  pl.BlockSpec((nt, W), lambda b, i: (b, 0))],
        out_specs=pl.BlockSpec((TM, W), lambda b, i: (b * nt + i, 0)),
        out_shape=jax.ShapeDtypeStruct((B * T, W), BF16),
        compiler_params=_cp(("arbitrary", "arbitrary"), 40),
        name="attn_prompt",
    )(q, kb, vb, kmean)


def _gmlp_p_kernel(u_ref, v_ref, lg_ref, lb_ref, ws_ref, bt_ref, y_ref, gv_ref, vbuf):
    i = pl.program_id(1)
    vbuf[...] = _layer_norm(v_ref[...], lg_ref[...], lb_ref[...])
    row = lax.broadcasted_iota(jnp.int32, (GMLP_CHUNK, GMLP_CHUNK), 0)
    col = lax.broadcasted_iota(jnp.int32, (GMLP_CHUNK, GMLP_CHUNK), 1)
    tri = col <= row
    for g in range(GMLP_GROUPS):
        c0 = g * LANE
        w = jnp.where(tri, ws_ref[g], 0.0).astype(BF16)
        for r0 in range(0, TM, GMLP_CHUNK):
            mixed = _dot(w, vbuf[r0:r0 + GMLP_CHUNK, c0:c0 + LANE].astype(BF16)) + bt_ref[:, g:g + 1]
            y_ref[r0:r0 + GMLP_CHUNK, c0:c0 + LANE] = (
                u_ref[r0:r0 + GMLP_CHUNK, c0:c0 + LANE] * mixed).astype(y_ref.dtype)

    @pl.when(i == pl.num_programs(1) - 1)
    def _():
        gv_ref[...] = vbuf[TM - GMLP_CHUNK:TM, :]


def _gmlp_prompt(proj, B, T, W, lg, lb, ws, bt):
    nt = T // TM
    return pl.pallas_call(
        _gmlp_p_kernel,
        grid=(B, nt),
        in_specs=[pl.BlockSpec((TM, W), lambda b, i: (b * nt + i, 6)),
                  pl.BlockSpec((TM, W), lambda b, i: (b * nt + i, 7)),
                  pl.BlockSpec((1, W), lambda b, i: (0, 0)),
                  pl.BlockSpec((1, W), lambda b, i: (0, 0)),
                  pl.BlockSpec((GMLP_GROUPS, GMLP_CHUNK, GMLP_CHUNK), lambda b, i: (0, 0, 0)),
                  pl.BlockSpec((GMLP_CHUNK, GMLP_GROUPS), lambda b, i: (0, 0))],
        out_specs=[pl.BlockSpec((TM, W), lambda b, i: (b * nt + i, 0)),
                   pl.BlockSpec((None, GMLP_CHUNK, W), lambda b, i: (b, 0, 0))],
        out_shape=[jax.ShapeDtypeStruct((B * T, W), BF16),
                   jax.ShapeDtypeStruct((B, GMLP_CHUNK, W), F32)],
        scratch_shapes=[pltpu.VMEM((TM, W), F32)],
        compiler_params=_cp(("arbitrary", "arbitrary"), 32),
        name="gmlp_prompt",
    )(proj, proj, lg, lb, ws, bt)


def _sample_mix_kernel(p_ref, sc_ref, sp_ref, cw_ref, cb_ref, clg_ref, clb_ref, pw_ref, ps_ref,
                       qg_ref, kg_ref, cos_ref, sa_ref, sb_ref, glg_ref, glb_ref, ws_ref, gb_ref,
                       ya_ref, yb_ref, yd_ref, q_ref, k_ref, v_ref, cn_ref, pn_ref, gv_ref):
    ts, sb_rows, _ = p_ref.shape
    W = ya_ref.shape[2]
    n_conv = CONV_W - 1

    def col(c):
        return slice(c * W, (c + 1) * W)

    z = [p_ref[t, :, col(0)] * _sigmoid(p_ref[t, :, col(1)]) for t in range(ts)]

    def zext(r):
        return sc_ref[r] if r < n_conv else z[r - n_conv]

    for t in range(ts):
        acc = jnp.broadcast_to(cb_ref[...], (sb_rows, W))
        for j in range(CONV_W):
            acc = acc + cw_ref[j:j + 1, :] * zext(t + j)
        y = _layer_norm(acc, clg_ref[...], clb_ref[...])
        ya_ref[t] = (y * _sigmoid(y)).astype(ya_ref.dtype)
    for r in range(n_conv):
        cn_ref[r] = zext(r + ts)

    def pext(r, c0):
        if r < POOL_KEEP:
            return sp_ref[r, :, c0:c0 + LANE]
        return p_ref[r - POOL_KEEP, :, 2 * W + c0:2 * W + c0 + LANE]

    for t in range(ts):
        for gi, win in enumerate(POOL_WINDOWS):
            c0 = gi * LANE
            s = pext(POOL_KEEP + t, c0)
            for k in range(1, win):
                s = s + pext(POOL_KEEP + t - k, c0)
            d = s / float(win) - pext(POOL_KEEP + t, c0)
            yg = _dot(d.astype(BF16), pw_ref[gi].astype(BF16)) * ps_ref[:, c0:c0 + LANE]
            yb_ref[t, :, c0:c0 + LANE] = yg.astype(yb_ref.dtype)
    for r in range(POOL_KEEP):
        pn_ref[r] = sp_ref[r + ts] if r + ts < POOL_KEEP else p_ref[r + ts - POOL_KEEP, :, col(2)]

    for t in range(ts):
        cos = cos_ref[t:t + 1, :]
        sa = sa_ref[t:t + 1, :]
        sb = sb_ref[t:t + 1, :]
        for h in range(N_HEADS):
            c0 = h * LANE
            q_ref[t, :, c0:c0 + LANE] = _norm_rope_head(
                p_ref[t, :, 3 * W + c0:3 * W + c0 + LANE], qg_ref[...], cos, sa, sb)
            k_ref[t, :, c0:c0 + LANE] = _norm_rope_head(
                p_ref[t, :, 4 * W + c0:4 * W + c0 + LANE], kg_ref[...], cos, sa, sb)
        v_ref[t] = p_ref[t, :, col(5)]

    vn = [_layer_norm(p_ref[t, :, col(7)], glg_ref[...], glb_ref[...]) for t in range(ts)]
    for t in range(ts):
        gv_ref[t] = vn[t]
        for g in range(GMLP_GROUPS):
            c0 = g * LANE
            mixed = jnp.broadcast_to(gb_ref[g:g + 1, t:t + 1], (sb_rows, LANE))
            for s_ in range(t + 1):
                mixed = mixed + ws_ref[g, t:t + 1, s_:s_ + 1] * vn[s_][:, c0:c0 + LANE]
            yd_ref[t, :, c0:c0 + LANE] = (
                p_ref[t, :, 6 * W + c0:6 * W + c0 + LANE] * mixed).astype(yd_ref.dtype)


SAMPLE_SEQ_BLOCK = 32


def _sample_mix(proj_s, nb, ts, W, sc_t, sp_t, cw, cb, clg, clb, pw, ps, qg, kg, cos, sa, sb,
                glg, glb, ws, gb):
    G = len(POOL_WINDOWS)
    sblk = SAMPLE_SEQ_BLOCK
    full2 = lambda shape: pl.BlockSpec(shape, lambda i: (0, 0))
    full3 = lambda shape: pl.BlockSpec(shape, lambda i: (0, 0, 0))
    seq3 = lambda rows, c: pl.BlockSpec((rows, sblk, c), lambda i: (0, i, 0))
    tok = lambda dt: jax.ShapeDtypeStruct((ts, nb, W), dt)
    return pl.pallas_call(
        _sample_mix_kernel,
        grid=(nb // sblk,),
        in_specs=[seq3(ts, 8 * W), seq3(CONV_W - 1, W), seq3(POOL_KEEP, W),
                  full2((CONV_W, W)), full2((1, W)), full2((1, W)), full2((1, W)),
                  full3((G, LANE, LANE)), full2((1, W)),
                  full2((1, LANE)), full2((1, LANE)),
                  full2((8, LANE)), full2((8, LANE)), full2((8, LANE)),
                  full2((1, W)), full2((1, W)),
                  full3((GMLP_GROUPS, GMLP_CHUNK, GMLP_CHUNK)), full2((GMLP_GROUPS, GMLP_CHUNK))],
        out_specs=[seq3(ts, W)] * 6 + [seq3(CONV_W - 1, W), seq3(POOL_KEEP, W), seq3(ts, W)],
        out_shape=[tok(BF16), tok(BF16), tok(BF16), tok(F32), tok(F32), tok(F32),
                   jax.ShapeDtypeStruct((CONV_W - 1, nb, W), F32),
                   jax.ShapeDtypeStruct((POOL_KEEP, nb, W), F32), tok(F32)],
        compiler_params=_cp(("arbitrary",), 40),
        name="sample_mix",
    )(proj_s, sc_t, sp_t, cw, cb, clg, clb, pw, ps, qg, kg, cos, sa, sb, glg, glb, ws, gb)


def _attn_s_kernel(pt_ref, q_ref, kn_ref, vn_ref, *refs, n_pages, ts):
    kp = refs[:n_pages]
    vp = refs[n_pages:2 * n_pages]
    o_ref = refs[2 * n_pages]
    kbuf, vbuf = refs[2 * n_pages + 1:]
    nh = N_HEADS
    prow = kp[0].shape[0]
    R, hd = q_ref.shape
    blk_rows = MOBA_BLOCK * nh
    ppb = blk_rows // prow
    nblk = n_pages // ppb
    scale = hd ** -0.5
    q = q_ref[...]
    rhead = lax.broadcasted_iota(jnp.int32, (R, 1), 0) >> 3
    t_row = lax.broadcasted_iota(jnp.int32, (R, 1), 0) & 7
    blkid = lax.broadcasted_iota(jnp.int32, (R, nblk), 1)
    bs = jnp.zeros((R, nblk), F32)
    for j in range(nblk):
        acc8 = None
        for u in range(ppb):
            pg = j * ppb + u
            kpg = kp[pg][...]
            kbuf[pg * prow:(pg + 1) * prow, :] = kpg.astype(BF16)
            vbuf[pg * prow:(pg + 1) * prow, :] = vp[pg][...].astype(BF16)
            part = jnp.sum(kpg.reshape(prow // 8, 8, hd), axis=0)
            acc8 = part if acc8 is None else acc8 + part
        kmean = (acc8[0:nh] + acc8[nh:2 * nh]) * (1.0 / MOBA_BLOCK)
        km_rows = jnp.concatenate([jnp.broadcast_to(kmean[h:h + 1], (8, hd)) for h in range(nh)], axis=0)
        bs = jnp.where(blkid == j, jnp.sum(q * km_rows, -1, keepdims=True), bs)
    sel = _topk_block_cols(bs, blkid >= 0)
    qb = q.astype(BF16)
    s_all = _dot_nt(qb, kbuf[...]) * scale
    own_head = (lax.broadcasted_iota(jnp.int32, (1, blk_rows), 1) & (nh - 1)) == rhead
    s_past = jnp.concatenate(
        [jnp.where(own_head & sel[j], s_all[:, j * blk_rows:(j + 1) * blk_rows], -jnp.inf)
         for j in range(nblk)], axis=-1)
    nc = kn_ref.shape[0]
    ocol = lax.broadcasted_iota(jnp.int32, (1, nc), 1)
    s_own = _dot_nt(qb, kn_ref[...].astype(BF16)) * scale
    s_own = jnp.where(((ocol & (nh - 1)) == rhead) & ((ocol >> 2) <= t_row), s_own, -jnp.inf)
    m = jnp.maximum(jnp.max(s_past, -1, keepdims=True), jnp.max(s_own, -1, keepdims=True))
    p_past = jnp.exp(s_past - m)
    p_own = jnp.exp(s_own - m)
    den = jnp.sum(p_past, -1, keepdims=True) + jnp.sum(p_own, -1, keepdims=True)
    acc = _dot(p_past.astype(BF16), vbuf[...]) + _dot(p_own.astype(BF16), vn_ref[...].astype(BF16))
    o_ref[...] = acc / den


def _attn_sample(pt_flat, q_hr, k_new, v_new, ck, cv, l, nb, ts, n_pages):
    prow, hd = ck.shape[2:]
    R = q_hr.shape[1]
    nc = k_new.shape[1]
    assert N_HEADS == 4 and nc % 8 == 0

    def page_spec(j):
        return pl.BlockSpec((None, None, prow, hd), lambda b, pt: (l, pt[b * n_pages + j], 0, 0))

    in_specs = ([pl.BlockSpec((None, R, hd), lambda b, pt: (b, 0, 0)),
                 pl.BlockSpec((None, nc, hd), lambda b, pt: (b, 0, 0)),
                 pl.BlockSpec((None, nc, hd), lambda b, pt: (b, 0, 0))]
                + [page_spec(j) for j in range(n_pages)] + [page_spec(j) for j in range(n_pages)])
    return pl.pallas_call(
        functools.partial(_attn_s_kernel, n_pages=n_pages, ts=ts),
        grid_spec=pltpu.PrefetchScalarGridSpec(
            num_scalar_prefetch=1, grid=(nb,), in_specs=in_specs,
            out_specs=pl.BlockSpec((None, R, hd), lambda b, pt: (b, 0, 0)),
            scratch_shapes=[pltpu.VMEM((n_pages * prow, hd), BF16), pltpu.VMEM((n_pages * prow, hd), BF16)]),
        out_shape=jax.ShapeDtypeStruct((nb, R, hd), F32),
        compiler_params=_cp(("arbitrary",), 48),
        name="attn_sample",
    )(pt_flat, q_hr, k_new, v_new, *([ck] * n_pages), *([cv] * n_pages))


def _merge_kernel(ya_ref, yb_ref, yc_ref, yd_ref, wa_ref, wb_ref, wc_ref, wd_ref,
                  ga_ref, gb_ref, gc_ref, gd_ref, o_ref, wbf_ref):
    w_refs = (wa_ref, wb_ref, wc_ref, wd_ref)

    @pl.when(pl.program_id(1) == 0)
    def _():
        for b in range(N_BRANCH):
            wbf_ref[b] = w_refs[b][...].astype(BF16)

    acc = None
    for b, (y_ref, g_ref) in enumerate(zip((ya_ref, yb_ref, yc_ref, yd_ref),
                                           (ga_ref, gb_ref, gc_ref, gd_ref))):
        term = _sigmoid(g_ref[...]) * _dot(y_ref[...], wbf_ref[b])
        acc = term if acc is None else acc + term
    o_ref[...] = acc.astype(o_ref.dtype)


def _merge(ys, ws, proj, l, W, D):
    M = proj.shape[0]
    tm, tn = 512, 1024
    npb = D // tn
    gate0 = 8 * W // tn
    y_spec = pl.BlockSpec((tm, W), lambda n, m: (m, 0))
    w_spec = pl.BlockSpec((None, W, tn), lambda n, m: (l, 0, n))
    g_specs = [pl.BlockSpec((tm, tn), functools.partial(lambda n, m, b: (m, gate0 + b * npb + n), b=b))
               for b in range(N_BRANCH)]
    return pl.pallas_call(
        _merge_kernel,
        grid=(npb, M // tm),
        in_specs=[y_spec] * 4 + [w_spec] * 4 + g_specs,
        out_specs=pl.BlockSpec((tm, tn), lambda n, m: (m, n)),
        out_shape=jax.ShapeDtypeStruct((M, D), BF16),
        scratch_shapes=[pltpu.VMEM((N_BRANCH, W, tn), BF16)],
        compiler_params=_cp(("arbitrary", "arbitrary"), 48),
        name="merge",
    )(*ys, *ws, proj, proj, proj, proj)


def _wo_kernel(a_ref, w_ref, x_ref, g_ref, o_ref, wbf_ref):
    @pl.when(pl.program_id(1) == 0)
    def _():
        wbf_ref[...] = w_ref[...].astype(BF16)

    o_ref[...] = x_ref[...] + g_ref[...] * _dot(a_ref[...], wbf_ref[...])


def _wo_residual(merged, w_o, l, x, mr, gate_chunk, seq_of_tile):
    M, D = x.shape
    tn = 1024
    ncb = D // tn
    return pl.pallas_call(
        _wo_kernel,
        grid=(ncb, M // TM),
        in_specs=[pl.BlockSpec((TM, D), lambda n, m: (m, 0)),
                  pl.BlockSpec((None, D, tn), lambda n, m: (l, 0, n)),
                  pl.BlockSpec((TM, tn), lambda n, m: (m, n)),
                  pl.BlockSpec((None, TM, tn), lambda n, m: (seq_of_tile(m), 0, gate_chunk * ncb + n))],
        out_specs=pl.BlockSpec((TM, tn), lambda n, m: (m, n)),
        out_shape=jax.ShapeDtypeStruct((M, D), F32),
        scratch_shapes=[pltpu.VMEM((D, tn), BF16)],
        compiler_params=_cp(("arbitrary", "arbitrary"), 40),
        name="wo_residual",
    )(merged, w_o, x, mr)


def _pack_bf16_pairs(x):
    n = x.shape[1] // 2
    lo = lax.bitcast_convert_type(x[:, :n].astype(BF16).astype(F32), jnp.uint32)
    hi = lax.bitcast_convert_type(x[:, n:].astype(BF16).astype(F32), jnp.uint32)
    return hi | (lo >> 16)


def _unpack_bf16_pairs(w):
    lo = lax.bitcast_convert_type(w << 16, F32).astype(BF16)
    hi = lax.bitcast_convert_type(w & jnp.uint32(0xFFFF0000), F32).astype(BF16)
    return lo, hi


def _norm_router_kernel(x_ref, g_ref, sc_ref, sh_ref, rw_ref, rb_ref,
                        h_ref, idx_ref, gate_ref, rank_ref, cnt_ref, run_ref):
    i = pl.program_id(0)
    E = rw_ref.shape[1]

    @pl.when(i == 0)
    def _():
        run_ref[...] = jnp.zeros(run_ref.shape, F32)

    x = x_ref[...]
    ms = jnp.mean(x * x, -1, keepdims=True)
    y = x * lax.rsqrt(ms + EPS) * g_ref[...]
    h = y * (1.0 + sc_ref[...]) + sh_ref[...]
    h_ref[...] = _pack_bf16_pairs(h)
    logits = jnp.dot(h, rw_ref[...], precision=lax.Precision.HIGHEST,
                     preferred_element_type=F32) + rb_ref[...]

    eidx = lax.broadcasted_iota(jnp.int32, (TM, E), 1).astype(F32)
    kcol = lax.broadcasted_iota(jnp.int32, (TM, TOP_K), 1)
    r_i = lax.broadcasted_iota(jnp.int32, (TM, TM), 0)
    c_i = lax.broadcasted_iota(jnp.int32, (TM, TM), 1)
    before = jnp.where(c_i < r_i, 1.0, 0.0).astype(BF16)
    work = logits
    vals, hots = [], []
    idx_out = jnp.zeros((TM, TOP_K), F32)
    rank_out = jnp.zeros((TM, TOP_K), F32)
    run = run_ref[...]
    for k in range(TOP_K):
        mx = jnp.max(work, -1, keepdims=True)
        am = jnp.min(jnp.where(work == mx, eidx, float(E)), -1, keepdims=True)
        hot = eidx == am
        work = jnp.where(hot, -jnp.inf, work)
        hot_f = jnp.where(hot, 1.0, 0.0)
        earlier = _dot(before, hot_f.astype(BF16))
        rank = jnp.sum(hot_f * (earlier + run), -1, keepdims=True)
        run = run + jnp.sum(hot_f, axis=0, keepdims=True)
        vals.append(mx)
        idx_out = jnp.where(kcol == k, am, idx_out)
        rank_out = jnp.where(kcol == k, rank, rank_out)
    run_ref[...] = run
    cnt_ref[...] = run
    den = jnp.zeros((TM, 1), F32)
    gate_out = jnp.zeros((TM, TOP_K), F32)
    ex = [jnp.exp(v - vals[0]) for v in vals]
    for e_ in ex:
        den = den + e_
    for k in range(TOP_K):
        gate_out = jnp.where(kcol == k, ex[k] / den, gate_out)
    idx_ref[...] = idx_out.astype(jnp.int32)
    rank_ref[...] = rank_out.astype(jnp.int32)
    gate_ref[...] = gate_out


def _norm_router(x, g, mr, sc_chunk, sh_chunk, seq_of_tile, rw, rb):
    M, D = x.shape
    E = rw.shape[1]
    tok = lambda c: pl.BlockSpec((TM, c), lambda i: (i, 0))
    return pl.pallas_call(
        _norm_router_kernel,
        grid=(M // TM,),
        in_specs=[tok(D),
                  pl.BlockSpec((1, D), lambda i: (0, 0)),
                  pl.BlockSpec((None, TM, D), lambda i: (seq_of_tile(i), 0, sc_chunk)),
                  pl.BlockSpec((None, TM, D), lambda i: (seq_of_tile(i), 0, sh_chunk)),
                  pl.BlockSpec((D, E), lambda i: (0, 0)),
                  pl.BlockSpec((1, E), lambda i: (0, 0))],
        out_specs=[tok(D // 2), tok(TOP_K), tok(TOP_K), tok(TOP_K), pl.BlockSpec((1, E), lambda i: (0, 0))],
        out_shape=[jax.ShapeDtypeStruct((M, D // 2), jnp.uint32), jax.ShapeDtypeStruct((M, TOP_K), jnp.int32),
                   jax.ShapeDtypeStruct((M, TOP_K), F32), jax.ShapeDtypeStruct((M, TOP_K), jnp.int32),
                   jax.ShapeDtypeStruct((1, E), F32)],
        scratch_shapes=[pltpu.VMEM((1, E), F32)],
        compiler_params=_cp(("arbitrary",), 40),
        name="norm_router",
    )(x, g, mr, mr, rw, rb)


def _row_copy(src_hbm, dst, src_row, dst_row, sem):
    return pltpu.make_async_copy(src_hbm.at[pl.ds(src_row, 1)], dst.at[pl.ds(dst_row, 1)], sem)


DISPATCH_TM = 512


def _dispatch_kernel(dest_ref, h_hbm, xb_init_hbm, xb_hbm, sem):
    del xb_init_hbm
    i = pl.program_id(0)

    def issue(r, c):
        tok = i * DISPATCH_TM + r
        for k in range(TOP_K):
            _row_copy(h_hbm, xb_hbm, tok, dest_ref[tok * TOP_K + k], sem).start()
        return c

    lax.fori_loop(0, DISPATCH_TM, issue, 0)

    def drain(r, c):
        for k in range(TOP_K):
            _row_copy(h_hbm, xb_hbm, 0, 0, sem).wait()
        return c

    lax.fori_loop(0, DISPATCH_TM, drain, 0)


def _dispatch(dest_flat, h2p, cap):
    M, Dp = h2p.shape
    any_spec = pl.BlockSpec(memory_space=pl.ANY)
    return pl.pallas_call(
        _dispatch_kernel,
        grid_spec=pltpu.PrefetchScalarGridSpec(
            num_scalar_prefetch=1, grid=(M // DISPATCH_TM,),
            in_specs=[any_spec, any_spec], out_specs=any_spec,
            scratch_shapes=[pltpu.SemaphoreType.DMA(())]),
        out_shape=jax.ShapeDtypeStruct((cap, Dp), jnp.uint32),
        input_output_aliases={2: 0},
        compiler_params=_cp(("arbitrary",), 16),
        name="moe_dispatch",
    )(dest_flat, h2p, jnp.zeros((cap, Dp), jnp.uint32))


def _moe_up_kernel(be_ref, nu_ref, x_ref, wg_ref, wl_ref, bg_ref, bl_ref, o_ref, wgb, wlb):
    m = pl.program_id(1)
    e = be_ref[m]
    prev = be_ref[jnp.maximum(m - 1, 0)]

    @pl.when((m == 0) | (e != prev))
    def _():
        wgb[...] = wg_ref[...].astype(BF16)
        wlb[...] = wl_ref[...].astype(BF16)

    @pl.when(m < nu_ref[0])
    def _():
        x_lo, x_hi = _unpack_bf16_pairs(x_ref[...])
        n = x_lo.shape[1]
        hg = _dot(x_lo, wgb[0:n, :]) + _dot(x_hi, wgb[n:2 * n, :]) + bg_ref[...]
        hl = _dot(x_lo, wlb[0:n, :]) + _dot(x_hi, wlb[n:2 * n, :]) + bl_ref[...]
        hg = jnp.minimum(hg, SWIGLU_LIMIT)
        hl = jnp.clip(hl, -SWIGLU_LIMIT, SWIGLU_LIMIT)
        o_ref[...] = (hg * _sigmoid(SWIGLU_ALPHA * hg) * (hl + 1.0)).astype(o_ref.dtype)

    @pl.when(m >= nu_ref[0])
    def _():
        o_ref[...] = jnp.zeros(o_ref.shape, o_ref.dtype)


def _moe_up(blk_expert, n_used, xb, w1, b1, l, E):
    cap, Dp = xb.shape
    D = w1.shape[1]
    F = w1.shape[2] // 2
    tn = 1024
    nf = F // tn
    n_blocks = cap // MOE_BM
    return pl.pallas_call(
        _moe_up_kernel,
        grid_spec=pltpu.PrefetchScalarGridSpec(
            num_scalar_prefetch=2, grid=(nf, n_blocks),
            in_specs=[pl.BlockSpec((MOE_BM, Dp), lambda n, m, be, nu: (m, 0)),
                      pl.BlockSpec((None, D, tn), lambda n, m, be, nu: (l * E + be[m], 0, n)),
                      pl.BlockSpec((None, D, tn), lambda n, m, be, nu: (l * E + be[m], 0, nf + n)),
                      pl.BlockSpec((None, 1, tn), lambda n, m, be, nu: (l * E + be[m], 0, n)),
                      pl.BlockSpec((None, 1, tn), lambda n, m, be, nu: (l * E + be[m], 0, nf + n))],
            out_specs=pl.BlockSpec((MOE_BM, tn), lambda n, m, be, nu: (m, n)),
            scratch_shapes=[pltpu.VMEM((D, tn), BF16), pltpu.VMEM((D, tn), BF16)]),
        out_shape=jax.ShapeDtypeStruct((cap, F), BF16),
        compiler_params=_cp(("arbitrary", "arbitrary"), 56),
        name="moe_up",
    )(blk_expert, n_used, xb, w1, w1, b1, b1)


def _moe_down_kernel(be_ref, nu_ref, a_ref, w_ref, b_ref, o_ref, wb):
    m = pl.program_id(1)
    e = be_ref[m]
    prev = be_ref[jnp.maximum(m - 1, 0)]

    @pl.when((m == 0) | (e != prev))
    def _():
        wb[...] = w_ref[...].astype(BF16)

    @pl.when(m < nu_ref[0])
    def _():
        o_ref[...] = _dot(a_ref[...], wb[...]) + b_ref[...]

    @pl.when(m >= nu_ref[0])
    def _():
        o_ref[...] = jnp.zeros(o_ref.shape, o_ref.dtype)


def _moe_down(blk_expert, n_used, act, w2, b2, l, E):
    cap, F = act.shape
    D = w2.shape[2]
    tn = D
    n_blocks = cap // MOE_BM
    return pl.pallas_call(
        _moe_down_kernel,
        grid_spec=pltpu.PrefetchScalarGridSpec(
            num_scalar_prefetch=2, grid=(D // tn, n_blocks),
            in_specs=[pl.BlockSpec((MOE_BM, F), lambda n, m, be, nu: (m, 0)),
                      pl.BlockSpec((None, F, tn), lambda n, m, be, nu: (l * E + be[m], 0, n)),
                      pl.BlockSpec((None, 1, tn), lambda n, m, be, nu: (l * E + be[m], 0, n))],
            out_specs=pl.BlockSpec((MOE_BM, tn), lambda n, m, be, nu: (m, n)),
            scratch_shapes=[pltpu.VMEM((F, tn), BF16)]),
        out_shape=jax.ShapeDtypeStruct((cap, D), F32),
        compiler_params=_cp(("arbitrary", "arbitrary"), 56),
        name="moe_down",
    )(blk_expert, n_used, act, w2, b2)


COMBINE_TM = 128


def _combine_kernel(slot_ref, y_hbm, x_ref, g_ref, gate_ref, o_ref, buf, sem):
    i = pl.program_id(0)

    def issue(r, c):
        for k in range(TOP_K):
            _row_copy(y_hbm, buf.at[k], slot_ref[(i * COMBINE_TM + r) * TOP_K + k], r, sem).start()
        return c

    lax.fori_loop(0, COMBINE_TM, issue, 0)

    def drain(r, c):
        for k in range(TOP_K):
            _row_copy(y_hbm, buf.at[k], 0, r, sem).wait()
        return c

    lax.fori_loop(0, COMBINE_TM, drain, 0)
    y = gate_ref[:, 0:1] * buf[0]
    for k in range(1, TOP_K):
        y = y + gate_ref[:, k:k + 1] * buf[k]
    o_ref[...] = x_ref[...] + g_ref[...] * y


def _combine(slot_of, yb, x, mr, gate_chunk, seq_of_tile, gate):
    M, D = x.shape
    per = TM // COMBINE_TM
    return pl.pallas_call(
        _combine_kernel,
        grid_spec=pltpu.PrefetchScalarGridSpec(
            num_scalar_prefetch=1, grid=(M // COMBINE_TM,),
            in_specs=[pl.BlockSpec(memory_space=pl.ANY),
                      pl.BlockSpec((COMBINE_TM, D), lambda i, s: (i, 0)),
                      pl.BlockSpec((None, COMBINE_TM, D), lambda i, s: (seq_of_tile(i // per), 0, gate_chunk)),
                      pl.BlockSpec((COMBINE_TM, TOP_K), lambda i, s: (i, 0))],
            out_specs=pl.BlockSpec((COMBINE_TM, D), lambda i, s: (i, 0)),
            scratch_shapes=[pltpu.VMEM((TOP_K, COMBINE_TM, D), F32), pltpu.SemaphoreType.DMA(())]),
        out_shape=jax.ShapeDtypeStruct((M, D), F32),
        compiler_params=_cp(("arbitrary",), 32),
        name="moe_combine",
    )(slot_of, yb, x, mr, gate)


def _slot_tables(top_idx, rank, counts, n_blocks):
    E = counts.shape[0]
    padded = ((counts + MOE_BM - 1) // MOE_BM) * MOE_BM
    pad_end = jnp.cumsum(padded)
    pad_start = pad_end - padded
    onehot = top_idx[..., None] == jnp.arange(E, dtype=jnp.int32)
    dest = jnp.sum(jnp.where(onehot, pad_start, 0), -1) + rank
    blk_start = jnp.arange(n_blocks, dtype=jnp.int32) * MOE_BM
    blk_expert = jnp.minimum(jnp.sum(blk_start[:, None] >= pad_end[None, :], -1), E - 1).astype(jnp.int32)
    n_used = (pad_end[-1] // MOE_BM).astype(jnp.int32).reshape(1)
    return dest.reshape(-1).astype(jnp.int32), blk_expert, n_used


def _rope_tables(pos, hd):
    rot = hd // 4
    half = rot // 2
    inv = jnp.power(jnp.float32(ROPE_THETA), -jnp.arange(half, dtype=F32) * 2.0 / rot)
    ang = pos.astype(F32)[:, None] * inv[None, :]
    cos, sin = jnp.cos(ang), jnp.sin(ang)
    n = pos.shape[0]
    c = jnp.concatenate([cos, cos, jnp.ones((n, hd - rot), F32)], 1)
    sa = jnp.concatenate([-sin, jnp.zeros((n, hd - half), F32)], 1)
    sb = jnp.concatenate([jnp.zeros((n, half), F32), sin, jnp.zeros((n, hd - rot), F32)], 1)
    return c, sa, sb


def kernel(x_prompt, x_sample, cache_k, cache_v, state_conv, state_pool, page_table, c_prompt, c_sample,
           w_ada, b_ada, norm1_g, norm2_g, w_in, conv_w, conv_b, conv_ln_g, conv_ln_b, conv_out,
           pool_w, pool_scale, pool_out, q_norm_g, k_norm_g, attn_out, gmlp_ln_g, gmlp_ln_b,
           gmlp_ws, gmlp_b, gmlp_out, w_o, router_w, router_b, exp_w1, exp_b1, exp_w2, exp_b2):
    B, T, D = x_prompt.shape
    NB, TS, _ = x_sample.shape
    L = w_ada.shape[0]
    W = D // N_BRANCH
    HD = W // N_HEADS
    E = router_w.shape[2]
    n_pool, PAGE = cache_k.shape[1], cache_k.shape[2]
    n_pages = page_table.shape[1]
    past = n_pages * PAGE
    M_p, M_s = B * T, NB * TS
    M = M_p + M_s
    assert HD == LANE and W == N_HEADS * LANE and T % TM == 0 and TM % NB == 0 and NB % 8 == 0
    assert M % DISPATCH_TM == 0 and M_s % TM == 0 and TS <= 8 and past % MOBA_BLOCK == 0
    tiles_per_seq = T // TM

    def seq_of_tile(i):
        return jnp.minimum(i // tiles_per_seq, B)

    x = jnp.concatenate([x_prompt.reshape(M_p, D), x_sample.transpose(1, 0, 2).reshape(M_s, D)], 0)
    n_c = B + NB
    c_all = jnp.concatenate([c_prompt, c_sample, jnp.zeros((-n_c % 8, D), F32)], 0)
    mod_all = _ada(c_all, w_ada, b_ada)

    cos_p, sa_p, sb_p = _rope_tables(jnp.arange(T), HD)
    cos_s, sa_s, sb_s = [jnp.pad(t, ((0, 8 - TS), (0, 0))) for t in _rope_tables(past + jnp.arange(TS), HD)]
    pt_flat = page_table.reshape(-1).astype(jnp.int32)
    ck = cache_k.reshape(L, n_pool, PAGE * N_HEADS, HD)
    cv = cache_v.reshape(L, n_pool, PAGE * N_HEADS, HD)
    w1 = exp_w1.reshape(L * E, D, exp_w1.shape[3])
    b1 = exp_b1.reshape(L * E, 1, exp_b1.shape[2])
    w2 = exp_w2.reshape(L * E, exp_w2.shape[2], D)
    b2 = exp_b2.reshape(L * E, 1, D)

    def to_bt(a_t):
        return a_t.reshape(TS, NB, -1).transpose(1, 0, 2)

    outs = {k: [] for k in ('kp', 'vp', 'ks', 'vs', 'cp', 'cs', 'pp', 'ps', 'gp', 'gs')}
    for l in range(L):
        mod = mod_all[l]
        mr = jnp.concatenate([jnp.broadcast_to(mod[:B, None, :], (B, TM, 6 * D)),
                              jnp.tile(mod[B:B + NB], (TM // NB, 1))[None]], 0)
        h = _norm_mod(x, norm1_g[l][None], mr, 1, 0, seq_of_tile, BF16)
        tm_proj = next(t for t in (1088, 1024, 512, 256) if M % t == 0)
        proj = _panel_mm(h, w_in, l, tm_proj, 1024)

        row = lambda v: v[l][None]
        ya_p, conv_p = _conv_prompt(proj, B, T, W, conv_w[l], row(conv_b), row(conv_ln_g), row(conv_ln_b))
        yb_p, pool_p = _pool_prompt(proj, B, T, W, pool_w[l], row(pool_scale))
        q_p, k_p, v_p, kb_p, vb_p, kmean_p = _qkv_prompt(proj, B, T, W, row(q_norm_g), row(k_norm_g),
                                                        cos_p, sa_p, sb_p)
        yc_p = _attn_prompt(q_p, kb_p, vb_p, kmean_p, B, T, W)
        yd_p, gv_p = _gmlp_prompt(proj, B, T, W, row(gmlp_ln_g), row(gmlp_ln_b), gmlp_ws[l],
                                  jnp.transpose(gmlp_b[l]))

        proj_s = proj[M_p:, :8 * W].reshape(TS, NB, 8 * W)
        (ya_s, yb_s, yd_s, q_s, k_s, v_s, conv_s, pool_s, gv_s) = _sample_mix(
            proj_s, NB, TS, W, state_conv[l].transpose(1, 0, 2), state_pool[l].transpose(1, 0, 2),
            conv_w[l], row(conv_b), row(conv_ln_g), row(conv_ln_b), pool_w[l], row(pool_scale),
            row(q_norm_g), row(k_norm_g), cos_s, sa_s, sb_s, row(gmlp_ln_g), row(gmlp_ln_b),
            gmlp_ws[l], gmlp_b[l])
        q_hr = jnp.pad(to_bt(q_s).reshape(NB, TS, N_HEADS, HD).transpose(0, 2, 1, 3),
                       ((0, 0), (0, 0), (0, 8 - TS), (0, 0))).reshape(NB, N_HEADS * 8, HD)
        k_new = to_bt(k_s).reshape(NB, TS * N_HEADS, HD)
        v_new = to_bt(v_s).reshape(NB, TS * N_HEADS, HD)
        att_s = _attn_sample(pt_flat, q_hr, k_new, v_new, ck, cv, l, NB, TS, n_pages)
        yc_s = (att_s.reshape(NB, N_HEADS, 8, HD)[:, :, :TS].transpose(2, 0, 1, 3)
                .reshape(M_s, W).astype(BF16))

        ys = [jnp.concatenate([p_, s_.reshape(M_s, W)], 0) for p_, s_ in
              ((ya_p, ya_s), (yb_p, yb_s), (yc_p, yc_s), (yd_p, yd_s))]
        merged = _merge(ys, (conv_out, pool_out, attn_out, gmlp_out), proj, l, W, D)
        x = _wo_residual(merged, w_o, l, x, mr, 2, seq_of_tile)
        h2p, top_idx, gate, rank, counts = _norm_router(x, norm2_g[l][None], mr, 4, 3, seq_of_tile,
                                                         router_w[l], router_b[l][None])
        n_blocks = -(-M * TOP_K // MOE_BM) + E
        dest, blk_expert, n_used = _slot_tables(top_idx, rank, counts[0].astype(jnp.int32), n_blocks)
        xb = _dispatch(dest, h2p, n_blocks * MOE_BM)
        act = _moe_up(blk_expert, n_used, xb, w1, b1, l, E)
        yb = _moe_down(blk_expert, n_used, act, w2, b2, l, E)
        x = _combine(dest, yb, x, mr, 5, seq_of_tile, gate)

        outs['kp'].append(k_p.reshape(B, T, N_HEADS, HD))
        outs['vp'].append(v_p.reshape(B, T, N_HEADS, HD))
        outs['ks'].append(to_bt(k_s).reshape(NB, TS, N_HEADS, HD))
        outs['vs'].append(to_bt(v_s).reshape(NB, TS, N_HEADS, HD))
        outs['cp'].append(conv_p)
        outs['cs'].append(conv_s.transpose(1, 0, 2))
        outs['pp'].append(pool_p)
        outs['ps'].append(pool_s.transpose(1, 0, 2))
        outs['gp'].append(gv_p)
        outs['gs'].append(to_bt(gv_s))

    y_prompt = x[:M_p].reshape(B, T, D)
    y_sample = to_bt(x[M_p:])
    st = lambda k: jnp.stack(outs[k])
    return (y_prompt, y_sample, st('kp'), st('vp'), st('ks'), st('vs'), st('cp'), st('cs'),
            st('pp'), st('ps'), st('gp'), st('gs'))
```

```python
import functools

import jax
import jax.numpy as jnp
from jax import lax
from jax.experimental import pallas as pl
from jax.experimental.pallas import tpu as pltpu

F32 = jnp.float32
BF16 = jnp.bfloat16

N_BRANCH = 4
CONV_W = 31
POOL_WINDOWS = (2, 4, 8, 16)
POOL_KEEP = max(POOL_WINDOWS) - 1
N_HEADS = 4
ROPE_THETA = 500000.0
MOBA_BLOCK = 256
MOBA_TOPK = 3
GMLP_CHUNK = 128
GMLP_GROUPS = 4
TOP_K = 4
SWIGLU_LIMIT = 7.0
SWIGLU_ALPHA = 1.702
EPS = 1e-6

LANE = 128
TM = 256
CONV_HIST = 32
POOL_HIST = 16
MOE_BM = 256
MIB = 1024 * 1024


def _cp(sem, vmem_mib):
    return pltpu.CompilerParams(dimension_semantics=sem, vmem_limit_bytes=vmem_mib * MIB)


def _sigmoid(x):
    return 1.0 / (1.0 + jnp.exp(-x))


def _layer_norm(x, g, b):
    mu = jnp.mean(x, -1, keepdims=True)
    xc = x - mu
    var = jnp.mean(xc * xc, -1, keepdims=True)
    return xc * lax.rsqrt(var + EPS) * g + b


def _dot(a, b):
    return jnp.dot(a, b, preferred_element_type=F32)


def _dot_nt(a, b, precision=None):
    return lax.dot_general(a, b, (((1,), (1,)), ((), ())), precision=precision,
                           preferred_element_type=F32)


def _ada_kernel(c_ref, w_ref, b_ref, o_ref):
    c = c_ref[...]
    s = (c * _sigmoid(c)).astype(BF16)
    o_ref[...] = _dot(s, w_ref[...].astype(BF16)) + b_ref[...]


def _ada(c_all, w_ada, b_ada):
    L, D, N = w_ada.shape
    R = c_all.shape[0]
    tn = 1024
    return pl.pallas_call(
        _ada_kernel,
        grid=(L, N // tn),
        in_specs=[pl.BlockSpec((R, D), lambda l, n: (0, 0)),
                  pl.BlockSpec((None, D, tn), lambda l, n: (l, 0, n)),
                  pl.BlockSpec((None, 1, tn), lambda l, n: (l, 0, n))],
        out_specs=pl.BlockSpec((None, R, tn), lambda l, n: (l, 0, n)),
        out_shape=jax.ShapeDtypeStruct((L, R, N), F32),
        compiler_params=_cp(("arbitrary", "arbitrary"), 40),
        name="ada",
    )(c_all, w_ada, b_ada.reshape(L, 1, N))


def _norm_mod_kernel(x_ref, g_ref, sc_ref, sh_ref, o_ref):
    x = x_ref[...]
    ms = jnp.mean(x * x, -1, keepdims=True)
    y = x * lax.rsqrt(ms + EPS) * g_ref[...]
    o_ref[...] = (y * (1.0 + sc_ref[...]) + sh_ref[...]).astype(o_ref.dtype)


def _norm_mod(x, g, mr, sc_chunk, sh_chunk, seq_of_tile, out_dtype):
    M, D = x.shape
    return pl.pallas_call(
        _norm_mod_kernel,
        grid=(M // TM,),
        in_specs=[pl.BlockSpec((TM, D), lambda i: (i, 0)),
                  pl.BlockSpec((1, D), lambda i: (0, 0)),
                  pl.BlockSpec((None, TM, D), lambda i: (seq_of_tile(i), 0, sc_chunk)),
                  pl.BlockSpec((None, TM, D), lambda i: (seq_of_tile(i), 0, sh_chunk))],
        out_specs=pl.BlockSpec((TM, D), lambda i: (i, 0)),
        out_shape=jax.ShapeDtypeStruct((M, D), out_dtype),
        compiler_params=_cp(("arbitrary",), 40),
        name="norm_mod",
    )(x, g, mr, mr)


def _panel_mm_kernel(a_ref, w_ref, o_ref, wbf_ref):
    @pl.when(pl.program_id(1) == 0)
    def _():
        wbf_ref[...] = w_ref[...].astype(BF16)

    o_ref[...] = _dot(a_ref[...], wbf_ref[...])


def _panel_mm(a, w, l, tm, tn):
    M, K = a.shape
    N = w.shape[2]
    return pl.pallas_call(
        _panel_mm_kernel,
        grid=(N // tn, M // tm),
        in_specs=[pl.BlockSpec((tm, K), lambda n, m: (m, 0)),
                  pl.BlockSpec((None, K, tn), lambda n, m: (l, 0, n))],
        out_specs=pl.BlockSpec((tm, tn), lambda n, m: (m, n)),
        out_shape=jax.ShapeDtypeStruct((M, N), F32),
        scratch_shapes=[pltpu.VMEM((K, tn), BF16)],
        compiler_params=_cp(("arbitrary", "arbitrary"), 48),
        name="in_proj",
    )(a, w)


def _conv_p_kernel(a_ref, g_ref, w_ref, cb_ref, lg_ref, lb_ref, y_ref, cn_ref, zbuf, ybuf):
    i = pl.program_id(1)
    W = a_ref.shape[1]

    @pl.when(i == 0)
    def _():
        zbuf[0:CONV_HIST, :] = jnp.zeros((CONV_HIST, W), F32)

    @pl.when(i > 0)
    def _():
        zbuf[0:CONV_HIST, :] = zbuf[TM:TM + CONV_HIST, :]

    zbuf[CONV_HIST:CONV_HIST + TM, :] = a_ref[...] * _sigmoid(g_ref[...])
    base = CONV_HIST - (CONV_W - 1)
    rb = 128
    for c0 in range(0, W, LANE):
        for r0 in range(0, TM, rb):
            acc = jnp.broadcast_to(cb_ref[:, c0:c0 + LANE], (rb, LANE))
            for j in range(CONV_W):
                acc = acc + w_ref[j:j + 1, c0:c0 + LANE] * zbuf[pl.ds(base + r0 + j, rb), c0:c0 + LANE]
            ybuf[r0:r0 + rb, c0:c0 + LANE] = acc
    y = _layer_norm(ybuf[...], lg_ref[...], lb_ref[...])
    y_ref[...] = (y * _sigmoid(y)).astype(y_ref.dtype)

    @pl.when(i == pl.num_programs(1) - 1)
    def _():
        cn_ref[...] = zbuf[CONV_HIST + TM - (CONV_W - 1):CONV_HIST + TM, :]


def _conv_prompt(proj, B, T, W, cw, cb, lg, lb):
    nt = T // TM
    return pl.pallas_call(
        _conv_p_kernel,
        grid=(B, nt),
        in_specs=[pl.BlockSpec((TM, W), lambda b, i: (b * nt + i, 0)),
                  pl.BlockSpec((TM, W), lambda b, i: (b * nt + i, 1)),
                  pl.BlockSpec((CONV_W, W), lambda b, i: (0, 0)),
                  pl.BlockSpec((1, W), lambda b, i: (0, 0)),
                  pl.BlockSpec((1, W), lambda b, i: (0, 0)),
                  pl.BlockSpec((1, W), lambda b, i: (0, 0))],
        out_specs=[pl.BlockSpec((TM, W), lambda b, i: (b * nt + i, 0)),
                   pl.BlockSpec((None, CONV_W - 1, W), lambda b, i: (b, 0, 0))],
        out_shape=[jax.ShapeDtypeStruct((B * T, W), BF16),
                   jax.ShapeDtypeStruct((B, CONV_W - 1, W), F32)],
        scratch_shapes=[pltpu.VMEM((CONV_HIST + TM, W), F32), pltpu.VMEM((TM, W), F32)],
        compiler_params=_cp(("arbitrary", "arbitrary"), 32),
        name="conv_prompt",
    )(proj, proj, cw, cb, lg, lb)


def _pool_p_kernel(p_ref, pw_ref, ps_ref, y_ref, pn_ref, ebuf):
    i = pl.program_id(1)
    W = p_ref.shape[1]

    @pl.when(i == 0)
    def _():
        ebuf[0:POOL_HIST, :] = jnp.zeros((POOL_HIST, W), F32)

    @pl.when(i > 0)
    def _():
        ebuf[0:POOL_HIST, :] = ebuf[TM:TM + POOL_HIST, :]

    ebuf[POOL_HIST:POOL_HIST + TM, :] = p_ref[...]
    t_abs = i * TM + lax.broadcasted_iota(jnp.int32, (TM, 1), 0)
    for gi, win in enumerate(POOL_WINDOWS):
        c0 = gi * LANE
        s = ebuf[POOL_HIST:POOL_HIST + TM, c0:c0 + LANE]
        for k in range(1, win):
            s = s + ebuf[pl.ds(POOL_HIST - k, TM), c0:c0 + LANE]
        cnt = jnp.minimum(t_abs + 1, win).astype(F32)
        d = s / cnt - p_ref[:, c0:c0 + LANE]
        yg = _dot(d.astype(BF16), pw_ref[gi].astype(BF16)) * ps_ref[:, c0:c0 + LANE]
        y_ref[:, c0:c0 + LANE] = yg.astype(y_ref.dtype)

    @pl.when(i == pl.num_programs(1) - 1)
    def _():
        pn_ref[...] = ebuf[POOL_HIST + TM - POOL_KEEP:POOL_HIST + TM, :]


def _pool_prompt(proj, B, T, W, pw, ps):
    nt = T // TM
    G = len(POOL_WINDOWS)
    return pl.pallas_call(
        _pool_p_kernel,
        grid=(B, nt),
        in_specs=[pl.BlockSpec((TM, W), lambda b, i: (b * nt + i, 2)),
                  pl.BlockSpec((G, LANE, LANE), lambda b, i: (0, 0, 0)),
                  pl.BlockSpec((1, W), lambda b, i: (0, 0))],
        out_specs=[pl.BlockSpec((TM, W), lambda b, i: (b * nt + i, 0)),
                   pl.BlockSpec((None, POOL_KEEP, W), lambda b, i: (b, 0, 0))],
        out_shape=[jax.ShapeDtypeStruct((B * T, W), BF16),
                   jax.ShapeDtypeStruct((B, POOL_KEEP, W), F32)],
        scratch_shapes=[pltpu.VMEM((POOL_HIST + TM, W), F32)],
        compiler_params=_cp(("arbitrary", "arbitrary"), 32),
        name="pool_prompt",
    )(proj, pw, ps)


def _norm_rope_head(xh, g, cos, sa, sb):
    ms = jnp.mean(xh * xh, -1, keepdims=True)
    xn = xh * lax.rsqrt(ms + EPS) * g
    hd = xh.shape[1]
    half = hd // 8
    up = pltpu.roll(xn, hd - half, axis=1)
    dn = pltpu.roll(xn, half, axis=1)
    return xn * cos + up * sa + dn * sb


def _qkv_p_kernel(q_ref, k_ref, v_ref, qg_ref, kg_ref, cos_ref, sa_ref, sb_ref,
                  qo_ref, ko_ref, vo_ref, kb_ref, vb_ref, km_ref):
    i = pl.program_id(1)
    cos, sa, sb = cos_ref[...], sa_ref[...], sb_ref[...]

    @pl.when(i == 0)
    def _():
        km_ref[...] = jnp.zeros(km_ref.shape, F32)

    blk_row = lax.broadcasted_iota(jnp.int32, (km_ref.shape[0], LANE), 0)
    for h in range(N_HEADS):
        c0 = h * LANE
        qo_ref[:, c0:c0 + LANE] = _norm_rope_head(q_ref[:, c0:c0 + LANE], qg_ref[...], cos, sa, sb)
        kh = _norm_rope_head(k_ref[:, c0:c0 + LANE], kg_ref[...], cos, sa, sb)
        ko_ref[:, c0:c0 + LANE] = kh
        kb_ref[:, c0:c0 + LANE] = kh.astype(BF16)
        km_ref[:, c0:c0 + LANE] = jnp.where(blk_row == i, jnp.mean(kh, axis=0, keepdims=True),
                                            km_ref[:, c0:c0 + LANE])
    v = v_ref[...]
    vo_ref[...] = v
    vb_ref[...] = v.astype(BF16)


def _qkv_prompt(proj, B, T, W, qg, kg, cos, sa, sb):
    nt = T // TM
    row = lambda b, i: (b * nt + i, 0)
    return pl.pallas_call(
        _qkv_p_kernel,
        grid=(B, nt),
        in_specs=[pl.BlockSpec((TM, W), lambda b, i: (b * nt + i, 3)),
                  pl.BlockSpec((TM, W), lambda b, i: (b * nt + i, 4)),
                  pl.BlockSpec((TM, W), lambda b, i: (b * nt + i, 5)),
                  pl.BlockSpec((1, LANE), lambda b, i: (0, 0)),
                  pl.BlockSpec((1, LANE), lambda b, i: (0, 0)),
                  pl.BlockSpec((TM, LANE), lambda b, i: (i, 0)),
                  pl.BlockSpec((TM, LANE), lambda b, i: (i, 0)),
                  pl.BlockSpec((TM, LANE), lambda b, i: (i, 0))],
        out_specs=[pl.BlockSpec((TM, W), row), pl.BlockSpec((TM, W), row), pl.BlockSpec((TM, W), row),
                   pl.BlockSpec((TM, W), row), pl.BlockSpec((TM, W), row),
                   pl.BlockSpec((nt, W), lambda b, i: (b, 0))],
        out_shape=[jax.ShapeDtypeStruct((B * T, W), F32), jax.ShapeDtypeStruct((B * T, W), F32),
                   jax.ShapeDtypeStruct((B * T, W), F32), jax.ShapeDtypeStruct((B * T, W), BF16),
                   jax.ShapeDtypeStruct((B * T, W), BF16), jax.ShapeDtypeStruct((B * nt, W), F32)],
        compiler_params=_cp(("arbitrary", "arbitrary"), 32),
        name="qkv_prompt",
    )(proj, proj, proj, qg, kg, cos, sa, sb)


def _topk_block_cols(bs, past):
    nb = bs.shape[1]
    jidx = lax.broadcasted_iota(jnp.int32, bs.shape, 1)
    cols = []
    for n in range(nb):
        bn = bs[:, n:n + 1]
        beats = ((bs > bn) | ((bs == bn) & (jidx < n))) & past
        rank = jnp.sum(beats.astype(F32), axis=-1, keepdims=True)
        cols.append(rank < (MOBA_TOPK - 0.5))
    return cols


def _attn_p_kernel(q_ref, k_ref, v_ref, km_ref, o_ref):
    i = pl.program_id(1)
    nb = km_ref.shape[0]
    scale = LANE ** -0.5
    row = lax.broadcasted_iota(jnp.int32, (TM, MOBA_BLOCK), 0)
    col = lax.broadcasted_iota(jnp.int32, (TM, MOBA_BLOCK), 1)
    tri = col <= row
    blk = lax.broadcasted_iota(jnp.int32, (TM, nb), 1)

    def attend(n_keys_blocks):
        nk = n_keys_blocks * MOBA_BLOCK
        for h in range(N_HEADS):
            hs = slice(h * LANE, (h + 1) * LANE)
            q = q_ref[:, hs]
            bs = _dot_nt(q, km_ref[:, hs], precision=lax.Precision.HIGHEST)
            sel = _topk_block_cols(bs, blk < i)
            s = _dot_nt(q.astype(BF16), k_ref[0:nk, hs]) * scale
            pieces = []
            for j in range(n_keys_blocks):
                jv = jnp.full((TM, 1), j, jnp.int32)
                mj = ((jv == i) & tri) | ((jv < i) & sel[j])
                pieces.append(jnp.where(mj, s[:, j * MOBA_BLOCK:(j + 1) * MOBA_BLOCK], -jnp.inf))
            s = jnp.concatenate(pieces, axis=-1)
            m = jnp.max(s, -1, keepdims=True)
            p = jnp.exp(s - m)
            den = jnp.sum(p, -1, keepdims=True)
            o_ref[:, hs] = (_dot(p.astype(BF16), v_ref[0:nk, hs]) / den).astype(o_ref.dtype)

    half = max(nb // 2, 1)
    if half < nb:
        pl.when(i < half)(lambda: attend(half))
        pl.when(i >= half)(lambda: attend(nb))
    else:
        attend(nb)


def _attn_prompt(q, kb, vb, kmean, B, T, W):
    nt = T // TM
    assert TM == MOBA_BLOCK
    return pl.pallas_call(
        _attn_p_kernel,
        grid=(B, nt),
        in_specs=[pl.BlockSpec((TM, W), lambda b, i: (b * nt + i, 0)),
                  pl.BlockSpec((T, W), lambda b, i: (b, 0)),
                  pl.BlockSpec((T, W), lambda b, i: (b, 0)),
                  pl.BlockSpec((nt, W), lambda b, i: (b, 0))],
        out_specs=pl.BlockSpec((TM, W), lambda b, i: (b * nt + i, 0)),
        out_shape=jax.ShapeDtypeStruct((B * T, W), BF16),
        compiler_params=_cp(("arbitrary", "arbitrary"), 40),
        name="attn_prompt",
    )(q, kb, vb, kmean)


def _gmlp_p_kernel(u_ref, v_ref, lg_ref, lb_ref, ws_ref, bt_ref, y_ref, gv_ref, vbuf):
    i = pl.program_id(1)
    vbuf[...] = _layer_norm(v_ref[...], lg_ref[...], lb_ref[...])
    row = lax.broadcasted_iota(jnp.int32, (GMLP_CHUNK, GMLP_CHUNK), 0)
    col = lax.broadcasted_iota(jnp.int32, (GMLP_CHUNK, GMLP_CHUNK), 1)
    tri = col <= row
    for g in range(GMLP_GROUPS):
        c0 = g * LANE
        w = jnp.where(tri, ws_ref[g], 0.0).astype(BF16)
        for r0 in range(0, TM, GMLP_CHUNK):
            mixed = _dot(w, vbuf[r0:r0 + GMLP_CHUNK, c0:c0 + LANE].astype(BF16)) + bt_ref[:, g:g + 1]
            y_ref[r0:r0 + GMLP_CHUNK, c0:c0 + LANE] = (
                u_ref[r0:r0 + GMLP_CHUNK, c0:c0 + LANE] * mixed).astype(y_ref.dtype)

    @pl.when(i == pl.num_programs(1) - 1)
    def _():
        gv_ref[...] = vbuf[TM - GMLP_CHUNK:TM, :]


def _gmlp_prompt(proj, B, T, W, lg, lb, ws, bt):
    nt = T // TM
    return pl.pallas_call(
        _gmlp_p_kernel,
        grid=(B, nt),
        in_specs=[pl.BlockSpec((TM, W), lambda b, i: (b * nt + i, 6)),
                  pl.BlockSpec((TM, W), lambda b, i: (b * nt + i, 7)),
                  pl.BlockSpec((1, W), lambda b, i: (0, 0)),
                  pl.BlockSpec((1, W), lambda b, i: (0, 0)),
                  pl.BlockSpec((GMLP_GROUPS, GMLP_CHUNK, GMLP_CHUNK), lambda b, i: (0, 0, 0)),
                  pl.BlockSpec((GMLP_CHUNK, GMLP_GROUPS), lambda b, i: (0, 0))],
        out_specs=[pl.BlockSpec((TM, W), lambda b, i: (b * nt + i, 0)),
                   pl.BlockSpec((None, GMLP_CHUNK, W), lambda b, i: (b, 0, 0))],
        out_shape=[jax.ShapeDtypeStruct((B * T, W), BF16),
                   jax.ShapeDtypeStruct((B, GMLP_CHUNK, W), F32)],
        scratch_shapes=[pltpu.VMEM((TM, W), F32)],
        compiler_params=_cp(("arbitrary", "arbitrary"), 32),
        name="gmlp_prompt",
    )(proj, proj, lg, lb, ws, bt)


def _sample_mix_kernel(p_ref, sc_ref, sp_ref, cw_ref, cb_ref, clg_ref, clb_ref, pw_ref, ps_ref,
                       qg_ref, kg_ref, cos_ref, sa_ref, sb_ref, glg_ref, glb_ref, ws_ref, gb_ref,
                       ya_ref, yb_ref, yd_ref, q_ref, k_ref, v_ref, cn_ref, pn_ref, gv_ref):
    ts, sb_rows, _ = p_ref.shape
    W = ya_ref.shape[2]
    n_conv = CONV_W - 1

    def col(c):
        return slice(c * W, (c + 1) * W)

    z = [p_ref[t, :, col(0)] * _sigmoid(p_ref[t, :, col(1)]) for t in range(ts)]

    def zext(r):
        return sc_ref[r] if r < n_conv else z[r - n_conv]

    for t in range(ts):
        acc = jnp.broadcast_to(cb_ref[...], (sb_rows, W))
        for j in range(CONV_W):
            acc = acc + cw_ref[j:j + 1, :] * zext(t + j)
        y = _layer_norm(acc, clg_ref[...], clb_ref[...])
        ya_ref[t] = (y * _sigmoid(y)).astype(ya_ref.dtype)
    for r in range(n_conv):
        cn_ref[r] = zext(r + ts)

    def pext(r, c0):
        if r < POOL_KEEP:
            return sp_ref[r, :, c0:c0 + LANE]
        return p_ref[r - POOL_KEEP, :, 2 * W + c0:2 * W + c0 + LANE]

    for t in range(ts):
        for gi, win in enumerate(POOL_WINDOWS):
            c0 = gi * LANE
            s = pext(POOL_KEEP + t, c0)
            for k in range(1, win):
                s = s + pext(POOL_KEEP + t - k, c0)
            d = s / float(win) - pext(POOL_KEEP + t, c0)
            yg = _dot(d.astype(BF16), pw_ref[gi].astype(BF16)) * ps_ref[:, c0:c0 + LANE]
            yb_ref[t, :, c0:c0 + LANE] = yg.astype(yb_ref.dtype)
    for r in range(POOL_KEEP):
        pn_ref[r] = sp_ref[r + ts] if r + ts < POOL_KEEP else p_ref[r + ts - POOL_KEEP, :, col(2)]

    for t in range(ts):
        cos = cos_ref[t:t + 1, :]
        sa = sa_ref[t:t + 1, :]
        sb = sb_ref[t:t + 1, :]
        for h in range(N_HEADS):
            c0 = h * LANE
            q_ref[t, :, c0:c0 + LANE] = _norm_rope_head(
                p_ref[t, :, 3 * W + c0:3 * W + c0 + LANE], qg_ref[...], cos, sa, sb)
            k_ref[t, :, c0:c0 + LANE] = _norm_rope_head(
                p_ref[t, :, 4 * W + c0:4 * W + c0 + LANE], kg_ref[...], cos, sa, sb)
        v_ref[t] = p_ref[t, :, col(5)]

    vn = [_layer_norm(p_ref[t, :, col(7)], glg_ref[...], glb_ref[...]) for t in range(ts)]
    for t in range(ts):
        gv_ref[t] = vn[t]
        for g in range(GMLP_GROUPS):
            c0 = g * LANE
            mixed = jnp.broadcast_to(gb_ref[g:g + 1, t:t + 1], (sb_rows, LANE))
            for s_ in range(t + 1):
                mixed = mixed + ws_ref[g, t:t + 1, s_:s_ + 1] * vn[s_][:, c0:c0 + LANE]
            yd_ref[t, :, c0:c0 + LANE] = (
                p_ref[t, :, 6 * W + c0:6 * W + c0 + LANE] * mixed).astype(yd_ref.dtype)


SAMPLE_SEQ_BLOCK = 32


def _sample_mix(proj_s, nb, ts, W, sc_t, sp_t, cw, cb, clg, clb, pw, ps, qg, kg, cos, sa, sb,
                glg, glb, ws, gb):
    G = len(POOL_WINDOWS)
    sblk = SAMPLE_SEQ_BLOCK
    full2 = lambda shape: pl.BlockSpec(shape, lambda i: (0, 0))
    full3 = lambda shape: pl.BlockSpec(shape, lambda i: (0, 0, 0))
    seq3 = lambda rows, c: pl.BlockSpec((rows, sblk, c), lambda i: (0, i, 0))
    tok = lambda dt: jax.ShapeDtypeStruct((ts, nb, W), dt)
    return pl.pallas_call(
        _sample_mix_kernel,
        grid=(nb // sblk,),
        in_specs=[seq3(ts, 8 * W), seq3(CONV_W - 1, W), seq3(POOL_KEEP, W),
                  full2((CONV_W, W)), full2((1, W)), full2((1, W)), full2((1, W)),
                  full3((G, LANE, LANE)), full2((1, W)),
                  full2((1, LANE)), full2((1, LANE)),
                  full2((8, LANE)), full2((8, LANE)), full2((8, LANE)),
                  full2((1, W)), full2((1, W)),
                  full3((GMLP_GROUPS, GMLP_CHUNK, GMLP_CHUNK)), full2((GMLP_GROUPS, GMLP_CHUNK))],
        out_specs=[seq3(ts, W)] * 6 + [seq3(CONV_W - 1, W), seq3(POOL_KEEP, W), seq3(ts, W)],
        out_shape=[tok(BF16), tok(BF16), tok(BF16), tok(F32), tok(F32), tok(F32),
                   jax.ShapeDtypeStruct((CONV_W - 1, nb, W), F32),
                   jax.ShapeDtypeStruct((POOL_KEEP, nb, W), F32), tok(F32)],
        compiler_params=_cp(("arbitrary",), 40),
        name="sample_mix",
    )(proj_s, sc_t, sp_t, cw, cb, clg, clb, pw, ps, qg, kg, cos, sa, sb, glg, glb, ws, gb)


def _attn_s_kernel(pt_ref, q_ref, kn_ref, vn_ref, *refs, n_pages, ts):
    kp = refs[:n_pages]
    vp = refs[n_pages:2 * n_pages]
    o_ref = refs[2 * n_pages]
    kbuf, vbuf = refs[2 * n_pages + 1:]
    nh = N_HEADS
    prow = kp[0].shape[0]
    R, hd = q_ref.shape
    blk_rows = MOBA_BLOCK * nh
    ppb = blk_rows // prow
    nblk = n_pages // ppb
    scale = hd ** -0.5
    q = q_ref[...]
    rhead = lax.broadcasted_iota(jnp.int32, (R, 1), 0) >> 3
    t_row = lax.broadcasted_iota(jnp.int32, (R, 1), 0) & 7
    blkid = lax.broadcasted_iota(jnp.int32, (R, nblk), 1)
    bs = jnp.zeros((R, nblk), F32)
    for j in range(nblk):
        acc8 = None
        for u in range(ppb):
            pg = j * ppb + u
            kpg = kp[pg][...]
            kbuf[pg * prow:(pg + 1) * prow, :] = kpg.astype(BF16)
            vbuf[pg * prow:(pg + 1) * prow, :] = vp[pg][...].astype(BF16)
            part = jnp.sum(kpg.reshape(prow // 8, 8, hd), axis=0)
            acc8 = part if acc8 is None else acc8 + part
        kmean = (acc8[0:nh] + acc8[nh:2 * nh]) * (1.0 / MOBA_BLOCK)
        km_rows = jnp.concatenate([jnp.broadcast_to(kmean[h:h + 1], (8, hd)) for h in range(nh)], axis=0)
        bs = jnp.where(blkid == j, jnp.sum(q * km_rows, -1, keepdims=True), bs)
    sel = _topk_block_cols(bs, blkid >= 0)
    qb = q.astype(BF16)
    s_all = _dot_nt(qb, kbuf[...]) * scale
    own_head = (lax.broadcasted_iota(jnp.int32, (1, blk_rows), 1) & (nh - 1)) == rhead
    s_past = jnp.concatenate(
        [jnp.where(own_head & sel[j], s_all[:, j * blk_rows:(j + 1) * blk_rows], -jnp.inf)
         for j in range(nblk)], axis=-1)
    nc = kn_ref.shape[0]
    ocol = lax.broadcasted_iota(jnp.int32, (1, nc), 1)
    s_own = _dot_nt(qb, kn_ref[...].astype(BF16)) * scale
    s_own = jnp.where(((ocol & (nh - 1)) == rhead) & ((ocol >> 2) <= t_row), s_own, -jnp.inf)
    m = jnp.maximum(jnp.max(s_past, -1, keepdims=True), jnp.max(s_own, -1, keepdims=True))
    p_past = jnp.exp(s_past - m)
    p_own = jnp.exp(s_own - m)
    den = jnp.sum(p_past, -1, keepdims=True) + jnp.sum(p_own, -1, keepdims=True)
    acc = _dot(p_past.astype(BF16), vbuf[...]) + _dot(p_own.astype(BF16), vn_ref[...].astype(BF16))
    o_ref[...] = acc / den


def _attn_sample(pt_flat, q_hr, k_new, v_new, ck, cv, l, nb, ts, n_pages):
    prow, hd = ck.shape[2:]
    R = q_hr.shape[1]
    nc = k_new.shape[1]
    assert N_HEADS == 4 and nc % 8 == 0

    def page_spec(j):
        return pl.BlockSpec((None, None, prow, hd), lambda b, pt: (l, pt[b * n_pages + j], 0, 0))

    in_specs = ([pl.BlockSpec((None, R, hd), lambda b, pt: (b, 0, 0)),
                 pl.BlockSpec((None, nc, hd), lambda b, pt: (b, 0, 0)),
                 pl.BlockSpec((None, nc, hd), lambda b, pt: (b, 0, 0))]
                + [page_spec(j) for j in range(n_pages)] + [page_spec(j) for j in range(n_pages)])
    return pl.pallas_call(
        functools.partial(_attn_s_kernel, n_pages=n_pages, ts=ts),
        grid_spec=pltpu.PrefetchScalarGridSpec(
            num_scalar_prefetch=1, grid=(nb,), in_specs=in_specs,
            out_specs=pl.BlockSpec((None, R, hd), lambda b, pt: (b, 0, 0)),
            scratch_shapes=[pltpu.VMEM((n_pages * prow, hd), BF16), pltpu.VMEM((n_pages * prow, hd), BF16)]),
        out_shape=jax.ShapeDtypeStruct((nb, R, hd), F32),
        compiler_params=_cp(("arbitrary",), 48),
        name="attn_sample",
    )(pt_flat, q_hr, k_new, v_new, *([ck] * n_pages), *([cv] * n_pages))


def _merge_kernel(ya_ref, yb_ref, yc_ref, yd_ref, wa_ref, wb_ref, wc_ref, wd_ref,
                  ga_ref, gb_ref, gc_ref, gd_ref, o_ref, wbf_ref):
    w_refs = (wa_ref, wb_ref, wc_ref, wd_ref)

    @pl.when(pl.program_id(1) == 0)
    def _():
        for b in range(N_BRANCH):
            wbf_ref[b] = w_refs[b][...].astype(BF16)

    acc = None
    for b, (y_ref, g_ref) in enumerate(zip((ya_ref, yb_ref, yc_ref, yd_ref),
                                           (ga_ref, gb_ref, gc_ref, gd_ref))):
        term = _sigmoid(g_ref[...]) * _dot(y_ref[...], wbf_ref[b])
        acc = term if acc is None else acc + term
    o_ref[...] = acc.astype(o_ref.dtype)


def _merge(ys, ws, proj, l, W, D):
    M = proj.shape[0]
    tm, tn = 512, 1024
    npb = D // tn
    gate0 = 8 * W // tn
    y_spec = pl.BlockSpec((tm, W), lambda n, m: (m, 0))
    w_spec = pl.BlockSpec((None, W, tn), lambda n, m: (l, 0, n))
    g_specs = [pl.BlockSpec((tm, tn), functools.partial(lambda n, m, b: (m, gate0 + b * npb + n), b=b))
               for b in range(N_BRANCH)]
    return pl.pallas_call(
        _merge_kernel,
        grid=(npb, M // tm),
        in_specs=[y_spec] * 4 + [w_spec] * 4 + g_specs,
        out_specs=pl.BlockSpec((tm, tn), lambda n, m: (m, n)),
        out_shape=jax.ShapeDtypeStruct((M, D), BF16),
        scratch_shapes=[pltpu.VMEM((N_BRANCH, W, tn), BF16)],
        compiler_params=_cp(("arbitrary", "arbitrary"), 48),
        name="merge",
    )(*ys, *ws, proj, proj, proj, proj)


def _wo_kernel(a_ref, w_ref, x_ref, g_ref, o_ref, wbf_ref):
    @pl.when(pl.program_id(1) == 0)
    def _():
        wbf_ref[...] = w_ref[...].astype(BF16)

    o_ref[...] = x_ref[...] + g_ref[...] * _dot(a_ref[...], wbf_ref[...])


def _wo_residual(merged, w_o, l, x, mr, gate_chunk, seq_of_tile):
    M, D = x.shape
    tn = 1024
    ncb = D // tn
    return pl.pallas_call(
        _wo_kernel,
        grid=(ncb, M // TM),
        in_specs=[pl.BlockSpec((TM, D), lambda n, m: (m, 0)),
                  pl.BlockSpec((None, D, tn), lambda n, m: (l, 0, n)),
                  pl.BlockSpec((TM, tn), lambda n, m: (m, n)),
                  pl.BlockSpec((None, TM, tn), lambda n, m: (seq_of_tile(m), 0, gate_chunk * ncb + n))],
        out_specs=pl.BlockSpec((TM, tn), lambda n, m: (m, n)),
        out_shape=jax.ShapeDtypeStruct((M, D), F32),
        scratch_shapes=[pltpu.VMEM((D, tn), BF16)],
        compiler_params=_cp(("arbitrary", "arbitrary"), 40),
        name="wo_residual",
    )(merged, w_o, x, mr)


def _pack_bf16_pairs(x):
    n = x.shape[1] // 2
    lo = lax.bitcast_convert_type(x[:, :n].astype(BF16).astype(F32), jnp.uint32)
    hi = lax.bitcast_convert_type(x[:, n:].astype(BF16).astype(F32), jnp.uint32)
    return hi | (lo >> 16)


def _unpack_bf16_pairs(w):
    lo = lax.bitcast_convert_type(w << 16, F32).astype(BF16)
    hi = lax.bitcast_convert_type(w & jnp.uint32(0xFFFF0000), F32).astype(BF16)
    return lo, hi


def _norm_router_kernel(x_ref, g_ref, sc_ref, sh_ref, rw_ref, rb_ref,
                        h_ref, idx_ref, gate_ref, rank_ref, cnt_ref, run_ref):
    i = pl.program_id(0)
    E = rw_ref.shape[1]

    @pl.when(i == 0)
    def _():
        run_ref[...] = jnp.zeros(run_ref.shape, F32)

    x = x_ref[...]
    ms = jnp.mean(x * x, -1, keepdims=True)
    y = x * lax.rsqrt(ms + EPS) * g_ref[...]
    h = y * (1.0 + sc_ref[...]) + sh_ref[...]
    h_ref[...] = _pack_bf16_pairs(h)
    logits = jnp.dot(h, rw_ref[...], precision=lax.Precision.HIGHEST,
                     preferred_element_type=F32) + rb_ref[...]

    eidx = lax.broadcasted_iota(jnp.int32, (TM, E), 1).astype(F32)
    kcol = lax.broadcasted_iota(jnp.int32, (TM, TOP_K), 1)
    r_i = lax.broadcasted_iota(jnp.int32, (TM, TM), 0)
    c_i = lax.broadcasted_iota(jnp.int32, (TM, TM), 1)
    before = jnp.where(c_i < r_i, 1.0, 0.0).astype(BF16)
    work = logits
    vals, hots = [], []
    idx_out = jnp.zeros((TM, TOP_K), F32)
    rank_out = jnp.zeros((TM, TOP_K), F32)
    run = run_ref[...]
    for k in range(TOP_K):
        mx = jnp.max(work, -1, keepdims=True)
        am = jnp.min(jnp.where(work == mx, eidx, float(E)), -1, keepdims=True)
        hot = eidx == am
        work = jnp.where(hot, -jnp.inf, work)
        hot_f = jnp.where(hot, 1.0, 0.0)
        earlier = _dot(before, hot_f.astype(BF16))
        rank = jnp.sum(hot_f * (earlier + run), -1, keepdims=True)
        run = run + jnp.sum(hot_f, axis=0, keepdims=True)
        vals.append(mx)
        idx_out = jnp.where(kcol == k, am, idx_out)
        rank_out = jnp.where(kcol == k, rank, rank_out)
    run_ref[...] = run
    cnt_ref[...] = run
    den = jnp.zeros((TM, 1), F32)
    gate_out = jnp.zeros((TM, TOP_K), F32)
    ex = [jnp.exp(v - vals[0]) for v in vals]
    for e_ in ex:
        den = den + e_
    for k in range(TOP_K):
        gate_out = jnp.where(kcol == k, ex[k] / den, gate_out)
    idx_ref[...] = idx_out.astype(jnp.int32)
    rank_ref[...] = rank_out.astype(jnp.int32)
    gate_ref[...] = gate_out


def _norm_router(x, g, mr, sc_chunk, sh_chunk, seq_of_tile, rw, rb):
    M, D = x.shape
    E = rw.shape[1]
    tok = lambda c: pl.BlockSpec((TM, c), lambda i: (i, 0))
    return pl.pallas_call(
        _norm_router_kernel,
        grid=(M // TM,),
        in_specs=[tok(D),
                  pl.BlockSpec((1, D), lambda i: (0, 0)),
                  pl.BlockSpec((None, TM, D), lambda i: (seq_of_tile(i), 0, sc_chunk)),
                  pl.BlockSpec((None, TM, D), lambda i: (seq_of_tile(i), 0, sh_chunk)),
                  pl.BlockSpec((D, E), lambda i: (0, 0)),
                  pl.BlockSpec((1, E), lambda i: (0, 0))],
        out_specs=[tok(D // 2), tok(TOP_K), tok(TOP_K), tok(TOP_K), pl.BlockSpec((1, E), lambda i: (0, 0))],
        out_shape=[jax.ShapeDtypeStruct((M, D // 2), jnp.uint32), jax.ShapeDtypeStruct((M, TOP_K), jnp.int32),
                   jax.ShapeDtypeStruct((M, TOP_K), F32), jax.ShapeDtypeStruct((M, TOP_K), jnp.int32),
                   jax.ShapeDtypeStruct((1, E), F32)],
        scratch_shapes=[pltpu.VMEM((1, E), F32)],
        compiler_params=_cp(("arbitrary",), 40),
        name="norm_router",
    )(x, g, mr, mr, rw, rb)


def _row_copy(src_hbm, dst, src_row, dst_row, sem):
    return pltpu.make_async_copy(src_hbm.at[pl.ds(src_row, 1)], dst.at[pl.ds(dst_row, 1)], sem)


DISPATCH_TM = 512


def _dispatch_kernel(dest_ref, h_hbm, xb_init_hbm, xb_hbm, sem):
    del xb_init_hbm
    i = pl.program_id(0)

    def issue(r, c):
        tok = i * DISPATCH_TM + r
        for k in range(TOP_K):
            _row_copy(h_hbm, xb_hbm, tok, dest_ref[tok * TOP_K + k], sem).start()
        return c

    lax.fori_loop(0, DISPATCH_TM, issue, 0)

    def drain(r, c):
        for k in range(TOP_K):
            _row_copy(h_hbm, xb_hbm, 0, 0, sem).wait()
        return c

    lax.fori_loop(0, DISPATCH_TM, drain, 0)


def _dispatch(dest_flat, h2p, cap):
    M, Dp = h2p.shape
    any_spec = pl.BlockSpec(memory_space=pl.ANY)
    return pl.pallas_call(
        _dispatch_kernel,
        grid_spec=pltpu.PrefetchScalarGridSpec(
            num_scalar_prefetch=1, grid=(M // DISPATCH_TM,),
            in_specs=[any_spec, any_spec], out_specs=any_spec,
            scratch_shapes=[pltpu.SemaphoreType.DMA(())]),
        out_shape=jax.ShapeDtypeStruct((cap, Dp), jnp.uint32),
        input_output_aliases={2: 0},
        compiler_params=_cp(("arbitrary",), 16),
        name="moe_dispatch",
    )(dest_flat, h2p, jnp.zeros((cap, Dp), jnp.uint32))


def _expert_weight_run(be_ref, nu_ref, nxt_ref, n_panels, copies, consume, slot_ref):
    n = pl.program_id(0)
    m = pl.program_id(1)
    e = be_ref[m]
    prev = be_ref[jnp.maximum(m - 1, 0)]

    @pl.when((m < nu_ref[0]) & ((m == 0) | (e != prev)))
    def _():
        @pl.when((n == 0) & (m == 0))
        def _():
            slot_ref[0] = 0
            for c in copies(0, e, n):
                c.start()

        slot = slot_ref[0]
        for c in copies(slot, e, n):
            c.wait()
        consume(slot)
        ne = nxt_ref[m]

        @pl.when(ne >= 0)
        def _():
            for c in copies(1 - slot, ne, n):
                c.start()

        @pl.when((ne < 0) & (n + 1 < n_panels))
        def _():
            for c in copies(1 - slot, be_ref[0], n + 1):
                c.start()

        slot_ref[0] = 1 - slot


def _moe_up_kernel(be_ref, nu_ref, nxt_ref, x_ref, w_hbm, bg_ref, bl_ref, o_ref,
                   wbuf, wgb, wlb, slot_ref, sem, *, row0, tn, nf):
    m = pl.program_id(1)

    def copies(slot, e, n):
        return [pltpu.make_async_copy(w_hbm.at[row0 + e, :, pl.ds(pl.multiple_of((half * nf + n) * tn, tn), tn)],
                                      wbuf.at[slot, half], sem.at[slot, half]) for half in range(2)]

    def consume(slot):
        wgb[...] = wbuf[slot, 0].astype(BF16)
        wlb[...] = wbuf[slot, 1].astype(BF16)

    _expert_weight_run(be_ref, nu_ref, nxt_ref, nf, copies, consume, slot_ref)

    @pl.when(m < nu_ref[0])
    def _():
        x_lo, x_hi = _unpack_bf16_pairs(x_ref[...])
        n = x_lo.shape[1]
        hg = _dot(x_lo, wgb[0:n, :]) + _dot(x_hi, wgb[n:2 * n, :]) + bg_ref[...]
        hl = _dot(x_lo, wlb[0:n, :]) + _dot(x_hi, wlb[n:2 * n, :]) + bl_ref[...]
        hg = jnp.minimum(hg, SWIGLU_LIMIT)
        hl = jnp.clip(hl, -SWIGLU_LIMIT, SWIGLU_LIMIT)
        o_ref[...] = (hg * _sigmoid(SWIGLU_ALPHA * hg) * (hl + 1.0)).astype(o_ref.dtype)

    @pl.when(m >= nu_ref[0])
    def _():
        o_ref[...] = jnp.zeros(o_ref.shape, o_ref.dtype)


def _moe_up(blk_expert, n_used, nxt_expert, xb, w1, b1, l, E):
    cap, Dp = xb.shape
    D = w1.shape[1]
    F = w1.shape[2] // 2
    tn = 1024
    nf = F // tn
    n_blocks = cap // MOE_BM
    return pl.pallas_call(
        functools.partial(_moe_up_kernel, row0=l * E, tn=tn, nf=nf),
        grid_spec=pltpu.PrefetchScalarGridSpec(
            num_scalar_prefetch=3, grid=(nf, n_blocks),
            in_specs=[pl.BlockSpec((MOE_BM, Dp), lambda n, m, be, nu, nx: (m, 0)),
                      pl.BlockSpec(memory_space=pl.ANY),
                      pl.BlockSpec((None, 1, tn), lambda n, m, be, nu, nx: (l * E + be[m], 0, n)),
                      pl.BlockSpec((None, 1, tn), lambda n, m, be, nu, nx: (l * E + be[m], 0, nf + n))],
            out_specs=pl.BlockSpec((MOE_BM, tn), lambda n, m, be, nu, nx: (m, n)),
            scratch_shapes=[pltpu.VMEM((2, 2, D, tn), F32), pltpu.VMEM((D, tn), BF16), pltpu.VMEM((D, tn), BF16),
                            pltpu.SMEM((1,), jnp.int32), pltpu.SemaphoreType.DMA((2, 2))]),
        out_shape=jax.ShapeDtypeStruct((cap, F), BF16),
        compiler_params=_cp(("arbitrary", "arbitrary"), 56),
        name="moe_up",
    )(blk_expert, n_used, nxt_expert, xb, w1, b1, b1)


def _moe_down_kernel(be_ref, nu_ref, nxt_ref, a_ref, w_hbm, b_ref, o_ref, wbuf, wb, slot_ref, sem, *, row0):
    m = pl.program_id(1)

    def copies(slot, e, n):
        return [pltpu.make_async_copy(w_hbm.at[row0 + e], wbuf.at[slot], sem.at[slot])]

    def consume(slot):
        wb[...] = wbuf[slot].astype(BF16)

    _expert_weight_run(be_ref, nu_ref, nxt_ref, 1, copies, consume, slot_ref)

    @pl.when(m < nu_ref[0])
    def _():
        o_ref[...] = _dot(a_ref[...], wb[...]) + b_ref[...]

    @pl.when(m >= nu_ref[0])
    def _():
        o_ref[...] = jnp.zeros(o_ref.shape, o_ref.dtype)


def _moe_down(blk_expert, n_used, nxt_expert, act, w2, b2, l, E):
    cap, F = act.shape
    D = w2.shape[2]
    n_blocks = cap // MOE_BM
    return pl.pallas_call(
        functools.partial(_moe_down_kernel, row0=l * E),
        grid_spec=pltpu.PrefetchScalarGridSpec(
            num_scalar_prefetch=3, grid=(1, n_blocks),
            in_specs=[pl.BlockSpec((MOE_BM, F), lambda n, m, be, nu, nx: (m, 0)),
                      pl.BlockSpec(memory_space=pl.ANY),
                      pl.BlockSpec((None, 1, D), lambda n, m, be, nu, nx: (l * E + be[m], 0, 0))],
            out_specs=pl.BlockSpec((MOE_BM, D), lambda n, m, be, nu, nx: (m, 0)),
            scratch_shapes=[pltpu.VMEM((2, F, D), F32), pltpu.VMEM((F, D), BF16),
                            pltpu.SMEM((1,), jnp.int32), pltpu.SemaphoreType.DMA((2,))]),
        out_shape=jax.ShapeDtypeStruct((cap, D), F32),
        compiler_params=_cp(("arbitrary", "arbitrary"), 56),
        name="moe_down",
    )(blk_expert, n_used, nxt_expert, act, w2, b2)


COMBINE_TM = 128


def _combine_kernel(slot_ref, y_hbm, x_ref, g_ref, gate_ref, o_ref, buf, sem):
    i = pl.program_id(0)

    def gather(tile, s, start):
        def body(r, c):
            for k in range(TOP_K):
                row = slot_ref[(tile * COMBINE_TM + r) * TOP_K + k] if start else 0
                cp = _row_copy(y_hbm, buf.at[s, k], row, r, sem.at[s])
                cp.start() if start else cp.wait()
            return c

        lax.fori_loop(0, COMBINE_TM, body, 0)

    @pl.when(i == 0)
    def _():
        gather(0, 0, True)

    @pl.when(i + 1 < pl.num_programs(0))
    def _():
        gather(i + 1, (i + 1) & 1, True)

    s = i & 1
    gather(i, s, False)
    y = gate_ref[:, 0:1] * buf[s, 0]
    for k in range(1, TOP_K):
        y = y + gate_ref[:, k:k + 1] * buf[s, k]
    o_ref[...] = x_ref[...] + g_ref[...] * y


def _combine(slot_of, yb, x, mr, gate_chunk, seq_of_tile, gate):
    M, D = x.shape
    per = TM // COMBINE_TM
    return pl.pallas_call(
        _combine_kernel,
        grid_spec=pltpu.PrefetchScalarGridSpec(
            num_scalar_prefetch=1, grid=(M // COMBINE_TM,),
            in_specs=[pl.BlockSpec(memory_space=pl.ANY),
                      pl.BlockSpec((COMBINE_TM, D), lambda i, s: (i, 0)),
                      pl.BlockSpec((None, COMBINE_TM, D), lambda i, s: (seq_of_tile(i // per), 0, gate_chunk)),
                      pl.BlockSpec((COMBINE_TM, TOP_K), lambda i, s: (i, 0))],
            out_specs=pl.BlockSpec((COMBINE_TM, D), lambda i, s: (i, 0)),
            scratch_shapes=[pltpu.VMEM((2, TOP_K, COMBINE_TM, D), F32), pltpu.SemaphoreType.DMA((2,))]),
        out_shape=jax.ShapeDtypeStruct((M, D), F32),
        compiler_params=_cp(("arbitrary",), 32),
        name="moe_combine",
    )(slot_of, yb, x, mr, gate)


def _slot_tables(top_idx, rank, counts, n_blocks):
    E = counts.shape[0]
    padded = ((counts + MOE_BM - 1) // MOE_BM) * MOE_BM
    pad_end = jnp.cumsum(padded)
    pad_start = pad_end - padded
    onehot = top_idx[..., None] == jnp.arange(E, dtype=jnp.int32)
    dest = jnp.sum(jnp.where(onehot, pad_start, 0), -1) + rank
    blk_start = jnp.arange(n_blocks, dtype=jnp.int32) * MOE_BM
    blk_expert = jnp.minimum(jnp.sum(blk_start[:, None] >= pad_end[None, :], -1), E - 1).astype(jnp.int32)
    n_used = (pad_end[-1] // MOE_BM).astype(jnp.int32).reshape(1)
    ids = jnp.arange(E, dtype=jnp.int32)
    later = (ids[None, :] > ids[:, None]) & (counts[None, :] > 0)
    nxt_of_expert = jnp.min(jnp.where(later, ids[None, :], E), axis=1)
    nxt_of_expert = jnp.where(nxt_of_expert == E, -1, nxt_of_expert)
    nxt_expert = jnp.sum(jnp.where(blk_expert[:, None] == ids[None, :], nxt_of_expert[None, :], 0), -1)
    return dest.reshape(-1).astype(jnp.int32), blk_expert, n_used, nxt_expert.astype(jnp.int32)


def _rope_tables(pos, hd):
    rot = hd // 4
    half = rot // 2
    inv = jnp.power(jnp.float32(ROPE_THETA), -jnp.arange(half, dtype=F32) * 2.0 / rot)
    ang = pos.astype(F32)[:, None] * inv[None, :]
    cos, sin = jnp.cos(ang), jnp.sin(ang)
    n = pos.shape[0]
    c = jnp.concatenate([cos, cos, jnp.ones((n, hd - rot), F32)], 1)
    sa = jnp.concatenate([-sin, jnp.zeros((n, hd - half), F32)], 1)
    sb = jnp.concatenate([jnp.zeros((n, half), F32), sin, jnp.zeros((n, hd - rot), F32)], 1)
    return c, sa, sb


def kernel(x_prompt, x_sample, cache_k, cache_v, state_conv, state_pool, page_table, c_prompt, c_sample,
           w_ada, b_ada, norm1_g, norm2_g, w_in, conv_w, conv_b, conv_ln_g, conv_ln_b, conv_out,
           pool_w, pool_scale, pool_out, q_norm_g, k_norm_g, attn_out, gmlp_ln_g, gmlp_ln_b,
           gmlp_ws, gmlp_b, gmlp_out, w_o, router_w, router_b, exp_w1, exp_b1, exp_w2, exp_b2):
    B, T, D = x_prompt.shape
    NB, TS, _ = x_sample.shape
    L = w_ada.shape[0]
    W = D // N_BRANCH
    HD = W // N_HEADS
    E = router_w.shape[2]
    n_pool, PAGE = cache_k.shape[1], cache_k.shape[2]
    n_pages = page_table.shape[1]
    past = n_pages * PAGE
    M_p, M_s = B * T, NB * TS
    M = M_p + M_s
    assert HD == LANE and W == N_HEADS * LANE and T % TM == 0 and TM % NB == 0 and NB % 8 == 0
    assert M % DISPATCH_TM == 0 and M_s % TM == 0 and TS <= 8 and past % MOBA_BLOCK == 0
    tiles_per_seq = T // TM

    def seq_of_tile(i):
        return jnp.minimum(i // tiles_per_seq, B)

    x = jnp.concatenate([x_prompt.reshape(M_p, D), x_sample.transpose(1, 0, 2).reshape(M_s, D)], 0)
    n_c = B + NB
    c_all = jnp.concatenate([c_prompt, c_sample, jnp.zeros((-n_c % 8, D), F32)], 0)
    mod_all = _ada(c_all, w_ada, b_ada)

    cos_p, sa_p, sb_p = _rope_tables(jnp.arange(T), HD)
    cos_s, sa_s, sb_s = [jnp.pad(t, ((0, 8 - TS), (0, 0))) for t in _rope_tables(past + jnp.arange(TS), HD)]
    pt_flat = page_table.reshape(-1).astype(jnp.int32)
    ck = cache_k.reshape(L, n_pool, PAGE * N_HEADS, HD)
    cv = cache_v.reshape(L, n_pool, PAGE * N_HEADS, HD)
    w1 = exp_w1.reshape(L * E, D, exp_w1.shape[3])
    b1 = exp_b1.reshape(L * E, 1, exp_b1.shape[2])
    w2 = exp_w2.reshape(L * E, exp_w2.shape[2], D)
    b2 = exp_b2.reshape(L * E, 1, D)

    def to_bt(a_t):
        return a_t.reshape(TS, NB, -1).transpose(1, 0, 2)

    outs = {k: [] for k in ('kp', 'vp', 'ks', 'vs', 'cp', 'cs', 'pp', 'ps', 'gp', 'gs')}
    for l in range(L):
        mod = mod_all[l]
        mr = jnp.concatenate([jnp.broadcast_to(mod[:B, None, :], (B, TM, 6 * D)),
                              jnp.tile(mod[B:B + NB], (TM // NB, 1))[None]], 0)
        h = _norm_mod(x, norm1_g[l][None], mr, 1, 0, seq_of_tile, BF16)
        tm_proj = next(t for t in (1088, 1024, 512, 256) if M % t == 0)
        proj = _panel_mm(h, w_in, l, tm_proj, 1024)

        row = lambda v: v[l][None]
        ya_p, conv_p = _conv_prompt(proj, B, T, W, conv_w[l], row(conv_b), row(conv_ln_g), row(conv_ln_b))
        yb_p, pool_p = _pool_prompt(proj, B, T, W, pool_w[l], row(pool_scale))
        q_p, k_p, v_p, kb_p, vb_p, kmean_p = _qkv_prompt(proj, B, T, W, row(q_norm_g), row(k_norm_g),
                                                        cos_p, sa_p, sb_p)
        yc_p = _attn_prompt(q_p, kb_p, vb_p, kmean_p, B, T, W)
        yd_p, gv_p = _gmlp_prompt(proj, B, T, W, row(gmlp_ln_g), row(gmlp_ln_b), gmlp_ws[l],
                                  jnp.transpose(gmlp_b[l]))

        proj_s = proj[M_p:, :8 * W].reshape(TS, NB, 8 * W)
        (ya_s, yb_s, yd_s, q_s, k_s, v_s, conv_s, pool_s, gv_s) = _sample_mix(
            proj_s, NB, TS, W, state_conv[l].transpose(1, 0, 2), state_pool[l].transpose(1, 0, 2),
            conv_w[l], row(conv_b), row(conv_ln_g), row(conv_ln_b), pool_w[l], row(pool_scale),
            row(q_norm_g), row(k_norm_g), cos_s, sa_s, sb_s, row(gmlp_ln_g), row(gmlp_ln_b),
            gmlp_ws[l], gmlp_b[l])
        q_hr = jnp.pad(to_bt(q_s).reshape(NB, TS, N_HEADS, HD).transpose(0, 2, 1, 3),
                       ((0, 0), (0, 0), (0, 8 - TS), (0, 0))).reshape(NB, N_HEADS * 8, HD)
        k_new = to_bt(k_s).reshape(NB, TS * N_HEADS, HD)
        v_new = to_bt(v_s).reshape(NB, TS * N_HEADS, HD)
        att_s = _attn_sample(pt_flat, q_hr, k_new, v_new, ck, cv, l, NB, TS, n_pages)
        yc_s = (att_s.reshape(NB, N_HEADS, 8, HD)[:, :, :TS].transpose(2, 0, 1, 3)
                .reshape(M_s, W).astype(BF16))

        ys = [jnp.concatenate([p_, s_.reshape(M_s, W)], 0) for p_, s_ in
              ((ya_p, ya_s), (yb_p, yb_s), (yc_p, yc_s), (yd_p, yd_s))]
        merged = _merge(ys, (conv_out, pool_out, attn_out, gmlp_out), proj, l, W, D)
        x = _wo_residual(merged, w_o, l, x, mr, 2, seq_of_tile)
        h2p, top_idx, gate, rank, counts = _norm_router(x, norm2_g[l][None], mr, 4, 3, seq_of_tile,
                                                         router_w[l], router_b[l][None])
        n_blocks = -(-M * TOP_K // MOE_BM) + E
        dest, blk_expert, n_used, nxt_expert = _slot_tables(top_idx, rank, counts[0].astype(jnp.int32), n_blocks)
        xb = _dispatch(dest, h2p, n_blocks * MOE_BM)
        act = _moe_up(blk_expert, n_used, nxt_expert, xb, w1, b1, l, E)
        yb = _moe_down(blk_expert, n_used, nxt_expert, act, w2, b2, l, E)
        x = _combine(dest, yb, x, mr, 5, seq_of_tile, gate)

        outs['kp'].append(k_p.reshape(B, T, N_HEADS, HD))
        outs['vp'].append(v_p.reshape(B, T, N_HEADS, HD))
        outs['ks'].append(to_bt(k_s).reshape(NB, TS, N_HEADS, HD))
        outs['vs'].append(to_bt(v_s).reshape(NB, TS, N_HEADS, HD))
        outs['cp'].append(conv_p)
        outs['cs'].append(conv_s.transpose(1, 0, 2))
        outs['pp'].append(pool_p)
        outs['ps'].append(pool_s.transpose(1, 0, 2))
        outs['gp'].append(gv_p)
        outs['gs'].append(to_bt(gv_s))

    y_prompt = x[:M_p].reshape(B, T, D)
    y_sample = to_bt(x[M_p:])
    st = lambda k: jnp.stack(outs[k])
    return (y_prompt, y_sample, st('kp'), st('vp'), st('ks'), st('vs'), st('cp'), st('cs'),
            st('pp'), st('ps'), st('gp'), st('gs'))
```

```python
import functools

import jax
import jax.numpy as jnp
from jax import lax
from jax.experimental import pallas as pl
from jax.experimental.pallas import tpu as pltpu

F32 = jnp.float32
BF16 = jnp.bfloat16

N_BRANCH = 4
CONV_W = 31
POOL_WINDOWS = (2, 4, 8, 16)
POOL_KEEP = max(POOL_WINDOWS) - 1
N_HEADS = 4
ROPE_THETA = 500000.0
MOBA_BLOCK = 256
MOBA_TOPK = 3
GMLP_CHUNK = 128
GMLP_GROUPS = 4
TOP_K = 4
SWIGLU_LIMIT = 7.0
SWIGLU_ALPHA = 1.702
EPS = 1e-6

LANE = 128
TM = 256
CONV_HIST = 32
POOL_HIST = 16
MOE_BM = 256
MIB = 1024 * 1024


def _cp(sem, vmem_mib):
    return pltpu.CompilerParams(dimension_semantics=sem, vmem_limit_bytes=vmem_mib * MIB)


def _sigmoid(x):
    return 1.0 / (1.0 + jnp.exp(-x))


def _layer_norm(x, g, b):
    mu = jnp.mean(x, -1, keepdims=True)
    xc = x - mu
    var = jnp.mean(xc * xc, -1, keepdims=True)
    return xc * lax.rsqrt(var + EPS) * g + b


def _dot(a, b):
    return jnp.dot(a, b, preferred_element_type=F32)


def _dot_nt(a, b, precision=None):
    return lax.dot_general(a, b, (((1,), (1,)), ((), ())), precision=precision,
                           preferred_element_type=F32)


def _ada_kernel(c_ref, w_ref, b_ref, o_ref):
    c = c_ref[...]
    s = (c * _sigmoid(c)).astype(BF16)
    o_ref[...] = _dot(s, w_ref[...].astype(BF16)) + b_ref[...]


def _ada(c_all, w_ada, b_ada):
    L, D, N = w_ada.shape
    R = c_all.shape[0]
    tn = 1024
    return pl.pallas_call(
        _ada_kernel,
        grid=(L, N // tn),
        in_specs=[pl.BlockSpec((R, D), lambda l, n: (0, 0)),
                  pl.BlockSpec((None, D, tn), lambda l, n: (l, 0, n)),
                  pl.BlockSpec((None, 1, tn), lambda l, n: (l, 0, n))],
        out_specs=pl.BlockSpec((None, R, tn), lambda l, n: (l, 0, n)),
        out_shape=jax.ShapeDtypeStruct((L, R, N), F32),
        compiler_params=_cp(("arbitrary", "arbitrary"), 40),
        name="ada",
    )(c_all, w_ada, b_ada.reshape(L, 1, N))


def _norm_mod_kernel(x_ref, g_ref, sc_ref, sh_ref, o_ref):
    x = x_ref[...]
    ms = jnp.mean(x * x, -1, keepdims=True)
    y = x * lax.rsqrt(ms + EPS) * g_ref[...]
    o_ref[...] = (y * (1.0 + sc_ref[...]) + sh_ref[...]).astype(o_ref.dtype)


def _norm_mod(x, g, mr, sc_chunk, sh_chunk, seq_of_tile, out_dtype):
    M, D = x.shape
    return pl.pallas_call(
        _norm_mod_kernel,
        grid=(M // TM,),
        in_specs=[pl.BlockSpec((TM, D), lambda i: (i, 0)),
                  pl.BlockSpec((1, D), lambda i: (0, 0)),
                  pl.BlockSpec((None, TM, D), lambda i: (seq_of_tile(i), 0, sc_chunk)),
                  pl.BlockSpec((None, TM, D), lambda i: (seq_of_tile(i), 0, sh_chunk))],
        out_specs=pl.BlockSpec((TM, D), lambda i: (i, 0)),
        out_shape=jax.ShapeDtypeStruct((M, D), out_dtype),
        compiler_params=_cp(("arbitrary",), 40),
        name="norm_mod",
    )(x, g, mr, mr)


def _panel_mm_kernel(a_ref, w_ref, o_ref, wbf_ref):
    @pl.when(pl.program_id(1) == 0)
    def _():
        wbf_ref[...] = w_ref[...].astype(BF16)

    o_ref[...] = _dot(a_ref[...], wbf_ref[...])


def _panel_mm(a, w, l, tm, tn):
    M, K = a.shape
    N = w.shape[2]
    return pl.pallas_call(
        _panel_mm_kernel,
        grid=(N // tn, M // tm),
        in_specs=[pl.BlockSpec((tm, K), lambda n, m: (m, 0)),
                  pl.BlockSpec((None, K, tn), lambda n, m: (l, 0, n))],
        out_specs=pl.BlockSpec((tm, tn), lambda n, m: (m, n)),
        out_shape=jax.ShapeDtypeStruct((M, N), F32),
        scratch_shapes=[pltpu.VMEM((K, tn), BF16)],
        compiler_params=_cp(("arbitrary", "arbitrary"), 48),
        name="in_proj",
    )(a, w)


def _conv_p_kernel(a_ref, g_ref, w_ref, cb_ref, lg_ref, lb_ref, y_ref, cn_ref, zbuf, ybuf):
    i = pl.program_id(1)
    W = a_ref.shape[1]

    @pl.when(i == 0)
    def _():
        zbuf[0:CONV_HIST, :] = jnp.zeros((CONV_HIST, W), F32)

    @pl.when(i > 0)
    def _():
        zbuf[0:CONV_HIST, :] = zbuf[TM:TM + CONV_HIST, :]

    zbuf[CONV_HIST:CONV_HIST + TM, :] = a_ref[...] * _sigmoid(g_ref[...])
    base = CONV_HIST - (CONV_W - 1)
    rb = 128
    for c0 in range(0, W, LANE):
        for r0 in range(0, TM, rb):
            acc = jnp.broadcast_to(cb_ref[:, c0:c0 + LANE], (rb, LANE))
            for j in range(CONV_W):
                acc = acc + w_ref[j:j + 1, c0:c0 + LANE] * zbuf[pl.ds(base + r0 + j, rb), c0:c0 + LANE]
            ybuf[r0:r0 + rb, c0:c0 + LANE] = acc
    y = _layer_norm(ybuf[...], lg_ref[...], lb_ref[...])
    y_ref[...] = (y * _sigmoid(y)).astype(y_ref.dtype)

    @pl.when(i == pl.num_programs(1) - 1)
    def _():
        cn_ref[...] = zbuf[CONV_HIST + TM - (CONV_W - 1):CONV_HIST + TM, :]


def _conv_prompt(proj, B, T, W, cw, cb, lg, lb):
    nt = T // TM
    return pl.pallas_call(
        _conv_p_kernel,
        grid=(B, nt),
        in_specs=[pl.BlockSpec((TM, W), lambda b, i: (b * nt + i, 0)),
                  pl.BlockSpec((TM, W), lambda b, i: (b * nt + i, 1)),
                  pl.BlockSpec((CONV_W, W), lambda b, i: (0, 0)),
                  pl.BlockSpec((1, W), lambda b, i: (0, 0)),
                  pl.BlockSpec((1, W), lambda b, i: (0, 0)),
                  pl.BlockSpec((1, W), lambda b, i: (0, 0))],
        out_specs=[pl.BlockSpec((TM, W), lambda b, i: (b * nt + i, 0)),
                   pl.BlockSpec((None, CONV_W - 1, W), lambda b, i: (b, 0, 0))],
        out_shape=[jax.ShapeDtypeStruct((B * T, W), BF16),
                   jax.ShapeDtypeStruct((B, CONV_W - 1, W), F32)],
        scratch_shapes=[pltpu.VMEM((CONV_HIST + TM, W), F32), pltpu.VMEM((TM, W), F32)],
        compiler_params=_cp(("arbitrary", "arbitrary"), 32),
        name="conv_prompt",
    )(proj, proj, cw, cb, lg, lb)


def _pool_p_kernel(p_ref, pw_ref, ps_ref, y_ref, pn_ref, ebuf):
    i = pl.program_id(1)
    W = p_ref.shape[1]

    @pl.when(i == 0)
    def _():
        ebuf[0:POOL_HIST, :] = jnp.zeros((POOL_HIST, W), F32)

    @pl.when(i > 0)
    def _():
        ebuf[0:POOL_HIST, :] = ebuf[TM:TM + POOL_HIST, :]

    ebuf[POOL_HIST:POOL_HIST + TM, :] = p_ref[...]
    t_abs = i * TM + lax.broadcasted_iota(jnp.int32, (TM, 1), 0)
    for gi, win in enumerate(POOL_WINDOWS):
        c0 = gi * LANE
        s = ebuf[POOL_HIST:POOL_HIST + TM, c0:c0 + LANE]
        for k in range(1, win):
            s = s + ebuf[pl.ds(POOL_HIST - k, TM), c0:c0 + LANE]
        cnt = jnp.minimum(t_abs + 1, win).astype(F32)
        d = s / cnt - p_ref[:, c0:c0 + LANE]
        yg = _dot(d.astype(BF16), pw_ref[gi].astype(BF16)) * ps_ref[:, c0:c0 + LANE]
        y_ref[:, c0:c0 + LANE] = yg.astype(y_ref.dtype)

    @pl.when(i == pl.num_programs(1) - 1)
    def _():
        pn_ref[...] = ebuf[POOL_HIST + TM - POOL_KEEP:POOL_HIST + TM, :]


def _pool_prompt(proj, B, T, W, pw, ps):
    nt = T // TM
    G = len(POOL_WINDOWS)
    return pl.pallas_call(
        _pool_p_kernel,
        grid=(B, nt),
        in_specs=[pl.BlockSpec((TM, W), lambda b, i: (b * nt + i, 2)),
                  pl.BlockSpec((G, LANE, LANE), lambda b, i: (0, 0, 0)),
                  pl.BlockSpec((1, W), lambda b, i: (0, 0))],
        out_specs=[pl.BlockSpec((TM, W), lambda b, i: (b * nt + i, 0)),
                   pl.BlockSpec((None, POOL_KEEP, W), lambda b, i: (b, 0, 0))],
        out_shape=[jax.ShapeDtypeStruct((B * T, W), BF16),
                   jax.ShapeDtypeStruct((B, POOL_KEEP, W), F32)],
        scratch_shapes=[pltpu.VMEM((POOL_HIST + TM, W), F32)],
        compiler_params=_cp(("arbitrary", "arbitrary"), 32),
        name="pool_prompt",
    )(proj, pw, ps)


def _norm_rope_head(xh, g, cos, sa, sb):
    ms = jnp.mean(xh * xh, -1, keepdims=True)
    xn = xh * lax.rsqrt(ms + EPS) * g
    hd = xh.shape[1]
    half = hd // 8
    up = pltpu.roll(xn, hd - half, axis=1)
    dn = pltpu.roll(xn, half, axis=1)
    return xn * cos + up * sa + dn * sb


def _qkv_p_kernel(q_ref, k_ref, v_ref, qg_ref, kg_ref, cos_ref, sa_ref, sb_ref,
                  qo_ref, ko_ref, vo_ref, kb_ref, vb_ref, km_ref):
    i = pl.program_id(1)
    cos, sa, sb = cos_ref[...], sa_ref[...], sb_ref[...]

    @pl.when(i == 0)
    def _():
        km_ref[...] = jnp.zeros(km_ref.shape, F32)

    blk_row = lax.broadcasted_iota(jnp.int32, (km_ref.shape[0], LANE), 0)
    for h in range(N_HEADS):
        c0 = h * LANE
        qo_ref[:, c0:c0 + LANE] = _norm_rope_head(q_ref[:, c0:c0 + LANE], qg_ref[...], cos, sa, sb)
        kh = _norm_rope_head(k_ref[:, c0:c0 + LANE], kg_ref[...], cos, sa, sb)
        ko_ref[:, c0:c0 + LANE] = kh
        kb_ref[:, c0:c0 + LANE] = kh.astype(BF16)
        km_ref[:, c0:c0 + LANE] = jnp.where(blk_row == i, jnp.mean(kh, axis=0, keepdims=True),
                                            km_ref[:, c0:c0 + LANE])
    v = v_ref[...]
    vo_ref[...] = v
    vb_ref[...] = v.astype(BF16)


def _qkv_prompt(proj, B, T, W, qg, kg, cos, sa, sb):
    nt = T // TM
    row = lambda b, i: (b * nt + i, 0)
    return pl.pallas_call(
        _qkv_p_kernel,
        grid=(B, nt),
        in_specs=[pl.BlockSpec((TM, W), lambda b, i: (b * nt + i, 3)),
                  pl.BlockSpec((TM, W), lambda b, i: (b * nt + i, 4)),
                  pl.BlockSpec((TM, W), lambda b, i: (b * nt + i, 5)),
                  pl.BlockSpec((1, LANE), lambda b, i: (0, 0)),
                  pl.BlockSpec((1, LANE), lambda b, i: (0, 0)),
                  pl.BlockSpec((TM, LANE), lambda b, i: (i, 0)),
                  pl.BlockSpec((TM, LANE), lambda b, i: (i, 0)),
                  pl.BlockSpec((TM, LANE), lambda b, i: (i, 0))],
        out_specs=[pl.BlockSpec((TM, W), row), pl.BlockSpec((TM, W), row), pl.BlockSpec((TM, W), row),
                   pl.BlockSpec((TM, W), row), pl.BlockSpec((TM, W), row),
                   pl.BlockSpec((nt, W), lambda b, i: (b, 0))],
        out_shape=[jax.ShapeDtypeStruct((B * T, W), F32), jax.ShapeDtypeStruct((B * T, W), F32),
                   jax.ShapeDtypeStruct((B * T, W), F32), jax.ShapeDtypeStruct((B * T, W), BF16),
                   jax.ShapeDtypeStruct((B * T, W), BF16), jax.ShapeDtypeStruct((B * nt, W), F32)],
        compiler_params=_cp(("arbitrary", "arbitrary"), 32),
        name="qkv_prompt",
    )(proj, proj, proj, qg, kg, cos, sa, sb)


def _topk_block_cols(bs, past):
    nb = bs.shape[1]
    jidx = lax.broadcasted_iota(jnp.int32, bs.shape, 1)
    cols = []
    for n in range(nb):
        bn = bs[:, n:n + 1]
        beats = ((bs > bn) | ((bs == bn) & (jidx < n))) & past
        rank = jnp.sum(beats.astype(F32), axis=-1, keepdims=True)
        cols.append(rank < (MOBA_TOPK - 0.5))
    return cols


def _attn_p_kernel(q_ref, k_ref, v_ref, km_ref, o_ref):
    i = pl.program_id(1)
    nb = km_ref.shape[0]
    scale = LANE ** -0.5
    row = lax.broadcasted_iota(jnp.int32, (TM, MOBA_BLOCK), 0)
    col = lax.broadcasted_iota(jnp.int32, (TM, MOBA_BLOCK), 1)
    tri = col <= row
    blk = lax.broadcasted_iota(jnp.int32, (TM, nb), 1)

    def attend(own):
        nk = (own + 1) * MOBA_BLOCK
        for h in range(N_HEADS):
            hs = slice(h * LANE, (h + 1) * LANE)
            q = q_ref[:, hs]
            s = _dot_nt(q.astype(BF16), k_ref[0:nk, hs]) * scale
            pieces = []
            if own > 0:
                bs = _dot_nt(q, km_ref[:, hs], precision=lax.Precision.HIGHEST)
                sel = _topk_block_cols(bs, blk < own)
                pieces = [jnp.where(sel[j], s[:, j * MOBA_BLOCK:(j + 1) * MOBA_BLOCK], -jnp.inf)
                          for j in range(own)]
            pieces.append(jnp.where(tri, s[:, own * MOBA_BLOCK:nk], -jnp.inf))
            s = jnp.concatenate(pieces, axis=-1) if own > 0 else pieces[0]
            m = jnp.max(s, -1, keepdims=True)
            p = jnp.exp(s - m)
            den = jnp.sum(p, -1, keepdims=True)
            o_ref[:, hs] = (_dot(p.astype(BF16), v_ref[0:nk, hs]) / den).astype(o_ref.dtype)

    for own in range(nb):
        pl.when(i == own)(functools.partial(attend, own))


def _attn_prompt(q, kb, vb, kmean, B, T, W):
    nt = T // TM
    assert TM == MOBA_BLOCK
    return pl.pallas_call(
        _attn_p_kernel,
        grid=(B, nt),
        in_specs=[pl.BlockSpec((TM, W), lambda b, i: (b * nt + i, 0)),
                  pl.BlockSpec((T, W), lambda b, i: (b, 0)),
                  pl.BlockSpec((T, W), lambda b, i: (b, 0)),
                  pl.BlockSpec((nt, W), lambda b, i: (b, 0))],
        out_specs=pl.BlockSpec((TM, W), lambda b, i: (b * nt + i, 0)),
        out_shape=jax.ShapeDtypeStruct((B * T, W), BF16),
        compiler_params=_cp(("arbitrary", "arbitrary"), 40),
        name="attn_prompt",
    )(q, kb, vb, kmean)


def _gmlp_p_kernel(u_ref, v_ref, lg_ref, lb_ref, ws_ref, bt_ref, y_ref, gv_ref, vbuf):
    i = pl.program_id(1)
    vbuf[...] = _layer_norm(v_ref[...], lg_ref[...], lb_ref[...])
    row = lax.broadcasted_iota(jnp.int32, (GMLP_CHUNK, GMLP_CHUNK), 0)
    col = lax.broadcasted_iota(jnp.int32, (GMLP_CHUNK, GMLP_CHUNK), 1)
    tri = col <= row
    for g in range(GMLP_GROUPS):
        c0 = g * LANE
        w = jnp.where(tri, ws_ref[g], 0.0).astype(BF16)
        for r0 in range(0, TM, GMLP_CHUNK):
            mixed = _dot(w, vbuf[r0:r0 + GMLP_CHUNK, c0:c0 + LANE].astype(BF16)) + bt_ref[:, g:g + 1]
            y_ref[r0:r0 + GMLP_CHUNK, c0:c0 + LANE] = (
                u_ref[r0:r0 + GMLP_CHUNK, c0:c0 + LANE] * mixed).astype(y_ref.dtype)

    @pl.when(i == pl.num_programs(1) - 1)
    def _():
        gv_ref[...] = vbuf[TM - GMLP_CHUNK:TM, :]


def _gmlp_prompt(proj, B, T, W, lg, lb, ws, bt):
    nt = T // TM
    return pl.pallas_call(
        _gmlp_p_kernel,
        grid=(B, nt),
        in_specs=[pl.BlockSpec((TM, W), lambda b, i: (b * nt + i, 6)),
                  pl.BlockSpec((TM, W), lambda b, i: (b * nt + i, 7)),
                  pl.BlockSpec((1, W), lambda b, i: (0, 0)),
                  pl.BlockSpec((1, W), lambda b, i: (0, 0)),
                  pl.BlockSpec((GMLP_GROUPS, GMLP_CHUNK, GMLP_CHUNK), lambda b, i: (0, 0, 0)),
                  pl.BlockSpec((GMLP_CHUNK, GMLP_GROUPS), lambda b, i: (0, 0))],
        out_specs=[pl.BlockSpec((TM, W), lambda b, i: (b * nt + i, 0)),
                   pl.BlockSpec((None, GMLP_CHUNK, W), lambda b, i: (b, 0, 0))],
        out_shape=[jax.ShapeDtypeStruct((B * T, W), BF16),
                   jax.ShapeDtypeStruct((B, GMLP_CHUNK, W), F32)],
        scratch_shapes=[pltpu.VMEM((TM, W), F32)],
        compiler_params=_cp(("arbitrary", "arbitrary"), 32),
        name="gmlp_prompt",
    )(proj, proj, lg, lb, ws, bt)


def _sample_mix_kernel(p_ref, sc_ref, sp_ref, cw_ref, cb_ref, clg_ref, clb_ref, pw_ref, ps_ref,
                       qg_ref, kg_ref, cos_ref, sa_ref, sb_ref, glg_ref, glb_ref, ws_ref, gb_ref,
                       ya_ref, yb_ref, yd_ref, q_ref, k_ref, v_ref, cn_ref, pn_ref, gv_ref):
    ts, sb_rows, _ = p_ref.shape
    W = ya_ref.shape[2]
    n_conv = CONV_W - 1

    def col(c):
        return slice(c * W, (c + 1) * W)

    z = [p_ref[t, :, col(0)] * _sigmoid(p_ref[t, :, col(1)]) for t in range(ts)]

    def zext(r):
        return sc_ref[r] if r < n_conv else z[r - n_conv]

    for t in range(ts):
        acc = jnp.broadcast_to(cb_ref[...], (sb_rows, W))
        for j in range(CONV_W):
            acc = acc + cw_ref[j:j + 1, :] * zext(t + j)
        y = _layer_norm(acc, clg_ref[...], clb_ref[...])
        ya_ref[t] = (y * _sigmoid(y)).astype(ya_ref.dtype)
    for r in range(n_conv):
        cn_ref[r] = zext(r + ts)

    def pext(r, c0):
        if r < POOL_KEEP:
            return sp_ref[r, :, c0:c0 + LANE]
        return p_ref[r - POOL_KEEP, :, 2 * W + c0:2 * W + c0 + LANE]

    for t in range(ts):
        for gi, win in enumerate(POOL_WINDOWS):
            c0 = gi * LANE
            s = pext(POOL_KEEP + t, c0)
            for k in range(1, win):
                s = s + pext(POOL_KEEP + t - k, c0)
            d = s / float(win) - pext(POOL_KEEP + t, c0)
            yg = _dot(d.astype(BF16), pw_ref[gi].astype(BF16)) * ps_ref[:, c0:c0 + LANE]
            yb_ref[t, :, c0:c0 + LANE] = yg.astype(yb_ref.dtype)
    for r in range(POOL_KEEP):
        pn_ref[r] = sp_ref[r + ts] if r + ts < POOL_KEEP else p_ref[r + ts - POOL_KEEP, :, col(2)]

    for t in range(ts):
        cos = cos_ref[t:t + 1, :]
        sa = sa_ref[t:t + 1, :]
        sb = sb_ref[t:t + 1, :]
        for h in range(N_HEADS):
            c0 = h * LANE
            q_ref[t, :, c0:c0 + LANE] = _norm_rope_head(
                p_ref[t, :, 3 * W + c0:3 * W + c0 + LANE], qg_ref[...], cos, sa, sb)
            k_ref[t, :, c0:c0 + LANE] = _norm_rope_head(
                p_ref[t, :, 4 * W + c0:4 * W + c0 + LANE], kg_ref[...], cos, sa, sb)
        v_ref[t] = p_ref[t, :, col(5)]

    vn = [_layer_norm(p_ref[t, :, col(7)], glg_ref[...], glb_ref[...]) for t in range(ts)]
    for t in range(ts):
        gv_ref[t] = vn[t]
        for g in range(GMLP_GROUPS):
            c0 = g * LANE
            mixed = jnp.broadcast_to(gb_ref[g:g + 1, t:t + 1], (sb_rows, LANE))
            for s_ in range(t + 1):
                mixed = mixed + ws_ref[g, t:t + 1, s_:s_ + 1] * vn[s_][:, c0:c0 + LANE]
            yd_ref[t, :, c0:c0 + LANE] = (
                p_ref[t, :, 6 * W + c0:6 * W + c0 + LANE] * mixed).astype(yd_ref.dtype)


SAMPLE_SEQ_BLOCK = 32


def _sample_mix(proj_s, nb, ts, W, sc_t, sp_t, cw, cb, clg, clb, pw, ps, qg, kg, cos, sa, sb,
                glg, glb, ws, gb):
    G = len(POOL_WINDOWS)
    sblk = SAMPLE_SEQ_BLOCK
    full2 = lambda shape: pl.BlockSpec(shape, lambda i: (0, 0))
    full3 = lambda shape: pl.BlockSpec(shape, lambda i: (0, 0, 0))
    seq3 = lambda rows, c: pl.BlockSpec((rows, sblk, c), lambda i: (0, i, 0))
    tok = lambda dt: jax.ShapeDtypeStruct((ts, nb, W), dt)
    return pl.pallas_call(
        _sample_mix_kernel,
        grid=(nb // sblk,),
        in_specs=[seq3(ts, 8 * W), seq3(CONV_W - 1, W), seq3(POOL_KEEP, W),
                  full2((CONV_W, W)), full2((1, W)), full2((1, W)), full2((1, W)),
                  full3((G, LANE, LANE)), full2((1, W)),
                  full2((1, LANE)), full2((1, LANE)),
                  full2((8, LANE)), full2((8, LANE)), full2((8, LANE)),
                  full2((1, W)), full2((1, W)),
                  full3((GMLP_GROUPS, GMLP_CHUNK, GMLP_CHUNK)), full2((GMLP_GROUPS, GMLP_CHUNK))],
        out_specs=[seq3(ts, W)] * 6 + [seq3(CONV_W - 1, W), seq3(POOL_KEEP, W), seq3(ts, W)],
        out_shape=[tok(BF16), tok(BF16), tok(BF16), tok(F32), tok(F32), tok(F32),
                   jax.ShapeDtypeStruct((CONV_W - 1, nb, W), F32),
                   jax.ShapeDtypeStruct((POOL_KEEP, nb, W), F32), tok(F32)],
        compiler_params=_cp(("arbitrary",), 40),
        name="sample_mix",
    )(proj_s, sc_t, sp_t, cw, cb, clg, clb, pw, ps, qg, kg, cos, sa, sb, glg, glb, ws, gb)


def _attn_s_kernel(pt_ref, q_ref, kn_ref, vn_ref, *refs, n_pages, ts):
    kp = refs[:n_pages]
    vp = refs[n_pages:2 * n_pages]
    o_ref = refs[2 * n_pages]
    kbuf, vbuf = refs[2 * n_pages + 1:]
    nh = N_HEADS
    prow = kp[0].shape[0]
    R, hd = q_ref.shape
    blk_rows = MOBA_BLOCK * nh
    ppb = blk_rows // prow
    nblk = n_pages // ppb
    scale = hd ** -0.5
    q = q_ref[...]
    rhead = lax.broadcasted_iota(jnp.int32, (R, 1), 0) >> 3
    t_row = lax.broadcasted_iota(jnp.int32, (R, 1), 0) & 7
    blkid = lax.broadcasted_iota(jnp.int32, (R, nblk), 1)
    bs = jnp.zeros((R, nblk), F32)
    for j in range(nblk):
        acc8 = None
        for u in range(ppb):
            pg = j * ppb + u
            kpg = kp[pg][...]
            kbuf[pg * prow:(pg + 1) * prow, :] = kpg.astype(BF16)
            vbuf[pg * prow:(pg + 1) * prow, :] = vp[pg][...].astype(BF16)
            part = jnp.sum(kpg.reshape(prow // 8, 8, hd), axis=0)
            acc8 = part if acc8 is None else acc8 + part
        kmean = (acc8[0:nh] + acc8[nh:2 * nh]) * (1.0 / MOBA_BLOCK)
        km_rows = jnp.concatenate([jnp.broadcast_to(kmean[h:h + 1], (8, hd)) for h in range(nh)], axis=0)
        bs = jnp.where(blkid == j, jnp.sum(q * km_rows, -1, keepdims=True), bs)
    sel = _topk_block_cols(bs, blkid >= 0)
    qb = q.astype(BF16)
    s_all = _dot_nt(qb, kbuf[...]) * scale
    own_head = (lax.broadcasted_iota(jnp.int32, (1, blk_rows), 1) & (nh - 1)) == rhead
    s_past = jnp.concatenate(
        [jnp.where(own_head & sel[j], s_all[:, j * blk_rows:(j + 1) * blk_rows], -jnp.inf)
         for j in range(nblk)], axis=-1)
    nc = kn_ref.shape[0]
    ocol = lax.broadcasted_iota(jnp.int32, (1, nc), 1)
    s_own = _dot_nt(qb, kn_ref[...].astype(BF16)) * scale
    s_own = jnp.where(((ocol & (nh - 1)) == rhead) & ((ocol >> 2) <= t_row), s_own, -jnp.inf)
    m = jnp.maximum(jnp.max(s_past, -1, keepdims=True), jnp.max(s_own, -1, keepdims=True))
    p_past = jnp.exp(s_past - m)
    p_own = jnp.exp(s_own - m)
    den = jnp.sum(p_past, -1, keepdims=True) + jnp.sum(p_own, -1, keepdims=True)
    acc = _dot(p_past.astype(BF16), vbuf[...]) + _dot(p_own.astype(BF16), vn_ref[...].astype(BF16))
    o_ref[...] = acc / den


def _attn_sample(pt_flat, q_hr, k_new, v_new, ck, cv, l, nb, ts, n_pages):
    prow, hd = ck.shape[2:]
    R = q_hr.shape[1]
    nc = k_new.shape[1]
    assert N_HEADS == 4 and nc % 8 == 0

    def page_spec(j):
        return pl.BlockSpec((None, None, prow, hd), lambda b, pt: (l, pt[b * n_pages + j], 0, 0))

    in_specs = ([pl.BlockSpec((None, R, hd), lambda b, pt: (b, 0, 0)),
                 pl.BlockSpec((None, nc, hd), lambda b, pt: (b, 0, 0)),
                 pl.BlockSpec((None, nc, hd), lambda b, pt: (b, 0, 0))]
                + [page_spec(j) for j in range(n_pages)] + [page_spec(j) for j in range(n_pages)])
    return pl.pallas_call(
        functools.partial(_attn_s_kernel, n_pages=n_pages, ts=ts),
        grid_spec=pltpu.PrefetchScalarGridSpec(
            num_scalar_prefetch=1, grid=(nb,), in_specs=in_specs,
            out_specs=pl.BlockSpec((None, R, hd), lambda b, pt: (b, 0, 0)),
            scratch_shapes=[pltpu.VMEM((n_pages * prow, hd), BF16), pltpu.VMEM((n_pages * prow, hd), BF16)]),
        out_shape=jax.ShapeDtypeStruct((nb, R, hd), F32),
        compiler_params=_cp(("arbitrary",), 48),
        name="attn_sample",
    )(pt_flat, q_hr, k_new, v_new, *([ck] * n_pages), *([cv] * n_pages))


def _merge_kernel(ya_ref, yb_ref, yc_ref, yd_ref, wa_ref, wb_ref, wc_ref, wd_ref,
                  ga_ref, gb_ref, gc_ref, gd_ref, o_ref, wbf_ref):
    w_refs = (wa_ref, wb_ref, wc_ref, wd_ref)

    @pl.when(pl.program_id(1) == 0)
    def _():
        for b in range(N_BRANCH):
            wbf_ref[b] = w_refs[b][...].astype(BF16)

    acc = None
    for b, (y_ref, g_ref) in enumerate(zip((ya_ref, yb_ref, yc_ref, yd_ref),
                                           (ga_ref, gb_ref, gc_ref, gd_ref))):
        term = _sigmoid(g_ref[...]) * _dot(y_ref[...], wbf_ref[b])
        acc = term if acc is None else acc + term
    o_ref[...] = acc.astype(o_ref.dtype)


def _merge(ys, ws, proj, l, W, D):
    M = proj.shape[0]
    tm, tn = 512, 1024
    npb = D // tn
    gate0 = 8 * W // tn
    y_spec = pl.BlockSpec((tm, W), lambda n, m: (m, 0))
    w_spec = pl.BlockSpec((None, W, tn), lambda n, m: (l, 0, n))
    g_specs = [pl.BlockSpec((tm, tn), functools.partial(lambda n, m, b: (m, gate0 + b * npb + n), b=b))
               for b in range(N_BRANCH)]
    return pl.pallas_call(
        _merge_kernel,
        grid=(npb, M // tm),
        in_specs=[y_spec] * 4 + [w_spec] * 4 + g_specs,
        out_specs=pl.BlockSpec((tm, tn), lambda n, m: (m, n)),
        out_shape=jax.ShapeDtypeStruct((M, D), BF16),
        scratch_shapes=[pltpu.VMEM((N_BRANCH, W, tn), BF16)],
        compiler_params=_cp(("arbitrary", "arbitrary"), 48),
        name="merge",
    )(*ys, *ws, proj, proj, proj, proj)


def _wo_kernel(a_ref, w_ref, x_ref, g_ref, o_ref, wbf_ref):
    @pl.when(pl.program_id(1) == 0)
    def _():
        wbf_ref[...] = w_ref[...].astype(BF16)

    o_ref[...] = x_ref[...] + g_ref[...] * _dot(a_ref[...], wbf_ref[...])


def _wo_residual(merged, w_o, l, x, mr, gate_chunk, seq_of_tile):
    M, D = x.shape
    tn = 1024
    ncb = D // tn
    return pl.pallas_call(
        _wo_kernel,
        grid=(ncb, M // TM),
        in_specs=[pl.BlockSpec((TM, D), lambda n, m: (m, 0)),
                  pl.BlockSpec((None, D, tn), lambda n, m: (l, 0, n)),
                  pl.BlockSpec((TM, tn), lambda n, m: (m, n)),
                  pl.BlockSpec((None, TM, tn), lambda n, m: (seq_of_tile(m), 0, gate_chunk * ncb + n))],
        out_specs=pl.BlockSpec((TM, tn), lambda n, m: (m, n)),
        out_shape=jax.ShapeDtypeStruct((M, D), F32),
        scratch_shapes=[pltpu.VMEM((D, tn), BF16)],
        compiler_params=_cp(("arbitrary", "arbitrary"), 40),
        name="wo_residual",
    )(merged, w_o, x, mr)


def _pack_bf16_pairs(x):
    n = x.shape[1] // 2
    lo = lax.bitcast_convert_type(x[:, :n].astype(BF16).astype(F32), jnp.uint32)
    hi = lax.bitcast_convert_type(x[:, n:].astype(BF16).astype(F32), jnp.uint32)
    return hi | (lo >> 16)


def _unpack_bf16_pairs(w):
    lo = lax.bitcast_convert_type(w << 16, F32).astype(BF16)
    hi = lax.bitcast_convert_type(w & jnp.uint32(0xFFFF0000), F32).astype(BF16)
    return lo, hi


def _norm_router_kernel(x_ref, g_ref, sc_ref, sh_ref, rw_ref, rb_ref,
                        h_ref, idx_ref, gate_ref, rank_ref, cnt_ref, run_ref):
    i = pl.program_id(0)
    E = rw_ref.shape[1]

    @pl.when(i == 0)
    def _():
        run_ref[...] = jnp.zeros(run_ref.shape, F32)

    x = x_ref[...]
    ms = jnp.mean(x * x, -1, keepdims=True)
    y = x * lax.rsqrt(ms + EPS) * g_ref[...]
    h = y * (1.0 + sc_ref[...]) + sh_ref[...]
    h_ref[...] = _pack_bf16_pairs(h)
    logits = jnp.dot(h, rw_ref[...], precision=lax.Precision.HIGHEST,
                     preferred_element_type=F32) + rb_ref[...]

    eidx = lax.broadcasted_iota(jnp.int32, (TM, E), 1).astype(F32)
    kcol = lax.broadcasted_iota(jnp.int32, (TM, TOP_K), 1)
    r_i = lax.broadcasted_iota(jnp.int32, (TM, TM), 0)
    c_i = lax.broadcasted_iota(jnp.int32, (TM, TM), 1)
    before = jnp.where(c_i < r_i, 1.0, 0.0).astype(BF16)
    work = logits
    vals, hots = [], []
    idx_out = jnp.zeros((TM, TOP_K), F32)
    rank_out = jnp.zeros((TM, TOP_K), F32)
    run = run_ref[...]
    for k in range(TOP_K):
        mx = jnp.max(work, -1, keepdims=True)
        am = jnp.min(jnp.where(work == mx, eidx, float(E)), -1, keepdims=True)
        hot = eidx == am
        work = jnp.where(hot, -jnp.inf, work)
        hot_f = jnp.where(hot, 1.0, 0.0)
        earlier = _dot(before, hot_f.astype(BF16))
        rank = jnp.sum(hot_f * (earlier + run), -1, keepdims=True)
        run = run + jnp.sum(hot_f, axis=0, keepdims=True)
        vals.append(mx)
        idx_out = jnp.where(kcol == k, am, idx_out)
        rank_out = jnp.where(kcol == k, rank, rank_out)
    run_ref[...] = run
    cnt_ref[...] = run
    den = jnp.zeros((TM, 1), F32)
    gate_out = jnp.zeros((TM, TOP_K), F32)
    ex = [jnp.exp(v - vals[0]) for v in vals]
    for e_ in ex:
        den = den + e_
    for k in range(TOP_K):
        gate_out = jnp.where(kcol == k, ex[k] / den, gate_out)
    idx_ref[...] = idx_out.astype(jnp.int32)
    rank_ref[...] = rank_out.astype(jnp.int32)
    gate_ref[...] = gate_out


def _norm_router(x, g, mr, sc_chunk, sh_chunk, seq_of_tile, rw, rb):
    M, D = x.shape
    E = rw.shape[1]
    tok = lambda c: pl.BlockSpec((TM, c), lambda i: (i, 0))
    return pl.pallas_call(
        _norm_router_kernel,
        grid=(M // TM,),
        in_specs=[tok(D),
                  pl.BlockSpec((1, D), lambda i: (0, 0)),
                  pl.BlockSpec((None, TM, D), lambda i: (seq_of_tile(i), 0, sc_chunk)),
                  pl.BlockSpec((None, TM, D), lambda i: (seq_of_tile(i), 0, sh_chunk)),
                  pl.BlockSpec((D, E), lambda i: (0, 0)),
                  pl.BlockSpec((1, E), lambda i: (0, 0))],
        out_specs=[tok(D // 2), tok(TOP_K), tok(TOP_K), tok(TOP_K), pl.BlockSpec((1, E), lambda i: (0, 0))],
        out_shape=[jax.ShapeDtypeStruct((M, D // 2), jnp.uint32), jax.ShapeDtypeStruct((M, TOP_K), jnp.int32),
                   jax.ShapeDtypeStruct((M, TOP_K), F32), jax.ShapeDtypeStruct((M, TOP_K), jnp.int32),
                   jax.ShapeDtypeStruct((1, E), F32)],
        scratch_shapes=[pltpu.VMEM((1, E), F32)],
        compiler_params=_cp(("arbitrary",), 40),
        name="norm_router",
    )(x, g, mr, mr, rw, rb)


def _row_copy(src_hbm, dst, src_row, dst_row, sem):
    return pltpu.make_async_copy(src_hbm.at[pl.ds(src_row, 1)], dst.at[pl.ds(dst_row, 1)], sem)


DISPATCH_TM = 512


def _dispatch_kernel(dest_ref, h_ref, xb_init_hbm, xb_hbm, sem):
    del xb_init_hbm
    i = pl.program_id(0)

    def issue(r, c):
        tok = i * DISPATCH_TM + r
        for k in range(TOP_K):
            _row_copy(h_ref, xb_hbm, r, dest_ref[tok * TOP_K + k], sem).start(priority=k % 2)
        return c

    lax.fori_loop(0, DISPATCH_TM, issue, 0)

    def drain(r, c):
        for k in range(TOP_K):
            _row_copy(h_ref, xb_hbm, 0, 0, sem).wait()
        return c

    lax.fori_loop(0, DISPATCH_TM, drain, 0)


def _dispatch(dest_flat, h2p, cap):
    M, Dp = h2p.shape
    any_spec = pl.BlockSpec(memory_space=pl.ANY)
    return pl.pallas_call(
        _dispatch_kernel,
        grid_spec=pltpu.PrefetchScalarGridSpec(
            num_scalar_prefetch=1, grid=(M // DISPATCH_TM,),
            in_specs=[pl.BlockSpec((DISPATCH_TM, Dp), lambda i, d: (i, 0)), any_spec], out_specs=any_spec,
            scratch_shapes=[pltpu.SemaphoreType.DMA(())]),
        out_shape=jax.ShapeDtypeStruct((cap, Dp), jnp.uint32),
        input_output_aliases={2: 0},
        compiler_params=_cp(("arbitrary",), 16),
        name="moe_dispatch",
    )(dest_flat, h2p, jnp.zeros((cap, Dp), jnp.uint32))


def _expert_weight_run(be_ref, nu_ref, nxt_ref, n_panels, copies, consume, slot_ref):
    n = pl.program_id(0)
    m = pl.program_id(1)
    e = be_ref[m]
    prev = be_ref[jnp.maximum(m - 1, 0)]

    @pl.when((m < nu_ref[0]) & ((m == 0) | (e != prev)))
    def _():
        @pl.when((n == 0) & (m == 0))
        def _():
            slot_ref[0] = 0
            for c in copies(0, e, n):
                c.start(priority=1)

        slot = slot_ref[0]
        for c in copies(slot, e, n):
            c.wait()
        consume(slot)
        ne = nxt_ref[m]

        @pl.when(ne >= 0)
        def _():
            for c in copies(1 - slot, ne, n):
                c.start(priority=1)

        @pl.when((ne < 0) & (n + 1 < n_panels))
        def _():
            for c in copies(1 - slot, be_ref[0], n + 1):
                c.start(priority=1)

        slot_ref[0] = 1 - slot


def _moe_up_kernel(be_ref, nu_ref, nxt_ref, x_ref, w_hbm, bg_ref, bl_ref, o_ref,
                   wbuf, wgb, wlb, slot_ref, sem, *, row0, tn, nf):
    m = pl.program_id(1)

    def copies(slot, e, n):
        return [pltpu.make_async_copy(w_hbm.at[row0 + e, :, pl.ds(pl.multiple_of((half * nf + n) * tn, tn), tn)],
                                      wbuf.at[slot, half], sem.at[slot, half]) for half in range(2)]

    def consume(slot):
        wgb[...] = wbuf[slot, 0].astype(BF16)
        wlb[...] = wbuf[slot, 1].astype(BF16)

    _expert_weight_run(be_ref, nu_ref, nxt_ref, nf, copies, consume, slot_ref)

    @pl.when(m < nu_ref[0])
    def _():
        x_lo, x_hi = _unpack_bf16_pairs(x_ref[...])
        n = x_lo.shape[1]
        hg = _dot(x_lo, wgb[0:n, :]) + _dot(x_hi, wgb[n:2 * n, :]) + bg_ref[...]
        hl = _dot(x_lo, wlb[0:n, :]) + _dot(x_hi, wlb[n:2 * n, :]) + bl_ref[...]
        hg = jnp.minimum(hg, SWIGLU_LIMIT)
        hl = jnp.clip(hl, -SWIGLU_LIMIT, SWIGLU_LIMIT)
        o_ref[...] = (hg * _sigmoid(SWIGLU_ALPHA * hg) * (hl + 1.0)).astype(o_ref.dtype)

    @pl.when(m >= nu_ref[0])
    def _():
        o_ref[...] = jnp.zeros(o_ref.shape, o_ref.dtype)


def _moe_up(blk_expert, n_used, nxt_expert, xb, w1, b1, l, E):
    cap, Dp = xb.shape
    D = w1.shape[1]
    F = w1.shape[2] // 2
    tn = 1024
    nf = F // tn
    n_blocks = cap // MOE_BM
    return pl.pallas_call(
        functools.partial(_moe_up_kernel, row0=l * E, tn=tn, nf=nf),
        grid_spec=pltpu.PrefetchScalarGridSpec(
            num_scalar_prefetch=3, grid=(nf, n_blocks),
            in_specs=[pl.BlockSpec((MOE_BM, Dp), lambda n, m, be, nu, nx: (m, 0)),
                      pl.BlockSpec(memory_space=pl.ANY),
                      pl.BlockSpec((None, 1, tn), lambda n, m, be, nu, nx: (l * E + be[m], 0, n)),
                      pl.BlockSpec((None, 1, tn), lambda n, m, be, nu, nx: (l * E + be[m], 0, nf + n))],
            out_specs=pl.BlockSpec((MOE_BM, tn), lambda n, m, be, nu, nx: (m, n)),
            scratch_shapes=[pltpu.VMEM((2, 2, D, tn), F32), pltpu.VMEM((D, tn), BF16), pltpu.VMEM((D, tn), BF16),
                            pltpu.SMEM((1,), jnp.int32), pltpu.SemaphoreType.DMA((2, 2))]),
        out_shape=jax.ShapeDtypeStruct((cap, F), BF16),
        compiler_params=_cp(("arbitrary", "arbitrary"), 56),
        name="moe_up",
    )(blk_expert, n_used, nxt_expert, xb, w1, b1, b1)


def _moe_down_kernel(be_ref, nu_ref, nxt_ref, a_ref, w_hbm, b_ref, o_ref, wbuf, wb, slot_ref, sem, *, row0):
    m = pl.program_id(1)

    def copies(slot, e, n):
        return [pltpu.make_async_copy(w_hbm.at[row0 + e], wbuf.at[slot], sem.at[slot])]

    def consume(slot):
        wb[...] = wbuf[slot].astype(BF16)

    _expert_weight_run(be_ref, nu_ref, nxt_ref, 1, copies, consume, slot_ref)

    @pl.when(m < nu_ref[0])
    def _():
        o_ref[...] = _dot(a_ref[...], wb[...]) + b_ref[...]

    @pl.when(m >= nu_ref[0])
    def _():
        o_ref[...] = jnp.zeros(o_ref.shape, o_ref.dtype)


def _moe_down(blk_expert, n_used, nxt_expert, act, w2, b2, l, E):
    cap, F = act.shape
    D = w2.shape[2]
    n_blocks = cap // MOE_BM
    return pl.pallas_call(
        functools.partial(_moe_down_kernel, row0=l * E),
        grid_spec=pltpu.PrefetchScalarGridSpec(
            num_scalar_prefetch=3, grid=(1, n_blocks),
            in_specs=[pl.BlockSpec((MOE_BM, F), lambda n, m, be, nu, nx: (m, 0)),
                      pl.BlockSpec(memory_space=pl.ANY),
                      pl.BlockSpec((None, 1, D), lambda n, m, be, nu, nx: (l * E + be[m], 0, 0))],
            out_specs=pl.BlockSpec((MOE_BM, D), lambda n, m, be, nu, nx: (m, 0)),
            scratch_shapes=[pltpu.VMEM((2, F, D), F32), pltpu.VMEM((F, D), BF16),
                            pltpu.SMEM((1,), jnp.int32), pltpu.SemaphoreType.DMA((2,))]),
        out_shape=jax.ShapeDtypeStruct((cap, D), F32),
        compiler_params=_cp(("arbitrary", "arbitrary"), 56),
        name="moe_down",
    )(blk_expert, n_used, nxt_expert, act, w2, b2)


COMBINE_TM = 128


def _combine_kernel(slot_ref, y_hbm, x_ref, g_ref, gate_ref, o_ref, buf, sem):
    i = pl.program_id(0)

    def gather(tile, s, start):
        def body(r, c):
            for k in range(TOP_K):
                row = slot_ref[(tile * COMBINE_TM + r) * TOP_K + k] if start else 0
                cp = _row_copy(y_hbm, buf.at[s, k], row, r, sem.at[s])
                cp.start(priority=k % 2) if start else cp.wait()
            return c

        lax.fori_loop(0, COMBINE_TM, body, 0)

    @pl.when(i == 0)
    def _():
        gather(0, 0, True)

    @pl.when(i + 1 < pl.num_programs(0))
    def _():
        gather(i + 1, (i + 1) & 1, True)

    s = i & 1
    gather(i, s, False)
    y = gate_ref[:, 0:1] * buf[s, 0]
    for k in range(1, TOP_K):
        y = y + gate_ref[:, k:k + 1] * buf[s, k]
    o_ref[...] = x_ref[...] + g_ref[...] * y


def _combine(slot_of, yb, x, mr, gate_chunk, seq_of_tile, gate):
    M, D = x.shape
    per = TM // COMBINE_TM
    return pl.pallas_call(
        _combine_kernel,
        grid_spec=pltpu.PrefetchScalarGridSpec(
            num_scalar_prefetch=1, grid=(M // COMBINE_TM,),
            in_specs=[pl.BlockSpec(memory_space=pl.ANY),
                      pl.BlockSpec((COMBINE_TM, D), lambda i, s: (i, 0)),
                      pl.BlockSpec((None, COMBINE_TM, D), lambda i, s: (seq_of_tile(i // per), 0, gate_chunk)),
                      pl.BlockSpec((COMBINE_TM, TOP_K), lambda i, s: (i, 0))],
            out_specs=pl.BlockSpec((COMBINE_TM, D), lambda i, s: (i, 0)),
            scratch_shapes=[pltpu.VMEM((2, TOP_K, COMBINE_TM, D), F32), pltpu.SemaphoreType.DMA((2,))]),
        out_shape=jax.ShapeDtypeStruct((M, D), F32),
        compiler_params=_cp(("arbitrary",), 32),
        name="moe_combine",
    )(slot_of, yb, x, mr, gate)


def _slot_tables(top_idx, rank, counts, n_blocks):
    E = counts.shape[0]
    padded = ((counts + MOE_BM - 1) // MOE_BM) * MOE_BM
    pad_end = jnp.cumsum(padded)
    pad_start = pad_end - padded
    onehot = top_idx[..., None] == jnp.arange(E, dtype=jnp.int32)
    dest = jnp.sum(jnp.where(onehot, pad_start, 0), -1) + rank
    blk_start = jnp.arange(n_blocks, dtype=jnp.int32) * MOE_BM
    blk_expert = jnp.minimum(jnp.sum(blk_start[:, None] >= pad_end[None, :], -1), E - 1).astype(jnp.int32)
    n_used = (pad_end[-1] // MOE_BM).astype(jnp.int32).reshape(1)
    ids = jnp.arange(E, dtype=jnp.int32)
    later = (ids[None, :] > ids[:, None]) & (counts[None, :] > 0)
    nxt_of_expert = jnp.min(jnp.where(later, ids[None, :], E), axis=1)
    nxt_of_expert = jnp.where(nxt_of_expert == E, -1, nxt_of_expert)
    nxt_expert = jnp.sum(jnp.where(blk_expert[:, None] == ids[None, :], nxt_of_expert[None, :], 0), -1)
    return dest.reshape(-1).astype(jnp.int32), blk_expert, n_used, nxt_expert.astype(jnp.int32)


def _rope_tables(pos, hd):
    rot = hd // 4
    half = rot // 2
    inv = jnp.power(jnp.float32(ROPE_THETA), -jnp.arange(half, dtype=F32) * 2.0 / rot)
    ang = pos.astype(F32)[:, None] * inv[None, :]
    cos, sin = jnp.cos(ang), jnp.sin(ang)
    n = pos.shape[0]
    c = jnp.concatenate([cos, cos, jnp.ones((n, hd - rot), F32)], 1)
    sa = jnp.concatenate([-sin, jnp.zeros((n, hd - half), F32)], 1)
    sb = jnp.concatenate([jnp.zeros((n, half), F32), sin, jnp.zeros((n, hd - rot), F32)], 1)
    return c, sa, sb


def kernel(x_prompt, x_sample, cache_k, cache_v, state_conv, state_pool, page_table, c_prompt, c_sample,
           w_ada, b_ada, norm1_g, norm2_g, w_in, conv_w, conv_b, conv_ln_g, conv_ln_b, conv_out,
           pool_w, pool_scale, pool_out, q_norm_g, k_norm_g, attn_out, gmlp_ln_g, gmlp_ln_b,
           gmlp_ws, gmlp_b, gmlp_out, w_o, router_w, router_b, exp_w1, exp_b1, exp_w2, exp_b2):
    B, T, D = x_prompt.shape
    NB, TS, _ = x_sample.shape
    L = w_ada.shape[0]
    W = D // N_BRANCH
    HD = W // N_HEADS
    E = router_w.shape[2]
    n_pool, PAGE = cache_k.shape[1], cache_k.shape[2]
    n_pages = page_table.shape[1]
    past = n_pages * PAGE
    M_p, M_s = B * T, NB * TS
    M = M_p + M_s
    assert HD == LANE and W == N_HEADS * LANE and T % TM == 0 and TM % NB == 0 and NB % 8 == 0
    assert M % DISPATCH_TM == 0 and M_s % TM == 0 and TS <= 8 and past % MOBA_BLOCK == 0
    tiles_per_seq = T // TM

    def seq_of_tile(i):
        return jnp.minimum(i // tiles_per_seq, B)

    x = jnp.concatenate([x_prompt.reshape(M_p, D), x_sample.transpose(1, 0, 2).reshape(M_s, D)], 0)
    n_c = B + NB
    c_all = jnp.concatenate([c_prompt, c_sample, jnp.zeros((-n_c % 8, D), F32)], 0)
    mod_all = _ada(c_all, w_ada, b_ada)

    cos_p, sa_p, sb_p = _rope_tables(jnp.arange(T), HD)
    cos_s, sa_s, sb_s = [jnp.pad(t, ((0, 8 - TS), (0, 0))) for t in _rope_tables(past + jnp.arange(TS), HD)]
    pt_flat = page_table.reshape(-1).astype(jnp.int32)
    ck = cache_k.reshape(L, n_pool, PAGE * N_HEADS, HD)
    cv = cache_v.reshape(L, n_pool, PAGE * N_HEADS, HD)
    w1 = exp_w1.reshape(L * E, D, exp_w1.shape[3])
    b1 = exp_b1.reshape(L * E, 1, exp_b1.shape[2])
    w2 = exp_w2.reshape(L * E, exp_w2.shape[2], D)
    b2 = exp_b2.reshape(L * E, 1, D)

    def to_bt(a_t):
        return a_t.reshape(TS, NB, -1).transpose(1, 0, 2)

    outs = {k: [] for k in ('kp', 'vp', 'ks', 'vs', 'cp', 'cs', 'pp', 'ps', 'gp', 'gs')}
    for l in range(L):
        mod = mod_all[l]
        mr = jnp.concatenate([jnp.broadcast_to(mod[:B, None, :], (B, TM, 6 * D)),
                              jnp.tile(mod[B:B + NB], (TM // NB, 1))[None]], 0)
        h = _norm_mod(x, norm1_g[l][None], mr, 1, 0, seq_of_tile, BF16)
        tm_proj = next(t for t in (1088, 1024, 512, 256) if M % t == 0)
        proj = _panel_mm(h, w_in, l, tm_proj, 1024)

        row = lambda v: v[l][None]
        ya_p, conv_p = _conv_prompt(proj, B, T, W, conv_w[l], row(conv_b), row(conv_ln_g), row(conv_ln_b))
        yb_p, pool_p = _pool_prompt(proj, B, T, W, pool_w[l], row(pool_scale))
        q_p, k_p, v_p, kb_p, vb_p, kmean_p = _qkv_prompt(proj, B, T, W, row(q_norm_g), row(k_norm_g),
                                                        cos_p, sa_p, sb_p)
        yc_p = _attn_prompt(q_p, kb_p, vb_p, kmean_p, B, T, W)
        yd_p, gv_p = _gmlp_prompt(proj, B, T, W, row(gmlp_ln_g), row(gmlp_ln_b), gmlp_ws[l],
                                  jnp.transpose(gmlp_b[l]))

        proj_s = proj[M_p:, :8 * W].reshape(TS, NB, 8 * W)
        (ya_s, yb_s, yd_s, q_s, k_s, v_s, conv_s, pool_s, gv_s) = _sample_mix(
            proj_s, NB, TS, W, state_conv[l].transpose(1, 0, 2), state_pool[l].transpose(1, 0, 2),
            conv_w[l], row(conv_b), row(conv_ln_g), row(conv_ln_b), pool_w[l], row(pool_scale),
            row(q_norm_g), row(k_norm_g), cos_s, sa_s, sb_s, row(gmlp_ln_g), row(gmlp_ln_b),
            gmlp_ws[l], gmlp_b[l])
        q_hr = jnp.pad(to_bt(q_s).reshape(NB, TS, N_HEADS, HD).transpose(0, 2, 1, 3),
                       ((0, 0), (0, 0), (0, 8 - TS), (0, 0))).reshape(NB, N_HEADS * 8, HD)
        k_new = to_bt(k_s).reshape(NB, TS * N_HEADS, HD)
        v_new = to_bt(v_s).reshape(NB, TS * N_HEADS, HD)
        att_s = _attn_sample(pt_flat, q_hr, k_new, v_new, ck, cv, l, NB, TS, n_pages)
        yc_s = (att_s.reshape(NB, N_HEADS, 8, HD)[:, :, :TS].transpose(2, 0, 1, 3)
                .reshape(M_s, W).astype(BF16))

        ys = [jnp.concatenate([p_, s_.reshape(M_s, W)], 0) for p_, s_ in
              ((ya_p, ya_s), (yb_p, yb_s), (yc_p, yc_s), (yd_p, yd_s))]
        merged = _merge(ys, (conv_out, pool_out, attn_out, gmlp_out), proj, l, W, D)
        x = _wo_residual(merged, w_o, l, x, mr, 2, seq_of_tile)
        h2p, top_idx, gate, rank, counts = _norm_router(x, norm2_g[l][None], mr, 4, 3, seq_of_tile,
                                                         router_w[l], router_b[l][None])
        n_blocks = -(-M * TOP_K // MOE_BM) + E
        dest, blk_expert, n_used, nxt_expert = _slot_tables(top_idx, rank, counts[0].astype(jnp.int32), n_blocks)
        xb = _dispatch(dest, h2p, n_blocks * MOE_BM)
        act = _moe_up(blk_expert, n_used, nxt_expert, xb, w1, b1, l, E)
        yb = _moe_down(blk_expert, n_used, nxt_expert, act, w2, b2, l, E)
        x = _combine(dest, yb, x, mr, 5, seq_of_tile, gate)

        outs['kp'].append(k_p.reshape(B, T, N_HEADS, HD))
        outs['vp'].append(v_p.reshape(B, T, N_HEADS, HD))
        outs['ks'].append(to_bt(k_s).reshape(NB, TS, N_HEADS, HD))
        outs['vs'].append(to_bt(v_s).reshape(NB, TS, N_HEADS, HD))
        outs['cp'].append(conv_p)
        outs['cs'].append(conv_s.transpose(1, 0, 2))
        outs['pp'].append(pool_p)
        outs['ps'].append(pool_s.transpose(1, 0, 2))
        outs['gp'].append(gv_p)
        outs['gs'].append(to_bt(gv_s))

    y_prompt = x[:M_p].reshape(B, T, D)
    y_sample = to_bt(x[M_p:])
    st = lambda k: jnp.stack(outs[k])
    return (y_prompt, y_sample, st('kp'), st('vp'), st('ks'), st('vs'), st('cp'), st('cs'),
            st('pp'), st('ps'), st('gp'), st('gs'))
```

```python
import functools

import jax
import jax.numpy as jnp
from jax import lax
from jax.experimental import pallas as pl
from jax.experimental.pallas import tpu as pltpu

F32 = jnp.float32
BF16 = jnp.bfloat16

N_BRANCH = 4
CONV_W = 31
POOL_WINDOWS = (2, 4, 8, 16)
POOL_KEEP = max(POOL_WINDOWS) - 1
N_HEADS = 4
ROPE_THETA = 500000.0
MOBA_BLOCK = 256
MOBA_TOPK = 3
GMLP_CHUNK = 128
GMLP_GROUPS = 4
TOP_K = 4
SWIGLU_LIMIT = 7.0
SWIGLU_ALPHA = 1.702
EPS = 1e-6

LANE = 128
TM = 256
CONV_HIST = 32
POOL_HIST = 16
MOE_BM = 256
MIB = 1024 * 1024


def _cp(sem, vmem_mib):
    return pltpu.CompilerParams(dimension_semantics=sem, vmem_limit_bytes=vmem_mib * MIB)


def _sigmoid(x):
    return 1.0 / (1.0 + jnp.exp(-x))


def _layer_norm(x, g, b):
    mu = jnp.mean(x, -1, keepdims=True)
    xc = x - mu
    var = jnp.mean(xc * xc, -1, keepdims=True)
    return xc * lax.rsqrt(var + EPS) * g + b


def _dot(a, b):
    return jnp.dot(a, b, preferred_element_type=F32)


def _dot_nt(a, b, precision=None):
    return lax.dot_general(a, b, (((1,), (1,)), ((), ())), precision=precision,
                           preferred_element_type=F32)


def _ada_kernel(c_ref, w_ref, b_ref, o_ref):
    c = c_ref[...]
    s = (c * _sigmoid(c)).astype(BF16)
    o_ref[...] = _dot(s, w_ref[...].astype(BF16)) + b_ref[...]


def _ada(c_all, w_ada, b_ada):
    L, D, N = w_ada.shape
    R = c_all.shape[0]
    tn = 1024
    return pl.pallas_call(
        _ada_kernel,
        grid=(L, N // tn),
        in_specs=[pl.BlockSpec((R, D), lambda l, n: (0, 0)),
                  pl.BlockSpec((None, D, tn), lambda l, n: (l, 0, n)),
                  pl.BlockSpec((None, 1, tn), lambda l, n: (l, 0, n))],
        out_specs=pl.BlockSpec((None, R, tn), lambda l, n: (l, 0, n)),
        out_shape=jax.ShapeDtypeStruct((L, R, N), F32),
        compiler_params=_cp(("arbitrary", "arbitrary"), 40),
        name="ada",
    )(c_all, w_ada, b_ada.reshape(L, 1, N))


def _norm_mod_kernel(x_ref, g_ref, sc_ref, sh_ref, o_ref):
    x = x_ref[...]
    ms = jnp.mean(x * x, -1, keepdims=True)
    y = x * lax.rsqrt(ms + EPS) * g_ref[...]
    o_ref[...] = (y * (1.0 + sc_ref[...]) + sh_ref[...]).astype(o_ref.dtype)


def _norm_mod(x, g, mr, sc_chunk, sh_chunk, seq_of_tile, out_dtype):
    M, D = x.shape
    return pl.pallas_call(
        _norm_mod_kernel,
        grid=(M // TM,),
        in_specs=[pl.BlockSpec((TM, D), lambda i: (i, 0)),
                  pl.BlockSpec((1, D), lambda i: (0, 0)),
                  pl.BlockSpec((None, TM, D), lambda i: (seq_of_tile(i), 0, sc_chunk)),
                  pl.BlockSpec((None, TM, D), lambda i: (seq_of_tile(i), 0, sh_chunk))],
        out_specs=pl.BlockSpec((TM, D), lambda i: (i, 0)),
        out_shape=jax.ShapeDtypeStruct((M, D), out_dtype),
        compiler_params=_cp(("arbitrary",), 40),
        name="norm_mod",
    )(x, g, mr, mr)


def _panel_mm_kernel(a_ref, w_ref, o_ref, wbf_ref):
    @pl.when(pl.program_id(1) == 0)
    def _():
        wbf_ref[...] = w_ref[...].astype(BF16)

    o_ref[...] = _dot(a_ref[...], wbf_ref[...])


def _panel_mm(a, w, l, tm, tn):
    M, K = a.shape
    N = w.shape[2]
    return pl.pallas_call(
        _panel_mm_kernel,
        grid=(N // tn, M // tm),
        in_specs=[pl.BlockSpec((tm, K), lambda n, m: (m, 0)),
                  pl.BlockSpec((None, K, tn), lambda n, m: (l, 0, n))],
        out_specs=pl.BlockSpec((tm, tn), lambda n, m: (m, n)),
        out_shape=jax.ShapeDtypeStruct((M, N), F32),
        scratch_shapes=[pltpu.VMEM((K, tn), BF16)],
        compiler_params=_cp(("arbitrary", "arbitrary"), 48),
        name="in_proj",
    )(a, w)


def _conv_p_kernel(a_ref, g_ref, w_ref, cb_ref, lg_ref, lb_ref, y_ref, cn_ref, zbuf, ybuf):
    i = pl.program_id(1)
    W = a_ref.shape[1]

    @pl.when(i == 0)
    def _():
        zbuf[0:CONV_HIST, :] = jnp.zeros((CONV_HIST, W), F32)

    @pl.when(i > 0)
    def _():
        zbuf[0:CONV_HIST, :] = zbuf[TM:TM + CONV_HIST, :]

    zbuf[CONV_HIST:CONV_HIST + TM, :] = a_ref[...] * _sigmoid(g_ref[...])
    base = CONV_HIST - (CONV_W - 1)
    rb = 128
    for c0 in range(0, W, LANE):
        for r0 in range(0, TM, rb):
            acc = jnp.broadcast_to(cb_ref[:, c0:c0 + LANE], (rb, LANE))
            for j in range(CONV_W):
                acc = acc + w_ref[j:j + 1, c0:c0 + LANE] * zbuf[pl.ds(base + r0 + j, rb), c0:c0 + LANE]
            ybuf[r0:r0 + rb, c0:c0 + LANE] = acc
    y = _layer_norm(ybuf[...], lg_ref[...], lb_ref[...])
    y_ref[...] = (y * _sigmoid(y)).astype(y_ref.dtype)

    @pl.when(i == pl.num_programs(1) - 1)
    def _():
        cn_ref[...] = zbuf[CONV_HIST + TM - (CONV_W - 1):CONV_HIST + TM, :]


def _conv_prompt(proj, B, T, W, cw, cb, lg, lb):
    nt = T // TM
    return pl.pallas_call(
        _conv_p_kernel,
        grid=(B, nt),
        in_specs=[pl.BlockSpec((TM, W), lambda b, i: (b * nt + i, 0)),
                  pl.BlockSpec((TM, W), lambda b, i: (b * nt + i, 1)),
                  pl.BlockSpec((CONV_W, W), lambda b, i: (0, 0)),
                  pl.BlockSpec((1, W), lambda b, i: (0, 0)),
                  pl.BlockSpec((1, W), lambda b, i: (0, 0)),
                  pl.BlockSpec((1, W), lambda b, i: (0, 0))],
        out_specs=[pl.BlockSpec((TM, W), lambda b, i: (b * nt + i, 0)),
                   pl.BlockSpec((None, CONV_W - 1, W), lambda b, i: (b, 0, 0))],
        out_shape=[jax.ShapeDtypeStruct((B * T, W), BF16),
                   jax.ShapeDtypeStruct((B, CONV_W - 1, W), F32)],
        scratch_shapes=[pltpu.VMEM((CONV_HIST + TM, W), F32), pltpu.VMEM((TM, W), F32)],
        compiler_params=_cp(("arbitrary", "arbitrary"), 32),
        name="conv_prompt",
    )(proj, proj, cw, cb, lg, lb)


def _pool_p_kernel(p_ref, pw_ref, ps_ref, y_ref, pn_ref, ebuf):
    i = pl.program_id(1)
    W = p_ref.shape[1]

    @pl.when(i == 0)
    def _():
        ebuf[0:POOL_HIST, :] = jnp.zeros((POOL_HIST, W), F32)

    @pl.when(i > 0)
    def _():
        ebuf[0:POOL_HIST, :] = ebuf[TM:TM + POOL_HIST, :]

    ebuf[POOL_HIST:POOL_HIST + TM, :] = p_ref[...]
    t_abs = i * TM + lax.broadcasted_iota(jnp.int32, (TM, 1), 0)
    for gi, win in enumerate(POOL_WINDOWS):
        c0 = gi * LANE
        s = ebuf[POOL_HIST:POOL_HIST + TM, c0:c0 + LANE]
        for k in range(1, win):
            s = s + ebuf[pl.ds(POOL_HIST - k, TM), c0:c0 + LANE]
        cnt = jnp.minimum(t_abs + 1, win).astype(F32)
        d = s / cnt - p_ref[:, c0:c0 + LANE]
        yg = _dot(d.astype(BF16), pw_ref[gi].astype(BF16)) * ps_ref[:, c0:c0 + LANE]
        y_ref[:, c0:c0 + LANE] = yg.astype(y_ref.dtype)

    @pl.when(i == pl.num_programs(1) - 1)
    def _():
        pn_ref[...] = ebuf[POOL_HIST + TM - POOL_KEEP:POOL_HIST + TM, :]


def _pool_prompt(proj, B, T, W, pw, ps):
    nt = T // TM
    G = len(POOL_WINDOWS)
    return pl.pallas_call(
        _pool_p_kernel,
        grid=(B, nt),
        in_specs=[pl.BlockSpec((TM, W), lambda b, i: (b * nt + i, 2)),
                  pl.BlockSpec((G, LANE, LANE), lambda b, i: (0, 0, 0)),
                  pl.BlockSpec((1, W), lambda b, i: (0, 0))],
        out_specs=[pl.BlockSpec((TM, W), lambda b, i: (b * nt + i, 0)),
                   pl.BlockSpec((None, POOL_KEEP, W), lambda b, i: (b, 0, 0))],
        out_shape=[jax.ShapeDtypeStruct((B * T, W), BF16),
                   jax.ShapeDtypeStruct((B, POOL_KEEP, W), F32)],
        scratch_shapes=[pltpu.VMEM((POOL_HIST + TM, W), F32)],
        compiler_params=_cp(("arbitrary", "arbitrary"), 32),
        name="pool_prompt",
    )(proj, pw, ps)


def _norm_rope_head(xh, g, cos, sa, sb):
    ms = jnp.mean(xh * xh, -1, keepdims=True)
    xn = xh * lax.rsqrt(ms + EPS) * g
    hd = xh.shape[1]
    half = hd // 8
    up = pltpu.roll(xn, hd - half, axis=1)
    dn = pltpu.roll(xn, half, axis=1)
    return xn * cos + up * sa + dn * sb


def _qkv_p_kernel(q_ref, k_ref, v_ref, qg_ref, kg_ref, cos_ref, sa_ref, sb_ref,
                  qo_ref, ko_ref, vo_ref, kb_ref, vb_ref, km_ref):
    i = pl.program_id(1)
    cos, sa, sb = cos_ref[...], sa_ref[...], sb_ref[...]

    @pl.when(i == 0)
    def _():
        km_ref[...] = jnp.zeros(km_ref.shape, F32)

    blk_row = lax.broadcasted_iota(jnp.int32, (km_ref.shape[0], LANE), 0)
    for h in range(N_HEADS):
        c0 = h * LANE
        qo_ref[:, c0:c0 + LANE] = _norm_rope_head(q_ref[:, c0:c0 + LANE], qg_ref[...], cos, sa, sb)
        kh = _norm_rope_head(k_ref[:, c0:c0 + LANE], kg_ref[...], cos, sa, sb)
        ko_ref[:, c0:c0 + LANE] = kh
        kb_ref[:, c0:c0 + LANE] = kh.astype(BF16)
        km_ref[:, c0:c0 + LANE] = jnp.where(blk_row == i, jnp.mean(kh, axis=0, keepdims=True),
                                            km_ref[:, c0:c0 + LANE])
    v = v_ref[...]
    vo_ref[...] = v
    vb_ref[...] = v.astype(BF16)


def _qkv_prompt(proj, B, T, W, qg, kg, cos, sa, sb):
    nt = T // TM
    row = lambda b, i: (b * nt + i, 0)
    return pl.pallas_call(
        _qkv_p_kernel,
        grid=(B, nt),
        in_specs=[pl.BlockSpec((TM, W), lambda b, i: (b * nt + i, 3)),
                  pl.BlockSpec((TM, W), lambda b, i: (b * nt + i, 4)),
                  pl.BlockSpec((TM, W), lambda b, i: (b * nt + i, 5)),
                  pl.BlockSpec((1, LANE), lambda b, i: (0, 0)),
                  pl.BlockSpec((1, LANE), lambda b, i: (0, 0)),
                  pl.BlockSpec((TM, LANE), lambda b, i: (i, 0)),
                  pl.BlockSpec((TM, LANE), lambda b, i: (i, 0)),
                  pl.BlockSpec((TM, LANE), lambda b, i: (i, 0))],
        out_specs=[pl.BlockSpec((TM, W), row), pl.BlockSpec((TM, W), row), pl.BlockSpec((TM, W), row),
                   pl.BlockSpec((TM, W), row), pl.BlockSpec((TM, W), row),
                   pl.BlockSpec((nt, W), lambda b, i: (b, 0))],
        out_shape=[jax.ShapeDtypeStruct((B * T, W), F32), jax.ShapeDtypeStruct((B * T, W), F32),
                   jax.ShapeDtypeStruct((B * T, W), F32), jax.ShapeDtypeStruct((B * T, W), BF16),
                   jax.ShapeDtypeStruct((B * T, W), BF16), jax.ShapeDtypeStruct((B * nt, W), F32)],
        compiler_params=_cp(("arbitrary", "arbitrary"), 32),
        name="qkv_prompt",
    )(proj, proj, proj, qg, kg, cos, sa, sb)


def _topk_block_cols(bs, past):
    nb = bs.shape[1]
    jidx = lax.broadcasted_iota(jnp.int32, bs.shape, 1)
    cols = []
    for n in range(nb):
        bn = bs[:, n:n + 1]
        beats = ((bs > bn) | ((bs == bn) & (jidx < n))) & past
        rank = jnp.sum(beats.astype(F32), axis=-1, keepdims=True)
        cols.append(rank < (MOBA_TOPK - 0.5))
    return cols


def _attn_p_kernel(q_ref, k_ref, v_ref, km_ref, o_ref):
    i = pl.program_id(1)
    nb = km_ref.shape[0]
    scale = LANE ** -0.5
    row = lax.broadcasted_iota(jnp.int32, (TM, MOBA_BLOCK), 0)
    col = lax.broadcasted_iota(jnp.int32, (TM, MOBA_BLOCK), 1)
    tri = col <= row
    blk = lax.broadcasted_iota(jnp.int32, (TM, nb), 1)

    def attend(own):
        nk = (own + 1) * MOBA_BLOCK
        for h in range(N_HEADS):
            hs = slice(h * LANE, (h + 1) * LANE)
            q = q_ref[:, hs]
            s = _dot_nt(q.astype(BF16), k_ref[0:nk, hs]) * scale
            pieces = []
            if own > 0:
                bs = _dot_nt(q, km_ref[:, hs], precision=lax.Precision.HIGHEST)
                sel = _topk_block_cols(bs, blk < own)
                pieces = [jnp.where(sel[j], s[:, j * MOBA_BLOCK:(j + 1) * MOBA_BLOCK], -jnp.inf)
                          for j in range(own)]
            pieces.append(jnp.where(tri, s[:, own * MOBA_BLOCK:nk], -jnp.inf))
            s = jnp.concatenate(pieces, axis=-1) if own > 0 else pieces[0]
            m = jnp.max(s, -1, keepdims=True)
            p = jnp.exp(s - m)
            den = jnp.sum(p, -1, keepdims=True)
            o_ref[:, hs] = (_dot(p.astype(BF16), v_ref[0:nk, hs]) / den).astype(o_ref.dtype)

    for own in range(nb):
        pl.when(i == own)(functools.partial(attend, own))


def _attn_prompt(q, kb, vb, kmean, B, T, W):
    nt = T // TM
    assert TM == MOBA_BLOCK
    return pl.pallas_call(
        _attn_p_kernel,
        grid=(B, nt),
        in_specs=[pl.BlockSpec((TM, W), lambda b, i: (b * nt + i, 0)),
                  pl.BlockSpec((T, W), lambda b, i: (b, 0)),
                  pl.BlockSpec((T, W), lambda b, i: (b, 0)),
                  pl.BlockSpec((nt, W), lambda b, i: (b, 0))],
        out_specs=pl.BlockSpec((TM, W), lambda b, i: (b * nt + i, 0)),
        out_shape=jax.ShapeDtypeStruct((B * T, W), BF16),
        compiler_params=_cp(("arbitrary", "arbitrary"), 40),
        name="attn_prompt",
    )(q, kb, vb, kmean)


def _gmlp_p_kernel(u_ref, v_ref, lg_ref, lb_ref, ws_ref, bt_ref, y_ref, gv_ref, vbuf):
    i = pl.program_id(1)
    vbuf[...] = _layer_norm(v_ref[...], lg_ref[...], lb_ref[...])
    row = lax.broadcasted_iota(jnp.int32, (GMLP_CHUNK, GMLP_CHUNK), 0)
    col = lax.broadcasted_iota(jnp.int32, (GMLP_CHUNK, GMLP_CHUNK), 1)
    tri = col <= row
    for g in range(GMLP_GROUPS):
        c0 = g * LANE
        w = jnp.where(tri, ws_ref[g], 0.0).astype(BF16)
        for r0 in range(0, TM, GMLP_CHUNK):
            mixed = _dot(w, vbuf[r0:r0 + GMLP_CHUNK, c0:c0 + LANE].astype(BF16)) + bt_ref[:, g:g + 1]
            y_ref[r0:r0 + GMLP_CHUNK, c0:c0 + LANE] = (
                u_ref[r0:r0 + GMLP_CHUNK, c0:c0 + LANE] * mixed).astype(y_ref.dtype)

    @pl.when(i == pl.num_programs(1) - 1)
    def _():
        gv_ref[...] = vbuf[TM - GMLP_CHUNK:TM, :]


def _gmlp_prompt(proj, B, T, W, lg, lb, ws, bt):
    nt = T // TM
    return pl.pallas_call(
        _gmlp_p_kernel,
        grid=(B, nt),
        in_specs=[pl.BlockSpec((TM, W), lambda b, i: (b * nt + i, 6)),
                  pl.BlockSpec((TM, W), lambda b, i: (b * nt + i, 7)),
                  pl.BlockSpec((1, W), lambda b, i: (0, 0)),
                  pl.BlockSpec((1, W), lambda b, i: (0, 0)),
                  pl.BlockSpec((GMLP_GROUPS, GMLP_CHUNK, GMLP_CHUNK), lambda b, i: (0, 0, 0)),
                  pl.BlockSpec((GMLP_CHUNK, GMLP_GROUPS), lambda b, i: (0, 0))],
        out_specs=[pl.BlockSpec((TM, W), lambda b, i: (b * nt + i, 0)),
                   pl.BlockSpec((None, GMLP_CHUNK, W), lambda b, i: (b, 0, 0))],
        out_shape=[jax.ShapeDtypeStruct((B * T, W), BF16),
                   jax.ShapeDtypeStruct((B, GMLP_CHUNK, W), F32)],
        scratch_shapes=[pltpu.VMEM((TM, W), F32)],
        compiler_params=_cp(("arbitrary", "arbitrary"), 32),
        name="gmlp_prompt",
    )(proj, proj, lg, lb, ws, bt)


def _sample_mix_kernel(p_ref, sc_ref, sp_ref, cw_ref, cb_ref, clg_ref, clb_ref, pw_ref, ps_ref,
                       qg_ref, kg_ref, cos_ref, sa_ref, sb_ref, glg_ref, glb_ref, ws_ref, gb_ref,
                       ya_ref, yb_ref, yd_ref, q_ref, k_ref, v_ref, cn_ref, pn_ref, gv_ref):
    ts, sb_rows, _ = p_ref.shape
    W = ya_ref.shape[2]
    n_conv = CONV_W - 1

    def col(c):
        return slice(c * W, (c + 1) * W)

    z = [p_ref[t, :, col(0)] * _sigmoid(p_ref[t, :, col(1)]) for t in range(ts)]

    def zext(r):
        return sc_ref[r] if r < n_conv else z[r - n_conv]

    for t in range(ts):
        acc = jnp.broadcast_to(cb_ref[...], (sb_rows, W))
        for j in range(CONV_W):
            acc = acc + cw_ref[j:j + 1, :] * zext(t + j)
        y = _layer_norm(acc, clg_ref[...], clb_ref[...])
        ya_ref[t] = (y * _sigmoid(y)).astype(ya_ref.dtype)
    for r in range(n_conv):
        cn_ref[r] = zext(r + ts)

    def pext(r, c0):
        if r < POOL_KEEP:
            return sp_ref[r, :, c0:c0 + LANE]
        return p_ref[r - POOL_KEEP, :, 2 * W + c0:2 * W + c0 + LANE]

    for t in range(ts):
        for gi, win in enumerate(POOL_WINDOWS):
            c0 = gi * LANE
            s = pext(POOL_KEEP + t, c0)
            for k in range(1, win):
                s = s + pext(POOL_KEEP + t - k, c0)
            d = s / float(win) - pext(POOL_KEEP + t, c0)
            yg = _dot(d.astype(BF16), pw_ref[gi].astype(BF16)) * ps_ref[:, c0:c0 + LANE]
            yb_ref[t, :, c0:c0 + LANE] = yg.astype(yb_ref.dtype)
    for r in range(POOL_KEEP):
        pn_ref[r] = sp_ref[r + ts] if r + ts < POOL_KEEP else p_ref[r + ts - POOL_KEEP, :, col(2)]

    for t in range(ts):
        cos = cos_ref[t:t + 1, :]
        sa = sa_ref[t:t + 1, :]
        sb = sb_ref[t:t + 1, :]
        for h in range(N_HEADS):
            c0 = h * LANE
            q_ref[t, :, c0:c0 + LANE] = _norm_rope_head(
                p_ref[t, :, 3 * W + c0:3 * W + c0 + LANE], qg_ref[...], cos, sa, sb)
            k_ref[t, :, c0:c0 + LANE] = _norm_rope_head(
                p_ref[t, :, 4 * W + c0:4 * W + c0 + LANE], kg_ref[...], cos, sa, sb)
        v_ref[t] = p_ref[t, :, col(5)]

    vn = [_layer_norm(p_ref[t, :, col(7)], glg_ref[...], glb_ref[...]) for t in range(ts)]
    for t in range(ts):
        gv_ref[t] = vn[t]
        for g in range(GMLP_GROUPS):
            c0 = g * LANE
            mixed = jnp.broadcast_to(gb_ref[g:g + 1, t:t + 1], (sb_rows, LANE))
            for s_ in range(t + 1):
                mixed = mixed + ws_ref[g, t:t + 1, s_:s_ + 1] * vn[s_][:, c0:c0 + LANE]
            yd_ref[t, :, c0:c0 + LANE] = (
                p_ref[t, :, 6 * W + c0:6 * W + c0 + LANE] * mixed).astype(yd_ref.dtype)


SAMPLE_SEQ_BLOCK = 32


def _sample_mix(proj_s, nb, ts, W, sc_t, sp_t, cw, cb, clg, clb, pw, ps, qg, kg, cos, sa, sb,
                glg, glb, ws, gb):
    G = len(POOL_WINDOWS)
    sblk = SAMPLE_SEQ_BLOCK
    full2 = lambda shape: pl.BlockSpec(shape, lambda i: (0, 0))
    full3 = lambda shape: pl.BlockSpec(shape, lambda i: (0, 0, 0))
    seq3 = lambda rows, c: pl.BlockSpec((rows, sblk, c), lambda i: (0, i, 0))
    tok = lambda dt: jax.ShapeDtypeStruct((ts, nb, W), dt)
    return pl.pallas_call(
        _sample_mix_kernel,
        grid=(nb // sblk,),
        in_specs=[seq3(ts, 8 * W), seq3(CONV_W - 1, W), seq3(POOL_KEEP, W),
                  full2((CONV_W, W)), full2((1, W)), full2((1, W)), full2((1, W)),
                  full3((G, LANE, LANE)), full2((1, W)),
                  full2((1, LANE)), full2((1, LANE)),
                  full2((8, LANE)), full2((8, LANE)), full2((8, LANE)),
                  full2((1, W)), full2((1, W)),
                  full3((GMLP_GROUPS, GMLP_CHUNK, GMLP_CHUNK)), full2((GMLP_GROUPS, GMLP_CHUNK))],
        out_specs=[seq3(ts, W)] * 6 + [seq3(CONV_W - 1, W), seq3(POOL_KEEP, W), seq3(ts, W)],
        out_shape=[tok(BF16), tok(BF16), tok(BF16), tok(F32), tok(F32), tok(F32),
                   jax.ShapeDtypeStruct((CONV_W - 1, nb, W), F32),
                   jax.ShapeDtypeStruct((POOL_KEEP, nb, W), F32), tok(F32)],
        compiler_params=_cp(("arbitrary",), 40),
        name="sample_mix",
    )(proj_s, sc_t, sp_t, cw, cb, clg, clb, pw, ps, qg, kg, cos, sa, sb, glg, glb, ws, gb)


def _attn_s_kernel(pt_ref, q_ref, kn_ref, vn_ref, *refs, n_pages, ts):
    kp = refs[:n_pages]
    vp = refs[n_pages:2 * n_pages]
    o_ref = refs[2 * n_pages]
    kbuf, vbuf = refs[2 * n_pages + 1:]
    nh = N_HEADS
    prow = kp[0].shape[0]
    R, hd = q_ref.shape
    blk_rows = MOBA_BLOCK * nh
    ppb = blk_rows // prow
    nblk = n_pages // ppb
    scale = hd ** -0.5
    q = q_ref[...]
    rhead = lax.broadcasted_iota(jnp.int32, (R, 1), 0) >> 3
    t_row = lax.broadcasted_iota(jnp.int32, (R, 1), 0) & 7
    blkid = lax.broadcasted_iota(jnp.int32, (R, nblk), 1)
    bs = jnp.zeros((R, nblk), F32)
    for j in range(nblk):
        acc8 = None
        for u in range(ppb):
            pg = j * ppb + u
            kpg = kp[pg][...]
            kbuf[pg * prow:(pg + 1) * prow, :] = kpg.astype(BF16)
            vbuf[pg * prow:(pg + 1) * prow, :] = vp[pg][...].astype(BF16)
            part = jnp.sum(kpg.reshape(prow // 8, 8, hd), axis=0)
            acc8 = part if acc8 is None else acc8 + part
        kmean = (acc8[0:nh] + acc8[nh:2 * nh]) * (1.0 / MOBA_BLOCK)
        km_rows = jnp.concatenate([jnp.broadcast_to(kmean[h:h + 1], (8, hd)) for h in range(nh)], axis=0)
        bs = jnp.where(blkid == j, jnp.sum(q * km_rows, -1, keepdims=True), bs)
    sel = _topk_block_cols(bs, blkid >= 0)
    qb = q.astype(BF16)
    s_all = _dot_nt(qb, kbuf[...]) * scale
    own_head = (lax.broadcasted_iota(jnp.int32, (1, blk_rows), 1) & (nh - 1)) == rhead
    s_past = jnp.concatenate(
        [jnp.where(own_head & sel[j], s_all[:, j * blk_rows:(j + 1) * blk_rows], -jnp.inf)
         for j in range(nblk)], axis=-1)
    nc = kn_ref.shape[0]
    ocol = lax.broadcasted_iota(jnp.int32, (1, nc), 1)
    s_own = _dot_nt(qb, kn_ref[...].astype(BF16)) * scale
    s_own = jnp.where(((ocol & (nh - 1)) == rhead) & ((ocol >> 2) <= t_row), s_own, -jnp.inf)
    m = jnp.maximum(jnp.max(s_past, -1, keepdims=True), jnp.max(s_own, -1, keepdims=True))
    p_past = jnp.exp(s_past - m)
    p_own = jnp.exp(s_own - m)
    den = jnp.sum(p_past, -1, keepdims=True) + jnp.sum(p_own, -1, keepdims=True)
    acc = _dot(p_past.astype(BF16), vbuf[...]) + _dot(p_own.astype(BF16), vn_ref[...].astype(BF16))
    o_ref[...] = acc / den


def _attn_sample(pt_flat, q_hr, k_new, v_new, ck, cv, l, nb, ts, n_pages):
    prow, hd = ck.shape[2:]
    R = q_hr.shape[1]
    nc = k_new.shape[1]
    assert N_HEADS == 4 and nc % 8 == 0

    def page_spec(j):
        return pl.BlockSpec((None, None, prow, hd), lambda b, pt: (l, pt[b * n_pages + j], 0, 0))

    in_specs = ([pl.BlockSpec((None, R, hd), lambda b, pt: (b, 0, 0)),
                 pl.BlockSpec((None, nc, hd), lambda b, pt: (b, 0, 0)),
                 pl.BlockSpec((None, nc, hd), lambda b, pt: (b, 0, 0))]
                + [page_spec(j) for j in range(n_pages)] + [page_spec(j) for j in range(n_pages)])
    return pl.pallas_call(
        functools.partial(_attn_s_kernel, n_pages=n_pages, ts=ts),
        grid_spec=pltpu.PrefetchScalarGridSpec(
            num_scalar_prefetch=1, grid=(nb,), in_specs=in_specs,
            out_specs=pl.BlockSpec((None, R, hd), lambda b, pt: (b, 0, 0)),
            scratch_shapes=[pltpu.VMEM((n_pages * prow, hd), BF16), pltpu.VMEM((n_pages * prow, hd), BF16)]),
        out_shape=jax.ShapeDtypeStruct((nb, R, hd), F32),
        compiler_params=_cp(("arbitrary",), 48),
        name="attn_sample",
    )(pt_flat, q_hr, k_new, v_new, *([ck] * n_pages), *([cv] * n_pages))


def _merge_kernel(ya_ref, yb_ref, yc_ref, yd_ref, wa_ref, wb_ref, wc_ref, wd_ref,
                  ga_ref, gb_ref, gc_ref, gd_ref, o_ref, wbf_ref):
    w_refs = (wa_ref, wb_ref, wc_ref, wd_ref)

    @pl.when(pl.program_id(1) == 0)
    def _():
        for b in range(N_BRANCH):
            wbf_ref[b] = w_refs[b][...].astype(BF16)

    acc = None
    for b, (y_ref, g_ref) in enumerate(zip((ya_ref, yb_ref, yc_ref, yd_ref),
                                           (ga_ref, gb_ref, gc_ref, gd_ref))):
        term = _sigmoid(g_ref[...]) * _dot(y_ref[...], wbf_ref[b])
        acc = term if acc is None else acc + term
    o_ref[...] = acc.astype(o_ref.dtype)


def _merge(ys, ws, proj, l, W, D):
    M = proj.shape[0]
    tm, tn = 512, 1024
    npb = D // tn
    gate0 = 8 * W // tn
    y_spec = pl.BlockSpec((tm, W), lambda n, m: (m, 0))
    w_spec = pl.BlockSpec((None, W, tn), lambda n, m: (l, 0, n))
    g_specs = [pl.BlockSpec((tm, tn), functools.partial(lambda n, m, b: (m, gate0 + b * npb + n), b=b))
               for b in range(N_BRANCH)]
    return pl.pallas_call(
        _merge_kernel,
        grid=(npb, M // tm),
        in_specs=[y_spec] * 4 + [w_spec] * 4 + g_specs,
        out_specs=pl.BlockSpec((tm, tn), lambda n, m: (m, n)),
        out_shape=jax.ShapeDtypeStruct((M, D), BF16),
        scratch_shapes=[pltpu.VMEM((N_BRANCH, W, tn), BF16)],
        compiler_params=_cp(("arbitrary", "arbitrary"), 48),
        name="merge",
    )(*ys, *ws, proj, proj, proj, proj)


def _wo_kernel(a_ref, w_ref, x_ref, g_ref, o_ref, wbf_ref):
    @pl.when(pl.program_id(1) == 0)
    def _():
        wbf_ref[...] = w_ref[...].astype(BF16)

    o_ref[...] = x_ref[...] + g_ref[...] * _dot(a_ref[...], wbf_ref[...])


def _wo_residual(merged, w_o, l, x, mr, gate_chunk, seq_of_tile):
    M, D = x.shape
    tn = 1024
    ncb = D // tn
    return pl.pallas_call(
        _wo_kernel,
        grid=(ncb, M // TM),
        in_specs=[pl.BlockSpec((TM, D), lambda n, m: (m, 0)),
                  pl.BlockSpec((None, D, tn), lambda n, m: (l, 0, n)),
                  pl.BlockSpec((TM, tn), lambda n, m: (m, n)),
                  pl.BlockSpec((None, TM, tn), lambda n, m: (seq_of_tile(m), 0, gate_chunk * ncb + n))],
        out_specs=pl.BlockSpec((TM, tn), lambda n, m: (m, n)),
        out_shape=jax.ShapeDtypeStruct((M, D), F32),
        scratch_shapes=[pltpu.VMEM((D, tn), BF16)],
        compiler_params=_cp(("arbitrary", "arbitrary"), 40),
        name="wo_residual",
    )(merged, w_o, x, mr)


def _pack_bf16_pairs(x):
    n = x.shape[1] // 2
    lo = lax.bitcast_convert_type(x[:, :n].astype(BF16).astype(F32), jnp.uint32)
    hi = lax.bitcast_convert_type(x[:, n:].astype(BF16).astype(F32), jnp.uint32)
    return hi | (lo >> 16)


def _unpack_bf16_pairs(w):
    lo = lax.bitcast_convert_type(w << 16, F32).astype(BF16)
    hi = lax.bitcast_convert_type(w & jnp.uint32(0xFFFF0000), F32).astype(BF16)
    return lo, hi


def _norm_router_kernel(x_ref, g_ref, sc_ref, sh_ref, rw_ref, rb_ref,
                        h_ref, idx_ref, gate_ref, rank_ref, cnt_ref, run_ref):
    i = pl.program_id(0)
    E = rw_ref.shape[1]

    @pl.when(i == 0)
    def _():
        run_ref[...] = jnp.zeros(run_ref.shape, F32)

    x = x_ref[...]
    ms = jnp.mean(x * x, -1, keepdims=True)
    y = x * lax.rsqrt(ms + EPS) * g_ref[...]
    h = y * (1.0 + sc_ref[...]) + sh_ref[...]
    h_ref[...] = _pack_bf16_pairs(h)
    logits = jnp.dot(h, rw_ref[...], precision=lax.Precision.HIGHEST,
                     preferred_element_type=F32) + rb_ref[...]

    eidx = lax.broadcasted_iota(jnp.int32, (TM, E), 1).astype(F32)
    kcol = lax.broadcasted_iota(jnp.int32, (TM, TOP_K), 1)
    r_i = lax.broadcasted_iota(jnp.int32, (TM, TM), 0)
    c_i = lax.broadcasted_iota(jnp.int32, (TM, TM), 1)
    before = jnp.where(c_i < r_i, 1.0, 0.0).astype(BF16)
    work = logits
    vals, hots = [], []
    idx_out = jnp.zeros((TM, TOP_K), F32)
    rank_out = jnp.zeros((TM, TOP_K), F32)
    run = run_ref[...]
    for k in range(TOP_K):
        mx = jnp.max(work, -1, keepdims=True)
        am = jnp.min(jnp.where(work == mx, eidx, float(E)), -1, keepdims=True)
        hot = eidx == am
        work = jnp.where(hot, -jnp.inf, work)
        hot_f = jnp.where(hot, 1.0, 0.0)
        earlier = _dot(before, hot_f.astype(BF16))
        rank = jnp.sum(hot_f * (earlier + run), -1, keepdims=True)
        run = run + jnp.sum(hot_f, axis=0, keepdims=True)
        vals.append(mx)
        idx_out = jnp.where(kcol == k, am, idx_out)
        rank_out = jnp.where(kcol == k, rank, rank_out)
    run_ref[...] = run
    cnt_ref[...] = run
    den = jnp.zeros((TM, 1), F32)
    gate_out = jnp.zeros((TM, TOP_K), F32)
    ex = [jnp.exp(v - vals[0]) for v in vals]
    for e_ in ex:
        den = den + e_
    for k in range(TOP_K):
        gate_out = jnp.where(kcol == k, ex[k] / den, gate_out)
    idx_ref[...] = idx_out.astype(jnp.int32)
    rank_ref[...] = rank_out.astype(jnp.int32)
    gate_ref[...] = gate_out


def _norm_router(x, g, mr, sc_chunk, sh_chunk, seq_of_tile, rw, rb):
    M, D = x.shape
    E = rw.shape[1]
    tok = lambda c: pl.BlockSpec((TM, c), lambda i: (i, 0))
    return pl.pallas_call(
        _norm_router_kernel,
        grid=(M // TM,),
        in_specs=[tok(D),
                  pl.BlockSpec((1, D), lambda i: (0, 0)),
                  pl.BlockSpec((None, TM, D), lambda i: (seq_of_tile(i), 0, sc_chunk)),
                  pl.BlockSpec((None, TM, D), lambda i: (seq_of_tile(i), 0, sh_chunk)),
                  pl.BlockSpec((D, E), lambda i: (0, 0)),
                  pl.BlockSpec((1, E), lambda i: (0, 0))],
        out_specs=[tok(D // 2), tok(TOP_K), tok(TOP_K), tok(TOP_K), pl.BlockSpec((1, E), lambda i: (0, 0))],
        out_shape=[jax.ShapeDtypeStruct((M, D // 2), jnp.uint32), jax.ShapeDtypeStruct((M, TOP_K), jnp.int32),
                   jax.ShapeDtypeStruct((M, TOP_K), F32), jax.ShapeDtypeStruct((M, TOP_K), jnp.int32),
                   jax.ShapeDtypeStruct((1, E), F32)],
        scratch_shapes=[pltpu.VMEM((1, E), F32)],
        compiler_params=_cp(("arbitrary",), 40),
        name="norm_router",
    )(x, g, mr, mr, rw, rb)


def _row_copy(src_hbm, dst, src_row, dst_row, sem):
    return pltpu.make_async_copy(src_hbm.at[pl.ds(src_row, 1)], dst.at[pl.ds(dst_row, 1)], sem)


DISPATCH_TM = 512


def _dispatch_kernel(dest_ref, h_ref, xb_init_hbm, xb_hbm, sem):
    del xb_init_hbm
    i = pl.program_id(0)

    def issue(r, c):
        tok = i * DISPATCH_TM + r
        for k in range(TOP_K):
            _row_copy(h_ref, xb_hbm, r, dest_ref[tok * TOP_K + k], sem).start(priority=k % 2)
        return c

    lax.fori_loop(0, DISPATCH_TM, issue, 0, unroll=8)
    for k in range(TOP_K):
        pltpu.make_async_copy(h_ref, xb_hbm.at[pl.ds(0, DISPATCH_TM)], sem).wait()


def _dispatch(dest_flat, h2p, cap):
    M, Dp = h2p.shape
    any_spec = pl.BlockSpec(memory_space=pl.ANY)
    return pl.pallas_call(
        _dispatch_kernel,
        grid_spec=pltpu.PrefetchScalarGridSpec(
            num_scalar_prefetch=1, grid=(M // DISPATCH_TM,),
            in_specs=[pl.BlockSpec((DISPATCH_TM, Dp), lambda i, d: (i, 0)), any_spec], out_specs=any_spec,
            scratch_shapes=[pltpu.SemaphoreType.DMA(())]),
        out_shape=jax.ShapeDtypeStruct((cap, Dp), jnp.uint32),
        input_output_aliases={2: 0},
        compiler_params=_cp(("arbitrary",), 16),
        name="moe_dispatch",
    )(dest_flat, h2p, jnp.zeros((cap, Dp), jnp.uint32))


def _expert_weight_run(be_ref, nu_ref, nxt_ref, n_panels, copies, consume, slot_ref):
    n = pl.program_id(0)
    m = pl.program_id(1)
    e = be_ref[m]
    prev = be_ref[jnp.maximum(m - 1, 0)]

    @pl.when((m < nu_ref[0]) & ((m == 0) | (e != prev)))
    def _():
        @pl.when((n == 0) & (m == 0))
        def _():
            slot_ref[0] = 0
            for c in copies(0, e, n):
                c.start(priority=1)

        slot = slot_ref[0]
        for c in copies(slot, e, n):
            c.wait()
        consume(slot)
        ne = nxt_ref[m]

        @pl.when(ne >= 0)
        def _():
            for c in copies(1 - slot, ne, n):
                c.start(priority=1)

        @pl.when((ne < 0) & (n + 1 < n_panels))
        def _():
            for c in copies(1 - slot, be_ref[0], n + 1):
                c.start(priority=1)

        slot_ref[0] = 1 - slot


def _moe_up_kernel(be_ref, nu_ref, nxt_ref, x_ref, w_hbm, bg_ref, bl_ref, o_ref,
                   wbuf, wgb, wlb, slot_ref, sem, *, row0, tn, nf):
    m = pl.program_id(1)

    def copies(slot, e, n):
        return [pltpu.make_async_copy(w_hbm.at[row0 + e, :, pl.ds(pl.multiple_of((half * nf + n) * tn, tn), tn)],
                                      wbuf.at[slot, half], sem.at[slot, half]) for half in range(2)]

    def consume(slot):
        wgb[...] = wbuf[slot, 0].astype(BF16)
        wlb[...] = wbuf[slot, 1].astype(BF16)

    _expert_weight_run(be_ref, nu_ref, nxt_ref, nf, copies, consume, slot_ref)

    @pl.when(m < nu_ref[0])
    def _():
        x_lo, x_hi = _unpack_bf16_pairs(x_ref[...])
        n = x_lo.shape[1]
        hg = _dot(x_lo, wgb[0:n, :]) + _dot(x_hi, wgb[n:2 * n, :]) + bg_ref[...]
        hl = _dot(x_lo, wlb[0:n, :]) + _dot(x_hi, wlb[n:2 * n, :]) + bl_ref[...]
        hg = jnp.minimum(hg, SWIGLU_LIMIT)
        hl = jnp.clip(hl, -SWIGLU_LIMIT, SWIGLU_LIMIT)
        o_ref[...] = (hg * _sigmoid(SWIGLU_ALPHA * hg) * (hl + 1.0)).astype(o_ref.dtype)

    @pl.when(m >= nu_ref[0])
    def _():
        o_ref[...] = jnp.zeros(o_ref.shape, o_ref.dtype)


def _moe_up(blk_expert, n_used, nxt_expert, xb, w1, b1, l, E):
    cap, Dp = xb.shape
    D = w1.shape[1]
    F = w1.shape[2] // 2
    tn = 1024
    nf = F // tn
    n_blocks = cap // MOE_BM
    return pl.pallas_call(
        functools.partial(_moe_up_kernel, row0=l * E, tn=tn, nf=nf),
        grid_spec=pltpu.PrefetchScalarGridSpec(
            num_scalar_prefetch=3, grid=(nf, n_blocks),
            in_specs=[pl.BlockSpec((MOE_BM, Dp), lambda n, m, be, nu, nx: (m, 0)),
                      pl.BlockSpec(memory_space=pl.ANY),
                      pl.BlockSpec((None, 1, tn), lambda n, m, be, nu, nx: (l * E + be[m], 0, n)),
                      pl.BlockSpec((None, 1, tn), lambda n, m, be, nu, nx: (l * E + be[m], 0, nf + n))],
            out_specs=pl.BlockSpec((MOE_BM, tn), lambda n, m, be, nu, nx: (m, n)),
            scratch_shapes=[pltpu.VMEM((2, 2, D, tn), F32), pltpu.VMEM((D, tn), BF16), pltpu.VMEM((D, tn), BF16),
                            pltpu.SMEM((1,), jnp.int32), pltpu.SemaphoreType.DMA((2, 2))]),
        out_shape=jax.ShapeDtypeStruct((cap, F), BF16),
        compiler_params=_cp(("arbitrary", "arbitrary"), 56),
        name="moe_up",
    )(blk_expert, n_used, nxt_expert, xb, w1, b1, b1)


def _moe_down_kernel(be_ref, nu_ref, nxt_ref, a_ref, w_hbm, b_ref, o_ref, wbuf, wb, slot_ref, sem, *, row0):
    m = pl.program_id(1)

    def copies(slot, e, n):
        return [pltpu.make_async_copy(w_hbm.at[row0 + e], wbuf.at[slot], sem.at[slot])]

    def consume(slot):
        wb[...] = wbuf[slot].astype(BF16)

    _expert_weight_run(be_ref, nu_ref, nxt_ref, 1, copies, consume, slot_ref)

    @pl.when(m < nu_ref[0])
    def _():
        o_ref[...] = _dot(a_ref[...], wb[...]) + b_ref[...]

    @pl.when(m >= nu_ref[0])
    def _():
        o_ref[...] = jnp.zeros(o_ref.shape, o_ref.dtype)


def _moe_down(blk_expert, n_used, nxt_expert, act, w2, b2, l, E):
    cap, F = act.shape
    D = w2.shape[2]
    n_blocks = cap // MOE_BM
    return pl.pallas_call(
        functools.partial(_moe_down_kernel, row0=l * E),
        grid_spec=pltpu.PrefetchScalarGridSpec(
            num_scalar_prefetch=3, grid=(1, n_blocks),
            in_specs=[pl.BlockSpec((MOE_BM, F), lambda n, m, be, nu, nx: (m, 0)),
                      pl.BlockSpec(memory_space=pl.ANY),
                      pl.BlockSpec((None, 1, D), lambda n, m, be, nu, nx: (l * E + be[m], 0, 0))],
            out_specs=pl.BlockSpec((MOE_BM, D), lambda n, m, be, nu, nx: (m, 0)),
            scratch_shapes=[pltpu.VMEM((2, F, D), F32), pltpu.VMEM((F, D), BF16),
                            pltpu.SMEM((1,), jnp.int32), pltpu.SemaphoreType.DMA((2,))]),
        out_shape=jax.ShapeDtypeStruct((cap, D), F32),
        compiler_params=_cp(("arbitrary", "arbitrary"), 56),
        name="moe_down",
    )(blk_expert, n_used, nxt_expert, act, w2, b2)


COMBINE_TM = 128


def _combine_kernel(slot_ref, y_hbm, x_ref, g_ref, gate_ref, o_ref, buf, sem):
    i = pl.program_id(0)

    def gather(tile, s):
        def body(r, c):
            for k in range(TOP_K):
                row = slot_ref[(tile * COMBINE_TM + r) * TOP_K + k]
                _row_copy(y_hbm, buf.at[s, k], row, r, sem.at[s]).start(priority=k % 2)
            return c

        lax.fori_loop(0, COMBINE_TM, body, 0, unroll=8)

    @pl.when(i == 0)
    def _():
        gather(0, 0)

    @pl.when(i + 1 < pl.num_programs(0))
    def _():
        gather(i + 1, (i + 1) & 1)

    s = i & 1
    for k in range(TOP_K):
        pltpu.make_async_copy(y_hbm.at[pl.ds(0, COMBINE_TM)], buf.at[s, k], sem.at[s]).wait()
    y = gate_ref[:, 0:1] * buf[s, 0]
    for k in range(1, TOP_K):
        y = y + gate_ref[:, k:k + 1] * buf[s, k]
    o_ref[...] = x_ref[...] + g_ref[...] * y


def _combine(slot_of, yb, x, mr, gate_chunk, seq_of_tile, gate):
    M, D = x.shape
    per = TM // COMBINE_TM
    return pl.pallas_call(
        _combine_kernel,
        grid_spec=pltpu.PrefetchScalarGridSpec(
            num_scalar_prefetch=1, grid=(M // COMBINE_TM,),
            in_specs=[pl.BlockSpec(memory_space=pl.ANY),
                      pl.BlockSpec((COMBINE_TM, D), lambda i, s: (i, 0)),
                      pl.BlockSpec((None, COMBINE_TM, D), lambda i, s: (seq_of_tile(i // per), 0, gate_chunk)),
                      pl.BlockSpec((COMBINE_TM, TOP_K), lambda i, s: (i, 0))],
            out_specs=pl.BlockSpec((COMBINE_TM, D), lambda i, s: (i, 0)),
            scratch_shapes=[pltpu.VMEM((2, TOP_K, COMBINE_TM, D), F32), pltpu.SemaphoreType.DMA((2,))]),
        out_shape=jax.ShapeDtypeStruct((M, D), F32),
        compiler_params=_cp(("arbitrary",), 32),
        name="moe_combine",
    )(slot_of, yb, x, mr, gate)


def _slot_tables(top_idx, rank, counts, n_blocks):
    E = counts.shape[0]
    padded = ((counts + MOE_BM - 1) // MOE_BM) * MOE_BM
    pad_end = jnp.cumsum(padded)
    pad_start = pad_end - padded
    onehot = top_idx[..., None] == jnp.arange(E, dtype=jnp.int32)
    dest = jnp.sum(jnp.where(onehot, pad_start, 0), -1) + rank
    blk_start = jnp.arange(n_blocks, dtype=jnp.int32) * MOE_BM
    blk_expert = jnp.minimum(jnp.sum(blk_start[:, None] >= pad_end[None, :], -1), E - 1).astype(jnp.int32)
    n_used = (pad_end[-1] // MOE_BM).astype(jnp.int32).reshape(1)
    ids = jnp.arange(E, dtype=jnp.int32)
    later = (ids[None, :] > ids[:, None]) & (counts[None, :] > 0)
    nxt_of_expert = jnp.min(jnp.where(later, ids[None, :], E), axis=1)
    nxt_of_expert = jnp.where(nxt_of_expert == E, -1, nxt_of_expert)
    nxt_expert = jnp.sum(jnp.where(blk_expert[:, None] == ids[None, :], nxt_of_expert[None, :], 0), -1)
    return dest.reshape(-1).astype(jnp.int32), blk_expert, n_used, nxt_expert.astype(jnp.int32)


def _rope_tables(pos, hd):
    rot = hd // 4
    half = rot // 2
    inv = jnp.power(jnp.float32(ROPE_THETA), -jnp.arange(half, dtype=F32) * 2.0 / rot)
    ang = pos.astype(F32)[:, None] * inv[None, :]
    cos, sin = jnp.cos(ang), jnp.sin(ang)
    n = pos.shape[0]
    c = jnp.concatenate([cos, cos, jnp.ones((n, hd - rot), F32)], 1)
    sa = jnp.concatenate([-sin, jnp.zeros((n, hd - half), F32)], 1)
    sb = jnp.concatenate([jnp.zeros((n, half), F32), sin, jnp.zeros((n, hd - rot), F32)], 1)
    return c, sa, sb


def kernel(x_prompt, x_sample, cache_k, cache_v, state_conv, state_pool, page_table, c_prompt, c_sample,
           w_ada, b_ada, norm1_g, norm2_g, w_in, conv_w, conv_b, conv_ln_g, conv_ln_b, conv_out,
           pool_w, pool_scale, pool_out, q_norm_g, k_norm_g, attn_out, gmlp_ln_g, gmlp_ln_b,
           gmlp_ws, gmlp_b, gmlp_out, w_o, router_w, router_b, exp_w1, exp_b1, exp_w2, exp_b2):
    B, T, D = x_prompt.shape
    NB, TS, _ = x_sample.shape
    L = w_ada.shape[0]
    W = D // N_BRANCH
    HD = W // N_HEADS
    E = router_w.shape[2]
    n_pool, PAGE = cache_k.shape[1], cache_k.shape[2]
    n_pages = page_table.shape[1]
    past = n_pages * PAGE
    M_p, M_s = B * T, NB * TS
    M = M_p + M_s
    assert HD == LANE and W == N_HEADS * LANE and T % TM == 0 and TM % NB == 0 and NB % 8 == 0
    assert M % DISPATCH_TM == 0 and M_s % TM == 0 and TS <= 8 and past % MOBA_BLOCK == 0
    tiles_per_seq = T // TM

    def seq_of_tile(i):
        return jnp.minimum(i // tiles_per_seq, B)

    x = jnp.concatenate([x_prompt.reshape(M_p, D), x_sample.transpose(1, 0, 2).reshape(M_s, D)], 0)
    n_c = B + NB
    c_all = jnp.concatenate([c_prompt, c_sample, jnp.zeros((-n_c % 8, D), F32)], 0)
    mod_all = _ada(c_all, w_ada, b_ada)

    cos_p, sa_p, sb_p = _rope_tables(jnp.arange(T), HD)
    cos_s, sa_s, sb_s = [jnp.pad(t, ((0, 8 - TS), (0, 0))) for t in _rope_tables(past + jnp.arange(TS), HD)]
    pt_flat = page_table.reshape(-1).astype(jnp.int32)
    ck = cache_k.reshape(L, n_pool, PAGE * N_HEADS, HD)
    cv = cache_v.reshape(L, n_pool, PAGE * N_HEADS, HD)
    w1 = exp_w1.reshape(L * E, D, exp_w1.shape[3])
    b1 = exp_b1.reshape(L * E, 1, exp_b1.shape[2])
    w2 = exp_w2.reshape(L * E, exp_w2.shape[2], D)
    b2 = exp_b2.reshape(L * E, 1, D)

    def to_bt(a_t):
        return a_t.reshape(TS, NB, -1).transpose(1, 0, 2)

    outs = {k: [] for k in ('kp', 'vp', 'ks', 'vs', 'cp', 'cs', 'pp', 'ps', 'gp', 'gs')}
    for l in range(L):
        mod = mod_all[l]
        mr = jnp.concatenate([jnp.broadcast_to(mod[:B, None, :], (B, TM, 6 * D)),
                              jnp.tile(mod[B:B + NB], (TM // NB, 1))[None]], 0)
        h = _norm_mod(x, norm1_g[l][None], mr, 1, 0, seq_of_tile, BF16)
        tm_proj, tn_proj = next((t, n) for t, n in ((2176, 512), (1088, 1024), (1024, 1024), (512, 1024),
                                                    (256, 1024)) if M % t == 0)
        proj = _panel_mm(h, w_in, l, tm_proj, tn_proj)

        row = lambda v: v[l][None]
        ya_p, conv_p = _conv_prompt(proj, B, T, W, conv_w[l], row(conv_b), row(conv_ln_g), row(conv_ln_b))
        yb_p, pool_p = _pool_prompt(proj, B, T, W, pool_w[l], row(pool_scale))
        q_p, k_p, v_p, kb_p, vb_p, kmean_p = _qkv_prompt(proj, B, T, W, row(q_norm_g), row(k_norm_g),
                                                        cos_p, sa_p, sb_p)
        yc_p = _attn_prompt(q_p, kb_p, vb_p, kmean_p, B, T, W)
        yd_p, gv_p = _gmlp_prompt(proj, B, T, W, row(gmlp_ln_g), row(gmlp_ln_b), gmlp_ws[l],
                                  jnp.transpose(gmlp_b[l]))

        proj_s = proj[M_p:, :8 * W].reshape(TS, NB, 8 * W)
        (ya_s, yb_s, yd_s, q_s, k_s, v_s, conv_s, pool_s, gv_s) = _sample_mix(
            proj_s, NB, TS, W, state_conv[l].transpose(1, 0, 2), state_pool[l].transpose(1, 0, 2),
            conv_w[l], row(conv_b), row(conv_ln_g), row(conv_ln_b), pool_w[l], row(pool_scale),
            row(q_norm_g), row(k_norm_g), cos_s, sa_s, sb_s, row(gmlp_ln_g), row(gmlp_ln_b),
            gmlp_ws[l], gmlp_b[l])
        q_hr = jnp.pad(to_bt(q_s).reshape(NB, TS, N_HEADS, HD).transpose(0, 2, 1, 3),
                       ((0, 0), (0, 0), (0, 8 - TS), (0, 0))).reshape(NB, N_HEADS * 8, HD)
        k_new = to_bt(k_s).reshape(NB, TS * N_HEADS, HD)
        v_new = to_bt(v_s).reshape(NB, TS * N_HEADS, HD)
        att_s = _attn_sample(pt_flat, q_hr, k_new, v_new, ck, cv, l, NB, TS, n_pages)
        yc_s = (att_s.reshape(NB, N_HEADS, 8, HD)[:, :, :TS].transpose(2, 0, 1, 3)
                .reshape(M_s, W).astype(BF16))

        ys = [jnp.concatenate([p_, s_.reshape(M_s, W)], 0) for p_, s_ in
              ((ya_p, ya_s), (yb_p, yb_s), (yc_p, yc_s), (yd_p, yd_s))]
        merged = _merge(ys, (conv_out, pool_out, attn_out, gmlp_out), proj, l, W, D)
        x = _wo_residual(merged, w_o, l, x, mr, 2, seq_of_tile)
        h2p, top_idx, gate, rank, counts = _norm_router(x, norm2_g[l][None], mr, 4, 3, seq_of_tile,
                                                         router_w[l], router_b[l][None])
        n_blocks = -(-M * TOP_K // MOE_BM) + E
        dest, blk_expert, n_used, nxt_expert = _slot_tables(top_idx, rank, counts[0].astype(jnp.int32), n_blocks)
        xb = _dispatch(dest, h2p, n_blocks * MOE_BM)
        act = _moe_up(blk_expert, n_used, nxt_expert, xb, w1, b1, l, E)
        yb = _moe_down(blk_expert, n_used, nxt_expert, act, w2, b2, l, E)
        x = _combine(dest, yb, x, mr, 5, seq_of_tile, gate)

        outs['kp'].append(k_p.reshape(B, T, N_HEADS, HD))
        outs['vp'].append(v_p.reshape(B, T, N_HEADS, HD))
        outs['ks'].append(to_bt(k_s).reshape(NB, TS, N_HEADS, HD))
        outs['vs'].append(to_bt(v_s).reshape(NB, TS, N_HEADS, HD))
        outs['cp'].append(conv_p)
        outs['cs'].append(conv_s.transpose(1, 0, 2))
        outs['pp'].append(pool_p)
        outs['ps'].append(pool_s.transpose(1, 0, 2))
        outs['gp'].append(gv_p)
        outs['gs'].append(to_bt(gv_s))

    y_prompt = x[:M_p].reshape(B, T, D)
    y_sample = to_bt(x[M_p:])
    st = lambda k: jnp.stack(outs[k])
    return (y_prompt, y_sample, st('kp'), st('vp'), st('ks'), st('vs'), st('cp'), st('cs'),
            st('pp'), st('ps'), st('gp'), st('gs'))
```

```python
import functools

import jax
import jax.numpy as jnp
from jax import lax
from jax.experimental import pallas as pl
from jax.experimental.pallas import tpu as pltpu

F32 = jnp.float32
BF16 = jnp.bfloat16

N_BRANCH = 4
CONV_W = 31
POOL_WINDOWS = (2, 4, 8, 16)
POOL_KEEP = max(POOL_WINDOWS) - 1
N_HEADS = 4
ROPE_THETA = 500000.0
MOBA_BLOCK = 256
MOBA_TOPK = 3
GMLP_CHUNK = 128
GMLP_GROUPS = 4
TOP_K = 4
SWIGLU_LIMIT = 7.0
SWIGLU_ALPHA = 1.702
EPS = 1e-6

LANE = 128
TM = 256
CONV_HIST = 32
POOL_HIST = 16
MOE_BM = 256
MIB = 1024 * 1024


def _cp(sem, vmem_mib):
    return pltpu.CompilerParams(dimension_semantics=sem, vmem_limit_bytes=vmem_mib * MIB)


def _sigmoid(x):
    return 1.0 / (1.0 + jnp.exp(-x))


def _layer_norm(x, g, b):
    mu = jnp.mean(x, -1, keepdims=True)
    xc = x - mu
    var = jnp.mean(xc * xc, -1, keepdims=True)
    return xc * lax.rsqrt(var + EPS) * g + b


def _dot(a, b):
    return jnp.dot(a, b, preferred_element_type=F32)


def _dot_nt(a, b, precision=None):
    return lax.dot_general(a, b, (((1,), (1,)), ((), ())), precision=precision,
                           preferred_element_type=F32)


def _ada_kernel(c_ref, w_ref, b_ref, o_ref):
    c = c_ref[...]
    s = (c * _sigmoid(c)).astype(BF16)
    o_ref[...] = _dot(s, w_ref[...].astype(BF16)) + b_ref[...]


def _ada(c_all, w_ada, b_ada):
    L, D, N = w_ada.shape
    R = c_all.shape[0]
    tn = 1024
    return pl.pallas_call(
        _ada_kernel,
        grid=(L, N // tn),
        in_specs=[pl.BlockSpec((R, D), lambda l, n: (0, 0)),
                  pl.BlockSpec((None, D, tn), lambda l, n: (l, 0, n)),
                  pl.BlockSpec((None, 1, tn), lambda l, n: (l, 0, n))],
        out_specs=pl.BlockSpec((None, R, tn), lambda l, n: (l, 0, n)),
        out_shape=jax.ShapeDtypeStruct((L, R, N), F32),
        compiler_params=_cp(("arbitrary", "arbitrary"), 40),
        name="ada",
    )(c_all, w_ada, b_ada.reshape(L, 1, N))


def _norm_mod_kernel(x_ref, g_ref, sc_ref, sh_ref, o_ref):
    x = x_ref[...]
    ms = jnp.mean(x * x, -1, keepdims=True)
    y = x * lax.rsqrt(ms + EPS) * g_ref[...]
    o_ref[...] = (y * (1.0 + sc_ref[...]) + sh_ref[...]).astype(o_ref.dtype)


def _norm_mod(x, g, mr, sc_chunk, sh_chunk, seq_of_tile, out_dtype):
    M, D = x.shape
    return pl.pallas_call(
        _norm_mod_kernel,
        grid=(M // TM,),
        in_specs=[pl.BlockSpec((TM, D), lambda i: (i, 0)),
                  pl.BlockSpec((1, D), lambda i: (0, 0)),
                  pl.BlockSpec((None, TM, D), lambda i: (seq_of_tile(i), 0, sc_chunk)),
                  pl.BlockSpec((None, TM, D), lambda i: (seq_of_tile(i), 0, sh_chunk))],
        out_specs=pl.BlockSpec((TM, D), lambda i: (i, 0)),
        out_shape=jax.ShapeDtypeStruct((M, D), out_dtype),
        compiler_params=_cp(("arbitrary",), 40),
        name="norm_mod",
    )(x, g, mr, mr)


def _panel_mm_kernel(a_ref, w_ref, o_ref, wbf_ref):
    @pl.when(pl.program_id(1) == 0)
    def _():
        wbf_ref[...] = w_ref[...].astype(BF16)

    o_ref[...] = _dot(a_ref[...], wbf_ref[...])


def _panel_mm(a, w, l, tm, tn):
    M, K = a.shape
    N = w.shape[2]
    return pl.pallas_call(
        _panel_mm_kernel,
        grid=(N // tn, M // tm),
        in_specs=[pl.BlockSpec((tm, K), lambda n, m: (m, 0)),
                  pl.BlockSpec((None, K, tn), lambda n, m: (l, 0, n))],
        out_specs=pl.BlockSpec((tm, tn), lambda n, m: (m, n)),
        out_shape=jax.ShapeDtypeStruct((M, N), F32),
        scratch_shapes=[pltpu.VMEM((K, tn), BF16)],
        compiler_params=_cp(("arbitrary", "arbitrary"), 48),
        name="in_proj",
    )(a, w)


def _conv_p_kernel(a_ref, g_ref, w_ref, cb_ref, lg_ref, lb_ref, y_ref, cn_ref, zbuf, ybuf):
    i = pl.program_id(1)
    W = a_ref.shape[1]

    @pl.when(i == 0)
    def _():
        zbuf[0:CONV_HIST, :] = jnp.zeros((CONV_HIST, W), F32)

    @pl.when(i > 0)
    def _():
        zbuf[0:CONV_HIST, :] = zbuf[TM:TM + CONV_HIST, :]

    zbuf[CONV_HIST:CONV_HIST + TM, :] = a_ref[...] * _sigmoid(g_ref[...])
    base = CONV_HIST - (CONV_W - 1)
    rb = 128
    for c0 in range(0, W, LANE):
        for r0 in range(0, TM, rb):
            acc = jnp.broadcast_to(cb_ref[:, c0:c0 + LANE], (rb, LANE))
            for j in range(CONV_W):
                acc = acc + w_ref[j:j + 1, c0:c0 + LANE] * zbuf[pl.ds(base + r0 + j, rb), c0:c0 + LANE]
            ybuf[r0:r0 + rb, c0:c0 + LANE] = acc
    y = _layer_norm(ybuf[...], lg_ref[...], lb_ref[...])
    y_ref[...] = (y * _sigmoid(y)).astype(y_ref.dtype)

    @pl.when(i == pl.num_programs(1) - 1)
    def _():
        cn_ref[...] = zbuf[CONV_HIST + TM - (CONV_W - 1):CONV_HIST + TM, :]


def _conv_prompt(proj, B, T, W, cw, cb, lg, lb):
    nt = T // TM
    return pl.pallas_call(
        _conv_p_kernel,
        grid=(B, nt),
        in_specs=[pl.BlockSpec((TM, W), lambda b, i: (b * nt + i, 0)),
                  pl.BlockSpec((TM, W), lambda b, i: (b * nt + i, 1)),
                  pl.BlockSpec((CONV_W, W), lambda b, i: (0, 0)),
                  pl.BlockSpec((1, W), lambda b, i: (0, 0)),
                  pl.BlockSpec((1, W), lambda b, i: (0, 0)),
                  pl.BlockSpec((1, W), lambda b, i: (0, 0))],
        out_specs=[pl.BlockSpec((TM, W), lambda b, i: (b * nt + i, 0)),
                   pl.BlockSpec((None, CONV_W - 1, W), lambda b, i: (b, 0, 0))],
        out_shape=[jax.ShapeDtypeStruct((B * T, W), BF16),
                   jax.ShapeDtypeStruct((B, CONV_W - 1, W), F32)],
        scratch_shapes=[pltpu.VMEM((CONV_HIST + TM, W), F32), pltpu.VMEM((TM, W), F32)],
        compiler_params=_cp(("arbitrary", "arbitrary"), 32),
        name="conv_prompt",
    )(proj, proj, cw, cb, lg, lb)


def _pool_p_kernel(p_ref, pw_ref, ps_ref, y_ref, pn_ref, ebuf):
    i = pl.program_id(1)
    W = p_ref.shape[1]

    @pl.when(i == 0)
    def _():
        ebuf[0:POOL_HIST, :] = jnp.zeros((POOL_HIST, W), F32)

    @pl.when(i > 0)
    def _():
        ebuf[0:POOL_HIST, :] = ebuf[TM:TM + POOL_HIST, :]

    ebuf[POOL_HIST:POOL_HIST + TM, :] = p_ref[...]
    t_abs = i * TM + lax.broadcasted_iota(jnp.int32, (TM, 1), 0)
    for gi, win in enumerate(POOL_WINDOWS):
        c0 = gi * LANE
        s = ebuf[POOL_HIST:POOL_HIST + TM, c0:c0 + LANE]
        for k in range(1, win):
            s = s + ebuf[pl.ds(POOL_HIST - k, TM), c0:c0 + LANE]
        cnt = jnp.minimum(t_abs + 1, win).astype(F32)
        d = s / cnt - p_ref[:, c0:c0 + LANE]
        yg = _dot(d.astype(BF16), pw_ref[gi].astype(BF16)) * ps_ref[:, c0:c0 + LANE]
        y_ref[:, c0:c0 + LANE] = yg.astype(y_ref.dtype)

    @pl.when(i == pl.num_programs(1) - 1)
    def _():
        pn_ref[...] = ebuf[POOL_HIST + TM - POOL_KEEP:POOL_HIST + TM, :]


def _pool_prompt(proj, B, T, W, pw, ps):
    nt = T // TM
    G = len(POOL_WINDOWS)
    return pl.pallas_call(
        _pool_p_kernel,
        grid=(B, nt),
        in_specs=[pl.BlockSpec((TM, W), lambda b, i: (b * nt + i, 2)),
                  pl.BlockSpec((G, LANE, LANE), lambda b, i: (0, 0, 0)),
                  pl.BlockSpec((1, W), lambda b, i: (0, 0))],
        out_specs=[pl.BlockSpec((TM, W), lambda b, i: (b * nt + i, 0)),
                   pl.BlockSpec((None, POOL_KEEP, W), lambda b, i: (b, 0, 0))],
        out_shape=[jax.ShapeDtypeStruct((B * T, W), BF16),
                   jax.ShapeDtypeStruct((B, POOL_KEEP, W), F32)],
        scratch_shapes=[pltpu.VMEM((POOL_HIST + TM, W), F32)],
        compiler_params=_cp(("arbitrary", "arbitrary"), 32),
        name="pool_prompt",
    )(proj, pw, ps)


def _norm_rope_head(xh, g, cos, sa, sb):
    ms = jnp.mean(xh * xh, -1, keepdims=True)
    xn = xh * lax.rsqrt(ms + EPS) * g
    hd = xh.shape[1]
    half = hd // 8
    up = pltpu.roll(xn, hd - half, axis=1)
    dn = pltpu.roll(xn, half, axis=1)
    return xn * cos + up * sa + dn * sb


def _qkv_p_kernel(q_ref, k_ref, v_ref, qg_ref, kg_ref, cos_ref, sa_ref, sb_ref,
                  qo_ref, ko_ref, vo_ref, kb_ref, vb_ref, km_ref):
    i = pl.program_id(1)
    cos, sa, sb = cos_ref[...], sa_ref[...], sb_ref[...]

    @pl.when(i == 0)
    def _():
        km_ref[...] = jnp.zeros(km_ref.shape, F32)

    blk_row = lax.broadcasted_iota(jnp.int32, (km_ref.shape[0], LANE), 0)
    for h in range(N_HEADS):
        c0 = h * LANE
        qo_ref[:, c0:c0 + LANE] = _norm_rope_head(q_ref[:, c0:c0 + LANE], qg_ref[...], cos, sa, sb)
        kh = _norm_rope_head(k_ref[:, c0:c0 + LANE], kg_ref[...], cos, sa, sb)
        ko_ref[:, c0:c0 + LANE] = kh
        kb_ref[:, c0:c0 + LANE] = kh.astype(BF16)
        km_ref[:, c0:c0 + LANE] = jnp.where(blk_row == i, jnp.mean(kh, axis=0, keepdims=True),
                                            km_ref[:, c0:c0 + LANE])
    v = v_ref[...]
    vo_ref[...] = v
    vb_ref[...] = v.astype(BF16)


def _qkv_prompt(proj, B, T, W, qg, kg, cos, sa, sb):
    nt = T // TM
    row = lambda b, i: (b * nt + i, 0)
    return pl.pallas_call(
        _qkv_p_kernel,
        grid=(B, nt),
        in_specs=[pl.BlockSpec((TM, W), lambda b, i: (b * nt + i, 3)),
                  pl.BlockSpec((TM, W), lambda b, i: (b * nt + i, 4)),
                  pl.BlockSpec((TM, W), lambda b, i: (b * nt + i, 5)),
                  pl.BlockSpec((1, LANE), lambda b, i: (0, 0)),
                  pl.BlockSpec((1, LANE), lambda b, i: (0, 0)),
                  pl.BlockSpec((TM, LANE), lambda b, i: (i, 0)),
                  pl.BlockSpec((TM, LANE), lambda b, i: (i, 0)),
                  pl.BlockSpec((TM, LANE), lambda b, i: (i, 0))],
        out_specs=[pl.BlockSpec((TM, W), row), pl.BlockSpec((TM, W), row), pl.BlockSpec((TM, W), row),
                   pl.BlockSpec((TM, W), row), pl.BlockSpec((TM, W), row),
                   pl.BlockSpec((nt, W), lambda b, i: (b, 0))],
        out_shape=[jax.ShapeDtypeStruct((B * T, W), F32), jax.ShapeDtypeStruct((B * T, W), F32),
                   jax.ShapeDtypeStruct((B * T, W), F32), jax.ShapeDtypeStruct((B * T, W), BF16),
                   jax.ShapeDtypeStruct((B * T, W), BF16), jax.ShapeDtypeStruct((B * nt, W), F32)],
        compiler_params=_cp(("arbitrary", "arbitrary"), 32),
        name="qkv_prompt",
    )(proj, proj, proj, qg, kg, cos, sa, sb)


def _topk_block_cols(bs, past):
    nb = bs.shape[1]
    jidx = lax.broadcasted_iota(jnp.int32, bs.shape, 1)
    cols = []
    for n in range(nb):
        bn = bs[:, n:n + 1]
        beats = ((bs > bn) | ((bs == bn) & (jidx < n))) & past
        rank = jnp.sum(beats.astype(F32), axis=-1, keepdims=True)
        cols.append(rank < (MOBA_TOPK - 0.5))
    return cols


def _attn_p_kernel(q_ref, k_ref, v_ref, km_ref, o_ref):
    i = pl.program_id(1)
    nb = km_ref.shape[0]
    scale = LANE ** -0.5
    row = lax.broadcasted_iota(jnp.int32, (TM, MOBA_BLOCK), 0)
    col = lax.broadcasted_iota(jnp.int32, (TM, MOBA_BLOCK), 1)
    tri = col <= row
    blk = lax.broadcasted_iota(jnp.int32, (TM, nb), 1)

    def attend(own):
        nk = (own + 1) * MOBA_BLOCK
        for h in range(N_HEADS):
            hs = slice(h * LANE, (h + 1) * LANE)
            q = q_ref[:, hs]
            s = _dot_nt(q.astype(BF16), k_ref[0:nk, hs]) * scale
            pieces = []
            if own > 0:
                bs = _dot_nt(q, km_ref[:, hs], precision=lax.Precision.HIGHEST)
                sel = _topk_block_cols(bs, blk < own)
                pieces = [jnp.where(sel[j], s[:, j * MOBA_BLOCK:(j + 1) * MOBA_BLOCK], -jnp.inf)
                          for j in range(own)]
            pieces.append(jnp.where(tri, s[:, own * MOBA_BLOCK:nk], -jnp.inf))
            s = jnp.concatenate(pieces, axis=-1) if own > 0 else pieces[0]
            m = jnp.max(s, -1, keepdims=True)
            p = jnp.exp(s - m)
            den = jnp.sum(p, -1, keepdims=True)
            o_ref[:, hs] = (_dot(p.astype(BF16), v_ref[0:nk, hs]) / den).astype(o_ref.dtype)

    for own in range(nb):
        pl.when(i == own)(functools.partial(attend, own))


def _attn_prompt(q, kb, vb, kmean, B, T, W):
    nt = T // TM
    assert TM == MOBA_BLOCK
    return pl.pallas_call(
        _attn_p_kernel,
        grid=(B, nt),
        in_specs=[pl.BlockSpec((TM, W), lambda b, i: (b * nt + i, 0)),
                  pl.BlockSpec((T, W), lambda b, i: (b, 0)),
                  pl.BlockSpec((T, W), lambda b, i: (b, 0)),
                  pl.BlockSpec((nt, W), lambda b, i: (b, 0))],
        out_specs=pl.BlockSpec((TM, W), lambda b, i: (b * nt + i, 0)),
        out_shape=jax.ShapeDtypeStruct((B * T, W), BF16),
        compiler_params=_cp(("arbitrary", "arbitrary"), 40),
        name="attn_prompt",
    )(q, kb, vb, kmean)


def _gmlp_p_kernel(u_ref, v_ref, lg_ref, lb_ref, ws_ref, bt_ref, y_ref, gv_ref, vbuf):
    i = pl.program_id(1)
    vbuf[...] = _layer_norm(v_ref[...], lg_ref[...], lb_ref[...])
    row = lax.broadcasted_iota(jnp.int32, (GMLP_CHUNK, GMLP_CHUNK), 0)
    col = lax.broadcasted_iota(jnp.int32, (GMLP_CHUNK, GMLP_CHUNK), 1)
    tri = col <= row
    for g in range(GMLP_GROUPS):
        c0 = g * LANE
        w = jnp.where(tri, ws_ref[g], 0.0).astype(BF16)
        for r0 in range(0, TM, GMLP_CHUNK):
            mixed = _dot(w, vbuf[r0:r0 + GMLP_CHUNK, c0:c0 + LANE].astype(BF16)) + bt_ref[:, g:g + 1]
            y_ref[r0:r0 + GMLP_CHUNK, c0:c0 + LANE] = (
                u_ref[r0:r0 + GMLP_CHUNK, c0:c0 + LANE] * mixed).astype(y_ref.dtype)

    @pl.when(i == pl.num_programs(1) - 1)
    def _():
        gv_ref[...] = vbuf[TM - GMLP_CHUNK:TM, :]


def _gmlp_prompt(proj, B, T, W, lg, lb, ws, bt):
    nt = T // TM
    return pl.pallas_call(
        _gmlp_p_kernel,
        grid=(B, nt),
        in_specs=[pl.BlockSpec((TM, W), lambda b, i: (b * nt + i, 6)),
                  pl.BlockSpec((TM, W), lambda b, i: (b * nt + i, 7)),
                  pl.BlockSpec((1, W), lambda b, i: (0, 0)),
                  pl.BlockSpec((1, W), lambda b, i: (0, 0)),
                  pl.BlockSpec((GMLP_GROUPS, GMLP_CHUNK, GMLP_CHUNK), lambda b, i: (0, 0, 0)),
                  pl.BlockSpec((GMLP_CHUNK, GMLP_GROUPS), lambda b, i: (0, 0))],
        out_specs=[pl.BlockSpec((TM, W), lambda b, i: (b * nt + i, 0)),
                   pl.BlockSpec((None, GMLP_CHUNK, W), lambda b, i: (b, 0, 0))],
        out_shape=[jax.ShapeDtypeStruct((B * T, W), BF16),
                   jax.ShapeDtypeStruct((B, GMLP_CHUNK, W), F32)],
        scratch_shapes=[pltpu.VMEM((TM, W), F32)],
        compiler_params=_cp(("arbitrary", "arbitrary"), 32),
        name="gmlp_prompt",
    )(proj, proj, lg, lb, ws, bt)


def _sample_mix_kernel(p_ref, sc_ref, sp_ref, cw_ref, cb_ref, clg_ref, clb_ref, pw_ref, ps_ref,
                       qg_ref, kg_ref, cos_ref, sa_ref, sb_ref, glg_ref, glb_ref, ws_ref, gb_ref,
                       ya_ref, yb_ref, yd_ref, q_ref, k_ref, v_ref, cn_ref, pn_ref, gv_ref):
    ts, sb_rows, _ = p_ref.shape
    W = ya_ref.shape[2]
    n_conv = CONV_W - 1

    def col(c):
        return slice(c * W, (c + 1) * W)

    z = [p_ref[t, :, col(0)] * _sigmoid(p_ref[t, :, col(1)]) for t in range(ts)]

    def zext(r):
        return sc_ref[r] if r < n_conv else z[r - n_conv]

    for t in range(ts):
        acc = jnp.broadcast_to(cb_ref[...], (sb_rows, W))
        for j in range(CONV_W):
            acc = acc + cw_ref[j:j + 1, :] * zext(t + j)
        y = _layer_norm(acc, clg_ref[...], clb_ref[...])
        ya_ref[t] = (y * _sigmoid(y)).astype(ya_ref.dtype)
    for r in range(n_conv):
        cn_ref[r] = zext(r + ts)

    def pext(r, c0):
        if r < POOL_KEEP:
            return sp_ref[r, :, c0:c0 + LANE]
        return p_ref[r - POOL_KEEP, :, 2 * W + c0:2 * W + c0 + LANE]

    for t in range(ts):
        for gi, win in enumerate(POOL_WINDOWS):
            c0 = gi * LANE
            s = pext(POOL_KEEP + t, c0)
            for k in range(1, win):
                s = s + pext(POOL_KEEP + t - k, c0)
            d = s / float(win) - pext(POOL_KEEP + t, c0)
            yg = _dot(d.astype(BF16), pw_ref[gi].astype(BF16)) * ps_ref[:, c0:c0 + LANE]
            yb_ref[t, :, c0:c0 + LANE] = yg.astype(yb_ref.dtype)
    for r in range(POOL_KEEP):
        pn_ref[r] = sp_ref[r + ts] if r + ts < POOL_KEEP else p_ref[r + ts - POOL_KEEP, :, col(2)]

    for t in range(ts):
        cos = cos_ref[t:t + 1, :]
        sa = sa_ref[t:t + 1, :]
        sb = sb_ref[t:t + 1, :]
        for h in range(N_HEADS):
            c0 = h * LANE
            q_ref[t, :, c0:c0 + LANE] = _norm_rope_head(
                p_ref[t, :, 3 * W + c0:3 * W + c0 + LANE], qg_ref[...], cos, sa, sb)
            k_ref[t, :, c0:c0 + LANE] = _norm_rope_head(
                p_ref[t, :, 4 * W + c0:4 * W + c0 + LANE], kg_ref[...], cos, sa, sb)
        v_ref[t] = p_ref[t, :, col(5)]

    vn = [_layer_norm(p_ref[t, :, col(7)], glg_ref[...], glb_ref[...]) for t in range(ts)]
    for t in range(ts):
        gv_ref[t] = vn[t]
        for g in range(GMLP_GROUPS):
            c0 = g * LANE
            mixed = jnp.broadcast_to(gb_ref[g:g + 1, t:t + 1], (sb_rows, LANE))
            for s_ in range(t + 1):
                mixed = mixed + ws_ref[g, t:t + 1, s_:s_ + 1] * vn[s_][:, c0:c0 + LANE]
            yd_ref[t, :, c0:c0 + LANE] = (
                p_ref[t, :, 6 * W + c0:6 * W + c0 + LANE] * mixed).astype(yd_ref.dtype)


SAMPLE_SEQ_BLOCK = 32


def _sample_mix(proj_s, nb, ts, W, sc_t, sp_t, cw, cb, clg, clb, pw, ps, qg, kg, cos, sa, sb,
                glg, glb, ws, gb):
    G = len(POOL_WINDOWS)
    sblk = SAMPLE_SEQ_BLOCK
    full2 = lambda shape: pl.BlockSpec(shape, lambda i: (0, 0))
    full3 = lambda shape: pl.BlockSpec(shape, lambda i: (0, 0, 0))
    seq3 = lambda rows, c: pl.BlockSpec((rows, sblk, c), lambda i: (0, i, 0))
    tok = lambda dt: jax.ShapeDtypeStruct((ts, nb, W), dt)
    return pl.pallas_call(
        _sample_mix_kernel,
        grid=(nb // sblk,),
        in_specs=[seq3(ts, 8 * W), seq3(CONV_W - 1, W), seq3(POOL_KEEP, W),
                  full2((CONV_W, W)), full2((1, W)), full2((1, W)), full2((1, W)),
                  full3((G, LANE, LANE)), full2((1, W)),
                  full2((1, LANE)), full2((1, LANE)),
                  full2((8, LANE)), full2((8, LANE)), full2((8, LANE)),
                  full2((1, W)), full2((1, W)),
                  full3((GMLP_GROUPS, GMLP_CHUNK, GMLP_CHUNK)), full2((GMLP_GROUPS, GMLP_CHUNK))],
        out_specs=[seq3(ts, W)] * 6 + [seq3(CONV_W - 1, W), seq3(POOL_KEEP, W), seq3(ts, W)],
        out_shape=[tok(BF16), tok(BF16), tok(BF16), tok(F32), tok(F32), tok(F32),
                   jax.ShapeDtypeStruct((CONV_W - 1, nb, W), F32),
                   jax.ShapeDtypeStruct((POOL_KEEP, nb, W), F32), tok(F32)],
        compiler_params=_cp(("arbitrary",), 40),
        name="sample_mix",
    )(proj_s, sc_t, sp_t, cw, cb, clg, clb, pw, ps, qg, kg, cos, sa, sb, glg, glb, ws, gb)


def _attn_s_kernel(pt_ref, q_ref, kn_ref, vn_ref, *refs, n_pages, ts):
    kp = refs[:n_pages]
    vp = refs[n_pages:2 * n_pages]
    o_ref = refs[2 * n_pages]
    kbuf, vbuf = refs[2 * n_pages + 1:]
    nh = N_HEADS
    prow = kp[0].shape[0]
    R, hd = q_ref.shape
    blk_rows = MOBA_BLOCK * nh
    ppb = blk_rows // prow
    nblk = n_pages // ppb
    scale = hd ** -0.5
    q = q_ref[...]
    rhead = lax.broadcasted_iota(jnp.int32, (R, 1), 0) >> 3
    t_row = lax.broadcasted_iota(jnp.int32, (R, 1), 0) & 7
    blkid = lax.broadcasted_iota(jnp.int32, (R, nblk), 1)
    bs = jnp.zeros((R, nblk), F32)
    for j in range(nblk):
        acc8 = None
        for u in range(ppb):
            pg = j * ppb + u
            kpg = kp[pg][...]
            kbuf[pg * prow:(pg + 1) * prow, :] = kpg.astype(BF16)
            vbuf[pg * prow:(pg + 1) * prow, :] = vp[pg][...].astype(BF16)
            part = jnp.sum(kpg.reshape(prow // 8, 8, hd), axis=0)
            acc8 = part if acc8 is None else acc8 + part
        kmean = (acc8[0:nh] + acc8[nh:2 * nh]) * (1.0 / MOBA_BLOCK)
        km_rows = jnp.concatenate([jnp.broadcast_to(kmean[h:h + 1], (8, hd)) for h in range(nh)], axis=0)
        bs = jnp.where(blkid == j, jnp.sum(q * km_rows, -1, keepdims=True), bs)
    sel = _topk_block_cols(bs, blkid >= 0)
    qb = q.astype(BF16)
    s_all = _dot_nt(qb, kbuf[...]) * scale
    own_head = (lax.broadcasted_iota(jnp.int32, (1, blk_rows), 1) & (nh - 1)) == rhead
    s_past = jnp.concatenate(
        [jnp.where(own_head & sel[j], s_all[:, j * blk_rows:(j + 1) * blk_rows], -jnp.inf)
         for j in range(nblk)], axis=-1)
    nc = kn_ref.shape[0]
    ocol = lax.broadcasted_iota(jnp.int32, (1, nc), 1)
    s_own = _dot_nt(qb, kn_ref[...].astype(BF16)) * scale
    s_own = jnp.where(((ocol & (nh - 1)) == rhead) & ((ocol >> 2) <= t_row), s_own, -jnp.inf)
    m = jnp.maximum(jnp.max(s_past, -1, keepdims=True), jnp.max(s_own, -1, keepdims=True))
    p_past = jnp.exp(s_past - m)
    p_own = jnp.exp(s_own - m)
    den = jnp.sum(p_past, -1, keepdims=True) + jnp.sum(p_own, -1, keepdims=True)
    acc = _dot(p_past.astype(BF16), vbuf[...]) + _dot(p_own.astype(BF16), vn_ref[...].astype(BF16))
    o_ref[...] = acc / den


def _attn_sample(pt_flat, q_hr, k_new, v_new, ck, cv, l, nb, ts, n_pages):
    prow, hd = ck.shape[2:]
    R = q_hr.shape[1]
    nc = k_new.shape[1]
    assert N_HEADS == 4 and nc % 8 == 0

    def page_spec(j):
        return pl.BlockSpec((None, None, prow, hd), lambda b, pt: (l, pt[b * n_pages + j], 0, 0))

    in_specs = ([pl.BlockSpec((None, R, hd), lambda b, pt: (b, 0, 0)),
                 pl.BlockSpec((None, nc, hd), lambda b, pt: (b, 0, 0)),
                 pl.BlockSpec((None, nc, hd), lambda b, pt: (b, 0, 0))]
                + [page_spec(j) for j in range(n_pages)] + [page_spec(j) for j in range(n_pages)])
    return pl.pallas_call(
        functools.partial(_attn_s_kernel, n_pages=n_pages, ts=ts),
        grid_spec=pltpu.PrefetchScalarGridSpec(
            num_scalar_prefetch=1, grid=(nb,), in_specs=in_specs,
            out_specs=pl.BlockSpec((None, R, hd), lambda b, pt: (b, 0, 0)),
            scratch_shapes=[pltpu.VMEM((n_pages * prow, hd), BF16), pltpu.VMEM((n_pages * prow, hd), BF16)]),
        out_shape=jax.ShapeDtypeStruct((nb, R, hd), F32),
        compiler_params=_cp(("arbitrary",), 48),
        name="attn_sample",
    )(pt_flat, q_hr, k_new, v_new, *([ck] * n_pages), *([cv] * n_pages))


def _merge_kernel(*refs, n_prompt_tiles):
    yp_refs, ys_refs = refs[0:4], refs[4:8]
    w_refs, g_refs = refs[8:12], refs[12:16]
    o_ref, wbf_ref = refs[16], refs[17]
    m = pl.program_id(1)

    @pl.when(m == 0)
    def _():
        for b in range(N_BRANCH):
            wbf_ref[b] = w_refs[b][...].astype(BF16)

    acc = None
    for b in range(N_BRANCH):
        y = jnp.where(m < n_prompt_tiles, yp_refs[b][...], ys_refs[b][...])
        term = _sigmoid(g_refs[b][...]) * _dot(y, wbf_ref[b])
        acc = term if acc is None else acc + term
    o_ref[...] = acc.astype(o_ref.dtype)


def _merge(ys_prompt, ys_sample, ws, proj, l, W, D):
    M = proj.shape[0]
    tm, tn = 512, 1024
    npb = D // tn
    gate0 = 8 * W // tn
    npt = ys_prompt[0].shape[0] // tm
    assert ys_prompt[0].shape[0] % tm == 0 and ys_sample[0].shape[0] % tm == 0
    yp_spec = pl.BlockSpec((tm, W), lambda n, m: (jnp.minimum(m, npt - 1), 0))
    ys_spec = pl.BlockSpec((tm, W), lambda n, m: (jnp.maximum(m - npt, 0), 0))
    w_spec = pl.BlockSpec((None, W, tn), lambda n, m: (l, 0, n))
    g_specs = [pl.BlockSpec((tm, tn), functools.partial(lambda n, m, b: (m, gate0 + b * npb + n), b=b))
               for b in range(N_BRANCH)]
    return pl.pallas_call(
        functools.partial(_merge_kernel, n_prompt_tiles=npt),
        grid=(npb, M // tm),
        in_specs=[yp_spec] * 4 + [ys_spec] * 4 + [w_spec] * 4 + g_specs,
        out_specs=pl.BlockSpec((tm, tn), lambda n, m: (m, n)),
        out_shape=jax.ShapeDtypeStruct((M, D), BF16),
        scratch_shapes=[pltpu.VMEM((N_BRANCH, W, tn), BF16)],
        compiler_params=_cp(("arbitrary", "arbitrary"), 48),
        name="merge",
    )(*ys_prompt, *ys_sample, *ws, proj, proj, proj, proj)


def _wo_kernel(a_ref, w_ref, x_ref, g_ref, o_ref, wbf_ref):
    @pl.when(pl.program_id(1) == 0)
    def _():
        wbf_ref[...] = w_ref[...].astype(BF16)

    o_ref[...] = x_ref[...] + g_ref[...] * _dot(a_ref[...], wbf_ref[...])


def _wo_residual(merged, w_o, l, x, mr, gate_chunk, seq_of_tile):
    M, D = x.shape
    tn = 1024
    ncb = D // tn
    return pl.pallas_call(
        _wo_kernel,
        grid=(ncb, M // TM),
        in_specs=[pl.BlockSpec((TM, D), lambda n, m: (m, 0)),
                  pl.BlockSpec((None, D, tn), lambda n, m: (l, 0, n)),
                  pl.BlockSpec((TM, tn), lambda n, m: (m, n)),
                  pl.BlockSpec((None, TM, tn), lambda n, m: (seq_of_tile(m), 0, gate_chunk * ncb + n))],
        out_specs=pl.BlockSpec((TM, tn), lambda n, m: (m, n)),
        out_shape=jax.ShapeDtypeStruct((M, D), F32),
        scratch_shapes=[pltpu.VMEM((D, tn), BF16)],
        compiler_params=_cp(("arbitrary", "arbitrary"), 40),
        name="wo_residual",
    )(merged, w_o, x, mr)


def _pack_bf16_pairs(x):
    n = x.shape[1] // 2
    lo = lax.bitcast_convert_type(x[:, :n].astype(BF16).astype(F32), jnp.uint32)
    hi = lax.bitcast_convert_type(x[:, n:].astype(BF16).astype(F32), jnp.uint32)
    return hi | (lo >> 16)


def _unpack_bf16_pairs(w):
    lo = lax.bitcast_convert_type(w << 16, F32).astype(BF16)
    hi = lax.bitcast_convert_type(w & jnp.uint32(0xFFFF0000), F32).astype(BF16)
    return lo, hi


def _norm_router_kernel(x_ref, g_ref, sc_ref, sh_ref, rw_ref, rb_ref,
                        h_ref, idx_ref, gate_ref, rank_ref, cnt_ref, run_ref):
    i = pl.program_id(0)
    E = rw_ref.shape[1]

    @pl.when(i == 0)
    def _():
        run_ref[...] = jnp.zeros(run_ref.shape, F32)

    x = x_ref[...]
    ms = jnp.mean(x * x, -1, keepdims=True)
    y = x * lax.rsqrt(ms + EPS) * g_ref[...]
    h = y * (1.0 + sc_ref[...]) + sh_ref[...]
    h_ref[...] = _pack_bf16_pairs(h)
    logits = jnp.dot(h, rw_ref[...], precision=lax.Precision.HIGHEST,
                     preferred_element_type=F32) + rb_ref[...]

    eidx = lax.broadcasted_iota(jnp.int32, (TM, E), 1).astype(F32)
    kcol = lax.broadcasted_iota(jnp.int32, (TM, TOP_K), 1)
    r_i = lax.broadcasted_iota(jnp.int32, (TM, TM), 0)
    c_i = lax.broadcasted_iota(jnp.int32, (TM, TM), 1)
    before = jnp.where(c_i < r_i, 1.0, 0.0).astype(BF16)
    work = logits
    vals, hots = [], []
    idx_out = jnp.zeros((TM, TOP_K), F32)
    rank_out = jnp.zeros((TM, TOP_K), F32)
    run = run_ref[...]
    for k in range(TOP_K):
        mx = jnp.max(work, -1, keepdims=True)
        am = jnp.min(jnp.where(work == mx, eidx, float(E)), -1, keepdims=True)
        hot = eidx == am
        work = jnp.where(hot, -jnp.inf, work)
        hot_f = jnp.where(hot, 1.0, 0.0)
        earlier = _dot(before, hot_f.astype(BF16))
        rank = jnp.sum(hot_f * (earlier + run), -1, keepdims=True)
        run = run + jnp.sum(hot_f, axis=0, keepdims=True)
        vals.append(mx)
        idx_out = jnp.where(kcol == k, am, idx_out)
        rank_out = jnp.where(kcol == k, rank, rank_out)
    run_ref[...] = run
    cnt_ref[...] = run
    den = jnp.zeros((TM, 1), F32)
    gate_out = jnp.zeros((TM, TOP_K), F32)
    ex = [jnp.exp(v - vals[0]) for v in vals]
    for e_ in ex:
        den = den + e_
    for k in range(TOP_K):
        gate_out = jnp.where(kcol == k, ex[k] / den, gate_out)
    idx_ref[...] = idx_out.astype(jnp.int32)
    rank_ref[...] = rank_out.astype(jnp.int32)
    gate_ref[...] = gate_out


def _norm_router(x, g, mr, sc_chunk, sh_chunk, seq_of_tile, rw, rb):
    M, D = x.shape
    E = rw.shape[1]
    tok = lambda c: pl.BlockSpec((TM, c), lambda i: (i, 0))
    return pl.pallas_call(
        _norm_router_kernel,
        grid=(M // TM,),
        in_specs=[tok(D),
                  pl.BlockSpec((1, D), lambda i: (0, 0)),
                  pl.BlockSpec((None, TM, D), lambda i: (seq_of_tile(i), 0, sc_chunk)),
                  pl.BlockSpec((None, TM, D), lambda i: (seq_of_tile(i), 0, sh_chunk)),
                  pl.BlockSpec((D, E), lambda i: (0, 0)),
                  pl.BlockSpec((1, E), lambda i: (0, 0))],
        out_specs=[tok(D // 2), tok(TOP_K), tok(TOP_K), tok(TOP_K), pl.BlockSpec((1, E), lambda i: (0, 0))],
        out_shape=[jax.ShapeDtypeStruct((M, D // 2), jnp.uint32), jax.ShapeDtypeStruct((M, TOP_K), jnp.int32),
                   jax.ShapeDtypeStruct((M, TOP_K), F32), jax.ShapeDtypeStruct((M, TOP_K), jnp.int32),
                   jax.ShapeDtypeStruct((1, E), F32)],
        scratch_shapes=[pltpu.VMEM((1, E), F32)],
        compiler_params=_cp(("arbitrary",), 40),
        name="norm_router",
    )(x, g, mr, mr, rw, rb)


def _row_copy(src_hbm, dst, src_row, dst_row, sem):
    return pltpu.make_async_copy(src_hbm.at[pl.ds(src_row, 1)], dst.at[pl.ds(dst_row, 1)], sem)


DISPATCH_TM = 512


def _dispatch_kernel(dest_ref, h_ref, xb_init_hbm, xb_hbm, sem):
    del xb_init_hbm
    i = pl.program_id(0)

    def issue(r, c):
        tok = i * DISPATCH_TM + r
        for k in range(TOP_K):
            _row_copy(h_ref, xb_hbm, r, dest_ref[tok * TOP_K + k], sem).start(priority=k % 2)
        return c

    lax.fori_loop(0, DISPATCH_TM, issue, 0, unroll=8)
    for k in range(TOP_K):
        pltpu.make_async_copy(h_ref, xb_hbm.at[pl.ds(0, DISPATCH_TM)], sem).wait()


def _dispatch(dest_flat, h2p, cap):
    M, Dp = h2p.shape
    any_spec = pl.BlockSpec(memory_space=pl.ANY)
    return pl.pallas_call(
        _dispatch_kernel,
        grid_spec=pltpu.PrefetchScalarGridSpec(
            num_scalar_prefetch=1, grid=(M // DISPATCH_TM,),
            in_specs=[pl.BlockSpec((DISPATCH_TM, Dp), lambda i, d: (i, 0)), any_spec], out_specs=any_spec,
            scratch_shapes=[pltpu.SemaphoreType.DMA(())]),
        out_shape=jax.ShapeDtypeStruct((cap, Dp), jnp.uint32),
        input_output_aliases={2: 0},
        compiler_params=_cp(("arbitrary",), 16),
        name="moe_dispatch",
    )(dest_flat, h2p, jnp.zeros((cap, Dp), jnp.uint32))


def _expert_weight_run(be_ref, nu_ref, nxt_ref, n_panels, copies, consume, slot_ref):
    n = pl.program_id(0)
    m = pl.program_id(1)
    e = be_ref[m]
    prev = be_ref[jnp.maximum(m - 1, 0)]

    @pl.when((m < nu_ref[0]) & ((m == 0) | (e != prev)))
    def _():
        @pl.when((n == 0) & (m == 0))
        def _():
            slot_ref[0] = 0
            for c in copies(0, e, n):
                c.start(priority=1)

        slot = slot_ref[0]
        for c in copies(slot, e, n):
            c.wait()
        consume(slot)
        ne = nxt_ref[m]

        @pl.when(ne >= 0)
        def _():
            for c in copies(1 - slot, ne, n):
                c.start(priority=1)

        @pl.when((ne < 0) & (n + 1 < n_panels))
        def _():
            for c in copies(1 - slot, be_ref[0], n + 1):
                c.start(priority=1)

        slot_ref[0] = 1 - slot


def _moe_up_kernel(be_ref, nu_ref, nxt_ref, x_ref, w_hbm, bg_ref, bl_ref, o_ref,
                   wbuf, wgb, wlb, slot_ref, sem, *, row0, tn, nf):
    m = pl.program_id(1)

    def copies(slot, e, n):
        return [pltpu.make_async_copy(w_hbm.at[row0 + e, :, pl.ds(pl.multiple_of((half * nf + n) * tn, tn), tn)],
                                      wbuf.at[slot, half], sem.at[slot, half]) for half in range(2)]

    def consume(slot):
        wgb[...] = wbuf[slot, 0].astype(BF16)
        wlb[...] = wbuf[slot, 1].astype(BF16)

    _expert_weight_run(be_ref, nu_ref, nxt_ref, nf, copies, consume, slot_ref)

    @pl.when(m < nu_ref[0])
    def _():
        x_lo, x_hi = _unpack_bf16_pairs(x_ref[...])
        n = x_lo.shape[1]
        hg = _dot(x_lo, wgb[0:n, :]) + _dot(x_hi, wgb[n:2 * n, :]) + bg_ref[...]
        hl = _dot(x_lo, wlb[0:n, :]) + _dot(x_hi, wlb[n:2 * n, :]) + bl_ref[...]
        hg = jnp.minimum(hg, SWIGLU_LIMIT)
        hl = jnp.clip(hl, -SWIGLU_LIMIT, SWIGLU_LIMIT)
        o_ref[...] = (hg * _sigmoid(SWIGLU_ALPHA * hg) * (hl + 1.0)).astype(o_ref.dtype)

    @pl.when(m >= nu_ref[0])
    def _():
        o_ref[...] = jnp.zeros(o_ref.shape, o_ref.dtype)


def _moe_up(blk_expert, n_used, nxt_expert, xb, w1, b1, l, E):
    cap, Dp = xb.shape
    D = w1.shape[1]
    F = w1.shape[2] // 2
    tn = 1024
    nf = F // tn
    n_blocks = cap // MOE_BM
    return pl.pallas_call(
        functools.partial(_moe_up_kernel, row0=l * E, tn=tn, nf=nf),
        grid_spec=pltpu.PrefetchScalarGridSpec(
            num_scalar_prefetch=3, grid=(nf, n_blocks),
            in_specs=[pl.BlockSpec((MOE_BM, Dp), lambda n, m, be, nu, nx: (m, 0)),
                      pl.BlockSpec(memory_space=pl.ANY),
                      pl.BlockSpec((None, 1, tn), lambda n, m, be, nu, nx: (l * E + be[m], 0, n)),
                      pl.BlockSpec((None, 1, tn), lambda n, m, be, nu, nx: (l * E + be[m], 0, nf + n))],
            out_specs=pl.BlockSpec((MOE_BM, tn), lambda n, m, be, nu, nx: (m, n)),
            scratch_shapes=[pltpu.VMEM((2, 2, D, tn), F32), pltpu.VMEM((D, tn), BF16), pltpu.VMEM((D, tn), BF16),
                            pltpu.SMEM((1,), jnp.int32), pltpu.SemaphoreType.DMA((2, 2))]),
        out_shape=jax.ShapeDtypeStruct((cap, F), BF16),
        compiler_params=_cp(("arbitrary", "arbitrary"), 56),
        name="moe_up",
    )(blk_expert, n_used, nxt_expert, xb, w1, b1, b1)


def _moe_down_kernel(be_ref, nu_ref, nxt_ref, a_ref, w_hbm, b_ref, o_ref, wbuf, wb, slot_ref, sem, *, row0):
    m = pl.program_id(1)

    def copies(slot, e, n):
        return [pltpu.make_async_copy(w_hbm.at[row0 + e], wbuf.at[slot], sem.at[slot])]

    def consume(slot):
        wb[...] = wbuf[slot].astype(BF16)

    _expert_weight_run(be_ref, nu_ref, nxt_ref, 1, copies, consume, slot_ref)

    @pl.when(m < nu_ref[0])
    def _():
        o_ref[...] = _dot(a_ref[...], wb[...]) + b_ref[...]

    @pl.when(m >= nu_ref[0])
    def _():
        o_ref[...] = jnp.zeros(o_ref.shape, o_ref.dtype)


def _moe_down(blk_expert, n_used, nxt_expert, act, w2, b2, l, E):
    cap, F = act.shape
    D = w2.shape[2]
    n_blocks = cap // MOE_BM
    return pl.pallas_call(
        functools.partial(_moe_down_kernel, row0=l * E),
        grid_spec=pltpu.PrefetchScalarGridSpec(
            num_scalar_prefetch=3, grid=(1, n_blocks),
            in_specs=[pl.BlockSpec((MOE_BM, F), lambda n, m, be, nu, nx: (m, 0)),
                      pl.BlockSpec(memory_space=pl.ANY),
                      pl.BlockSpec((None, 1, D), lambda n, m, be, nu, nx: (l * E + be[m], 0, 0))],
            out_specs=pl.BlockSpec((MOE_BM, D), lambda n, m, be, nu, nx: (m, 0)),
            scratch_shapes=[pltpu.VMEM((2, F, D), F32), pltpu.VMEM((F, D), BF16),
                            pltpu.SMEM((1,), jnp.int32), pltpu.SemaphoreType.DMA((2,))]),
        out_shape=jax.ShapeDtypeStruct((cap, D), F32),
        compiler_params=_cp(("arbitrary", "arbitrary"), 56),
        name="moe_down",
    )(blk_expert, n_used, nxt_expert, act, w2, b2)


COMBINE_TM = 128


def _combine_kernel(slot_ref, y_hbm, x_ref, g_ref, gate_ref, o_ref, buf, sem):
    i = pl.program_id(0)

    def gather(tile, s):
        def body(r, c):
            for k in range(TOP_K):
                row = slot_ref[(tile * COMBINE_TM + r) * TOP_K + k]
                _row_copy(y_hbm, buf.at[s, k], row, r, sem.at[s]).start(priority=k % 2)
            return c

        lax.fori_loop(0, COMBINE_TM, body, 0, unroll=8)

    @pl.when(i == 0)
    def _():
        gather(0, 0)

    @pl.when(i + 1 < pl.num_programs(0))
    def _():
        gather(i + 1, (i + 1) & 1)

    s = i & 1
    for k in range(TOP_K):
        pltpu.make_async_copy(y_hbm.at[pl.ds(0, COMBINE_TM)], buf.at[s, k], sem.at[s]).wait()
    y = gate_ref[:, 0:1] * buf[s, 0]
    for k in range(1, TOP_K):
        y = y + gate_ref[:, k:k + 1] * buf[s, k]
    o_ref[...] = x_ref[...] + g_ref[...] * y


def _combine(slot_of, yb, x, mr, gate_chunk, seq_of_tile, gate):
    M, D = x.shape
    per = TM // COMBINE_TM
    return pl.pallas_call(
        _combine_kernel,
        grid_spec=pltpu.PrefetchScalarGridSpec(
            num_scalar_prefetch=1, grid=(M // COMBINE_TM,),
            in_specs=[pl.BlockSpec(memory_space=pl.ANY),
                      pl.BlockSpec((COMBINE_TM, D), lambda i, s: (i, 0)),
                      pl.BlockSpec((None, COMBINE_TM, D), lambda i, s: (seq_of_tile(i // per), 0, gate_chunk)),
                      pl.BlockSpec((COMBINE_TM, TOP_K), lambda i, s: (i, 0))],
            out_specs=pl.BlockSpec((COMBINE_TM, D), lambda i, s: (i, 0)),
            scratch_shapes=[pltpu.VMEM((2, TOP_K, COMBINE_TM, D), F32), pltpu.SemaphoreType.DMA((2,))]),
        out_shape=jax.ShapeDtypeStruct((M, D), F32),
        compiler_params=_cp(("arbitrary",), 32),
        name="moe_combine",
    )(slot_of, yb, x, mr, gate)


def _slot_tables(top_idx, rank, counts, n_blocks):
    E = counts.shape[0]
    padded = ((counts + MOE_BM - 1) // MOE_BM) * MOE_BM
    pad_end = jnp.cumsum(padded)
    pad_start = pad_end - padded
    onehot = top_idx[..., None] == jnp.arange(E, dtype=jnp.int32)
    dest = jnp.sum(jnp.where(onehot, pad_start, 0), -1) + rank
    blk_start = jnp.arange(n_blocks, dtype=jnp.int32) * MOE_BM
    blk_expert = jnp.minimum(jnp.sum(blk_start[:, None] >= pad_end[None, :], -1), E - 1).astype(jnp.int32)
    n_used = (pad_end[-1] // MOE_BM).astype(jnp.int32).reshape(1)
    ids = jnp.arange(E, dtype=jnp.int32)
    later = (ids[None, :] > ids[:, None]) & (counts[None, :] > 0)
    nxt_of_expert = jnp.min(jnp.where(later, ids[None, :], E), axis=1)
    nxt_of_expert = jnp.where(nxt_of_expert == E, -1, nxt_of_expert)
    nxt_expert = jnp.sum(jnp.where(blk_expert[:, None] == ids[None, :], nxt_of_expert[None, :], 0), -1)
    return dest.reshape(-1).astype(jnp.int32), blk_expert, n_used, nxt_expert.astype(jnp.int32)


def _rope_tables(pos, hd):
    rot = hd // 4
    half = rot // 2
    inv = jnp.power(jnp.float32(ROPE_THETA), -jnp.arange(half, dtype=F32) * 2.0 / rot)
    ang = pos.astype(F32)[:, None] * inv[None, :]
    cos, sin = jnp.cos(ang), jnp.sin(ang)
    n = pos.shape[0]
    c = jnp.concatenate([cos, cos, jnp.ones((n, hd - rot), F32)], 1)
    sa = jnp.concatenate([-sin, jnp.zeros((n, hd - half), F32)], 1)
    sb = jnp.concatenate([jnp.zeros((n, half), F32), sin, jnp.zeros((n, hd - rot), F32)], 1)
    return c, sa, sb


def kernel(x_prompt, x_sample, cache_k, cache_v, state_conv, state_pool, page_table, c_prompt, c_sample,
           w_ada, b_ada, norm1_g, norm2_g, w_in, conv_w, conv_b, conv_ln_g, conv_ln_b, conv_out,
           pool_w, pool_scale, pool_out, q_norm_g, k_norm_g, attn_out, gmlp_ln_g, gmlp_ln_b,
           gmlp_ws, gmlp_b, gmlp_out, w_o, router_w, router_b, exp_w1, exp_b1, exp_w2, exp_b2):
    B, T, D = x_prompt.shape
    NB, TS, _ = x_sample.shape
    L = w_ada.shape[0]
    W = D // N_BRANCH
    HD = W // N_HEADS
    E = router_w.shape[2]
    n_pool, PAGE = cache_k.shape[1], cache_k.shape[2]
    n_pages = page_table.shape[1]
    past = n_pages * PAGE
    M_p, M_s = B * T, NB * TS
    M = M_p + M_s
    assert HD == LANE and W == N_HEADS * LANE and T % TM == 0 and TM % NB == 0 and NB % 8 == 0
    assert M % DISPATCH_TM == 0 and M_s % TM == 0 and TS <= 8 and past % MOBA_BLOCK == 0
    tiles_per_seq = T // TM

    def seq_of_tile(i):
        return jnp.minimum(i // tiles_per_seq, B)

    x = jnp.concatenate([x_prompt.reshape(M_p, D), x_sample.transpose(1, 0, 2).reshape(M_s, D)], 0)
    n_c = B + NB
    c_all = jnp.concatenate([c_prompt, c_sample, jnp.zeros((-n_c % 8, D), F32)], 0)
    mod_all = _ada(c_all, w_ada, b_ada)

    cos_p, sa_p, sb_p = _rope_tables(jnp.arange(T), HD)
    cos_s, sa_s, sb_s = [jnp.pad(t, ((0, 8 - TS), (0, 0))) for t in _rope_tables(past + jnp.arange(TS), HD)]
    pt_flat = page_table.reshape(-1).astype(jnp.int32)
    ck = cache_k.reshape(L, n_pool, PAGE * N_HEADS, HD)
    cv = cache_v.reshape(L, n_pool, PAGE * N_HEADS, HD)
    w1 = exp_w1.reshape(L * E, D, exp_w1.shape[3])
    b1 = exp_b1.reshape(L * E, 1, exp_b1.shape[2])
    w2 = exp_w2.reshape(L * E, exp_w2.shape[2], D)
    b2 = exp_b2.reshape(L * E, 1, D)

    def to_bt(a_t):
        return a_t.reshape(TS, NB, -1).transpose(1, 0, 2)

    outs = {k: [] for k in ('kp', 'vp', 'ks', 'vs', 'cp', 'cs', 'pp', 'ps', 'gp', 'gs')}
    for l in range(L):
        mod = mod_all[l]
        mr = jnp.concatenate([jnp.broadcast_to(mod[:B, None, :], (B, TM, 6 * D)),
                              jnp.tile(mod[B:B + NB], (TM // NB, 1))[None]], 0)
        h = _norm_mod(x, norm1_g[l][None], mr, 1, 0, seq_of_tile, BF16)
        tm_proj = next(t for t in (1088, 1024, 512, 256) if M % t == 0)
        proj = _panel_mm(h, w_in, l, tm_proj, 1024)

        row = lambda v: v[l][None]
        ya_p, conv_p = _conv_prompt(proj, B, T, W, conv_w[l], row(conv_b), row(conv_ln_g), row(conv_ln_b))
        yb_p, pool_p = _pool_prompt(proj, B, T, W, pool_w[l], row(pool_scale))
        q_p, k_p, v_p, kb_p, vb_p, kmean_p = _qkv_prompt(proj, B, T, W, row(q_norm_g), row(k_norm_g),
                                                        cos_p, sa_p, sb_p)
        yc_p = _attn_prompt(q_p, kb_p, vb_p, kmean_p, B, T, W)
        yd_p, gv_p = _gmlp_prompt(proj, B, T, W, row(gmlp_ln_g), row(gmlp_ln_b), gmlp_ws[l],
                                  jnp.transpose(gmlp_b[l]))

        proj_s = proj[M_p:, :8 * W].reshape(TS, NB, 8 * W)
        (ya_s, yb_s, yd_s, q_s, k_s, v_s, conv_s, pool_s, gv_s) = _sample_mix(
            proj_s, NB, TS, W, state_conv[l].transpose(1, 0, 2), state_pool[l].transpose(1, 0, 2),
            conv_w[l], row(conv_b), row(conv_ln_g), row(conv_ln_b), pool_w[l], row(pool_scale),
            row(q_norm_g), row(k_norm_g), cos_s, sa_s, sb_s, row(gmlp_ln_g), row(gmlp_ln_b),
            gmlp_ws[l], gmlp_b[l])
        q_hr = jnp.pad(to_bt(q_s).reshape(NB, TS, N_HEADS, HD).transpose(0, 2, 1, 3),
                       ((0, 0), (0, 0), (0, 8 - TS), (0, 0))).reshape(NB, N_HEADS * 8, HD)
        k_new = to_bt(k_s).reshape(NB, TS * N_HEADS, HD)
        v_new = to_bt(v_s).reshape(NB, TS * N_HEADS, HD)
        att_s = _attn_sample(pt_flat, q_hr, k_new, v_new, ck, cv, l, NB, TS, n_pages)
        yc_s = (att_s.reshape(NB, N_HEADS, 8, HD)[:, :, :TS].transpose(2, 0, 1, 3)
                .reshape(M_s, W).astype(BF16))

        merged = _merge((ya_p, yb_p, yc_p, yd_p), [s_.reshape(M_s, W) for s_ in (ya_s, yb_s, yc_s, yd_s)],
                        (conv_out, pool_out, attn_out, gmlp_out), proj, l, W, D)
        x = _wo_residual(merged, w_o, l, x, mr, 2, seq_of_tile)
        h2p, top_idx, gate, rank, counts = _norm_router(x, norm2_g[l][None], mr, 4, 3, seq_of_tile,
                                                         router_w[l], router_b[l][None])
        n_blocks = -(-M * TOP_K // MOE_BM) + E
        dest, blk_expert, n_used, nxt_expert = _slot_tables(top_idx, rank, counts[0].astype(jnp.int32), n_blocks)
        xb = _dispatch(dest, h2p, n_blocks * MOE_BM)
        act = _moe_up(blk_expert, n_used, nxt_expert, xb, w1, b1, l, E)
        yb = _moe_down(blk_expert, n_used, nxt_expert, act, w2, b2, l, E)
        x = _combine(dest, yb, x, mr, 5, seq_of_tile, gate)

        outs['kp'].append(k_p.reshape(B, T, N_HEADS, HD))
        outs['vp'].append(v_p.reshape(B, T, N_HEADS, HD))
        outs['ks'].append(to_bt(k_s).reshape(NB, TS, N_HEADS, HD))
        outs['vs'].append(to_bt(v_s).reshape(NB, TS, N_HEADS, HD))
        outs['cp'].append(conv_p)
        outs['cs'].append(conv_s.transpose(1, 0, 2))
        outs['pp'].append(pool_p)
        outs['ps'].append(pool_s.transpose(1, 0, 2))
        outs['gp'].append(gv_p)
        outs['gs'].append(to_bt(gv_s))

    y_prompt = x[:M_p].reshape(B, T, D)
    y_sample = to_bt(x[M_p:])
    st = lambda k: jnp.stack(outs[k])
    return (y_prompt, y_sample, st('kp'), st('vp'), st('ks'), st('vs'), st('cp'), st('cs'),
            st('pp'), st('ps'), st('gp'), st('gs'))
```

```python
import functools

import jax
import jax.numpy as jnp
from jax import lax
from jax.experimental import pallas as pl
from jax.experimental.pallas import tpu as pltpu

F32 = jnp.float32
BF16 = jnp.bfloat16

N_BRANCH = 4
CONV_W = 31
POOL_WINDOWS = (2, 4, 8, 16)
POOL_KEEP = max(POOL_WINDOWS) - 1
N_HEADS = 4
ROPE_THETA = 500000.0
MOBA_BLOCK = 256
MOBA_TOPK = 3
GMLP_CHUNK = 128
GMLP_GROUPS = 4
TOP_K = 4
SWIGLU_LIMIT = 7.0
SWIGLU_ALPHA = 1.702
EPS = 1e-6

LANE = 128
TM = 256
CONV_HIST = 32
POOL_HIST = 16
MOE_BM = 256
MIB = 1024 * 1024


def _cp(sem, vmem_mib):
    return pltpu.CompilerParams(dimension_semantics=sem, vmem_limit_bytes=vmem_mib * MIB)


def _sigmoid(x):
    return 1.0 / (1.0 + jnp.exp(-x))


def _layer_norm(x, g, b):
    mu = jnp.mean(x, -1, keepdims=True)
    xc = x - mu
    var = jnp.mean(xc * xc, -1, keepdims=True)
    return xc * lax.rsqrt(var + EPS) * g + b


def _dot(a, b):
    return jnp.dot(a, b, preferred_element_type=F32)


def _dot_nt(a, b, precision=None):
    return lax.dot_general(a, b, (((1,), (1,)), ((), ())), precision=precision,
                           preferred_element_type=F32)


def _ada_kernel(c_ref, w_ref, b_ref, o_ref):
    c = c_ref[...]
    s = (c * _sigmoid(c)).astype(BF16)
    o_ref[...] = _dot(s, w_ref[...].astype(BF16)) + b_ref[...]


def _ada(c_all, w_ada, b_ada):
    L, D, N = w_ada.shape
    R = c_all.shape[0]
    tn = 1024
    return pl.pallas_call(
        _ada_kernel,
        grid=(L, N // tn),
        in_specs=[pl.BlockSpec((R, D), lambda l, n: (0, 0)),
                  pl.BlockSpec((None, D, tn), lambda l, n: (l, 0, n)),
                  pl.BlockSpec((None, 1, tn), lambda l, n: (l, 0, n))],
        out_specs=pl.BlockSpec((None, R, tn), lambda l, n: (l, 0, n)),
        out_shape=jax.ShapeDtypeStruct((L, R, N), F32),
        compiler_params=_cp(("arbitrary", "arbitrary"), 40),
        name="ada",
    )(c_all, w_ada, b_ada.reshape(L, 1, N))


def _norm_mod_kernel(x_ref, g_ref, sc_ref, sh_ref, o_ref):
    x = x_ref[...]
    ms = jnp.mean(x * x, -1, keepdims=True)
    y = x * lax.rsqrt(ms + EPS) * g_ref[...]
    o_ref[...] = (y * (1.0 + sc_ref[...]) + sh_ref[...]).astype(o_ref.dtype)


def _norm_mod(x, g, mr, sc_chunk, sh_chunk, seq_of_tile, out_dtype):
    M, D = x.shape
    return pl.pallas_call(
        _norm_mod_kernel,
        grid=(M // TM,),
        in_specs=[pl.BlockSpec((TM, D), lambda i: (i, 0)),
                  pl.BlockSpec((1, D), lambda i: (0, 0)),
                  pl.BlockSpec((None, TM, D), lambda i: (seq_of_tile(i), 0, sc_chunk)),
                  pl.BlockSpec((None, TM, D), lambda i: (seq_of_tile(i), 0, sh_chunk))],
        out_specs=pl.BlockSpec((TM, D), lambda i: (i, 0)),
        out_shape=jax.ShapeDtypeStruct((M, D), out_dtype),
        compiler_params=_cp(("arbitrary",), 40),
        name="norm_mod",
    )(x, g, mr, mr)


def _panel_mm_kernel(a_ref, w_ref, o_ref, wbf_ref):
    @pl.when(pl.program_id(1) == 0)
    def _():
        wbf_ref[...] = w_ref[...].astype(BF16)

    o_ref[...] = _dot(a_ref[...], wbf_ref[...])


def _panel_mm(a, w, l, tm, tn):
    M, K = a.shape
    N = w.shape[2]
    return pl.pallas_call(
        _panel_mm_kernel,
        grid=(N // tn, M // tm),
        in_specs=[pl.BlockSpec((tm, K), lambda n, m: (m, 0)),
                  pl.BlockSpec((None, K, tn), lambda n, m: (l, 0, n))],
        out_specs=pl.BlockSpec((tm, tn), lambda n, m: (m, n)),
        out_shape=jax.ShapeDtypeStruct((M, N), F32),
        scratch_shapes=[pltpu.VMEM((K, tn), BF16)],
        compiler_params=_cp(("arbitrary", "arbitrary"), 48),
        name="in_proj",
    )(a, w)


def _conv_p_kernel(a_ref, g_ref, w_ref, cb_ref, lg_ref, lb_ref, y_ref, cn_ref, zbuf, ybuf, ubuf):
    i = pl.program_id(1)
    W = a_ref.shape[1]

    @pl.when(i == 0)
    def _():
        zbuf[0:CONV_HIST, :] = jnp.zeros((CONV_HIST, W), F32)

    @pl.when(i > 0)
    def _():
        zbuf[0:CONV_HIST, :] = zbuf[TM:TM + CONV_HIST, :]

    zbuf[CONV_HIST:CONV_HIST + TM, :] = a_ref[...] * _sigmoid(g_ref[...])
    base = CONV_HIST - (CONV_W - 1)
    rb = 128
    for c0 in range(0, W, LANE):
        for r0 in range(0, TM, rb):
            for b in range(8):
                rows = rb + (8 if b else 0)
                u = None
                for j in range(CONV_W):
                    if (base + j) % 8 == b:
                        off = r0 + base + j - b
                        term = w_ref[j:j + 1, c0:c0 + LANE] * zbuf[off:off + rows, c0:c0 + LANE]
                        u = term if u is None else u + term
                ubuf[b, 0:rows, :] = u
            acc = jnp.broadcast_to(cb_ref[:, c0:c0 + LANE], (rb, LANE))
            for b in range(8):
                acc = acc + ubuf[b, pl.ds(b, rb), :]
            ybuf[r0:r0 + rb, c0:c0 + LANE] = acc
    y = _layer_norm(ybuf[...], lg_ref[...], lb_ref[...])
    y_ref[...] = (y * _sigmoid(y)).astype(y_ref.dtype)

    @pl.when(i == pl.num_programs(1) - 1)
    def _():
        cn_ref[...] = zbuf[CONV_HIST + TM - (CONV_W - 1):CONV_HIST + TM, :]


def _conv_prompt(proj, B, T, W, cw, cb, lg, lb):
    nt = T // TM
    return pl.pallas_call(
        _conv_p_kernel,
        grid=(B, nt),
        in_specs=[pl.BlockSpec((TM, W), lambda b, i: (b * nt + i, 0)),
                  pl.BlockSpec((TM, W), lambda b, i: (b * nt + i, 1)),
                  pl.BlockSpec((CONV_W, W), lambda b, i: (0, 0)),
                  pl.BlockSpec((1, W), lambda b, i: (0, 0)),
                  pl.BlockSpec((1, W), lambda b, i: (0, 0)),
                  pl.BlockSpec((1, W), lambda b, i: (0, 0))],
        out_specs=[pl.BlockSpec((TM, W), lambda b, i: (b * nt + i, 0)),
                   pl.BlockSpec((None, CONV_W - 1, W), lambda b, i: (b, 0, 0))],
        out_shape=[jax.ShapeDtypeStruct((B * T, W), BF16),
                   jax.ShapeDtypeStruct((B, CONV_W - 1, W), F32)],
        scratch_shapes=[pltpu.VMEM((CONV_HIST + TM, W), F32), pltpu.VMEM((TM, W), F32),
                        pltpu.VMEM((8, 128 + 8, LANE), F32)],
        compiler_params=_cp(("arbitrary", "arbitrary"), 32),
        name="conv_prompt",
    )(proj, proj, cw, cb, lg, lb)


def _pool_p_kernel(p_ref, pw_ref, ps_ref, y_ref, pn_ref, ebuf):
    i = pl.program_id(1)
    W = p_ref.shape[1]

    @pl.when(i == 0)
    def _():
        ebuf[0:POOL_HIST, :] = jnp.zeros((POOL_HIST, W), F32)

    @pl.when(i > 0)
    def _():
        ebuf[0:POOL_HIST, :] = ebuf[TM:TM + POOL_HIST, :]

    ebuf[POOL_HIST:POOL_HIST + TM, :] = p_ref[...]
    t_abs = i * TM + lax.broadcasted_iota(jnp.int32, (TM, 1), 0)
    for gi, win in enumerate(POOL_WINDOWS):
        c0 = gi * LANE
        s = ebuf[POOL_HIST:POOL_HIST + TM, c0:c0 + LANE]
        for k in range(1, win):
            s = s + ebuf[pl.ds(POOL_HIST - k, TM), c0:c0 + LANE]
        cnt = jnp.minimum(t_abs + 1, win).astype(F32)
        d = s / cnt - p_ref[:, c0:c0 + LANE]
        yg = _dot(d.astype(BF16), pw_ref[gi].astype(BF16)) * ps_ref[:, c0:c0 + LANE]
        y_ref[:, c0:c0 + LANE] = yg.astype(y_ref.dtype)

    @pl.when(i == pl.num_programs(1) - 1)
    def _():
        pn_ref[...] = ebuf[POOL_HIST + TM - POOL_KEEP:POOL_HIST + TM, :]


def _pool_prompt(proj, B, T, W, pw, ps):
    nt = T // TM
    G = len(POOL_WINDOWS)
    return pl.pallas_call(
        _pool_p_kernel,
        grid=(B, nt),
        in_specs=[pl.BlockSpec((TM, W), lambda b, i: (b * nt + i, 2)),
                  pl.BlockSpec((G, LANE, LANE), lambda b, i: (0, 0, 0)),
                  pl.BlockSpec((1, W), lambda b, i: (0, 0))],
        out_specs=[pl.BlockSpec((TM, W), lambda b, i: (b * nt + i, 0)),
                   pl.BlockSpec((None, POOL_KEEP, W), lambda b, i: (b, 0, 0))],
        out_shape=[jax.ShapeDtypeStruct((B * T, W), BF16),
                   jax.ShapeDtypeStruct((B, POOL_KEEP, W), F32)],
        scratch_shapes=[pltpu.VMEM((POOL_HIST + TM, W), F32)],
        compiler_params=_cp(("arbitrary", "arbitrary"), 32),
        name="pool_prompt",
    )(proj, pw, ps)


def _norm_rope_head(xh, g, cos, sa, sb):
    ms = jnp.mean(xh * xh, -1, keepdims=True)
    xn = xh * lax.rsqrt(ms + EPS) * g
    hd = xh.shape[1]
    half = hd // 8
    up = pltpu.roll(xn, hd - half, axis=1)
    dn = pltpu.roll(xn, half, axis=1)
    return xn * cos + up * sa + dn * sb


def _qkv_p_kernel(q_ref, k_ref, v_ref, qg_ref, kg_ref, cos_ref, sa_ref, sb_ref,
                  qo_ref, ko_ref, vo_ref, kb_ref, vb_ref, km_ref):
    i = pl.program_id(1)
    cos, sa, sb = cos_ref[...], sa_ref[...], sb_ref[...]

    @pl.when(i == 0)
    def _():
        km_ref[...] = jnp.zeros(km_ref.shape, F32)

    blk_row = lax.broadcasted_iota(jnp.int32, (km_ref.shape[0], LANE), 0)
    for h in range(N_HEADS):
        c0 = h * LANE
        qo_ref[:, c0:c0 + LANE] = _norm_rope_head(q_ref[:, c0:c0 + LANE], qg_ref[...], cos, sa, sb)
        kh = _norm_rope_head(k_ref[:, c0:c0 + LANE], kg_ref[...], cos, sa, sb)
        ko_ref[:, c0:c0 + LANE] = kh
        kb_ref[:, c0:c0 + LANE] = kh.astype(BF16)
        km_ref[:, c0:c0 + LANE] = jnp.where(blk_row == i, jnp.mean(kh, axis=0, keepdims=True),
                                            km_ref[:, c0:c0 + LANE])
    v = v_ref[...]
    vo_ref[...] = v
    vb_ref[...] = v.astype(BF16)


def _qkv_prompt(proj, B, T, W, qg, kg, cos, sa, sb):
    nt = T // TM
    row = lambda b, i: (b * nt + i, 0)
    return pl.pallas_call(
        _qkv_p_kernel,
        grid=(B, nt),
        in_specs=[pl.BlockSpec((TM, W), lambda b, i: (b * nt + i, 3)),
                  pl.BlockSpec((TM, W), lambda b, i: (b * nt + i, 4)),
                  pl.BlockSpec((TM, W), lambda b, i: (b * nt + i, 5)),
                  pl.BlockSpec((1, LANE), lambda b, i: (0, 0)),
                  pl.BlockSpec((1, LANE), lambda b, i: (0, 0)),
                  pl.BlockSpec((TM, LANE), lambda b, i: (i, 0)),
                  pl.BlockSpec((TM, LANE), lambda b, i: (i, 0)),
                  pl.BlockSpec((TM, LANE), lambda b, i: (i, 0))],
        out_specs=[pl.BlockSpec((TM, W), row), pl.BlockSpec((TM, W), row), pl.BlockSpec((TM, W), row),
                   pl.BlockSpec((TM, W), row), pl.BlockSpec((TM, W), row),
                   pl.BlockSpec((nt, W), lambda b, i: (b, 0))],
        out_shape=[jax.ShapeDtypeStruct((B * T, W), F32), jax.ShapeDtypeStruct((B * T, W), F32),
                   jax.ShapeDtypeStruct((B * T, W), F32), jax.ShapeDtypeStruct((B * T, W), BF16),
                   jax.ShapeDtypeStruct((B * T, W), BF16), jax.ShapeDtypeStruct((B * nt, W), F32)],
        compiler_params=_cp(("arbitrary", "arbitrary"), 32),
        name="qkv_prompt",
    )(proj, proj, proj, qg, kg, cos, sa, sb)


def _topk_block_cols(bs, past):
    nb = bs.shape[1]
    jidx = lax.broadcasted_iota(jnp.int32, bs.shape, 1)
    cols = []
    for n in range(nb):
        bn = bs[:, n:n + 1]
        beats = ((bs > bn) | ((bs == bn) & (jidx < n))) & past
        rank = jnp.sum(beats.astype(F32), axis=-1, keepdims=True)
        cols.append(rank < (MOBA_TOPK - 0.5))
    return cols


def _attn_p_kernel(q_ref, k_ref, v_ref, km_ref, o_ref):
    i = pl.program_id(1)
    nb = km_ref.shape[0]
    scale = LANE ** -0.5
    row = lax.broadcasted_iota(jnp.int32, (TM, MOBA_BLOCK), 0)
    col = lax.broadcasted_iota(jnp.int32, (TM, MOBA_BLOCK), 1)
    tri = col <= row
    blk = lax.broadcasted_iota(jnp.int32, (TM, nb), 1)

    def attend(own):
        nk = (own + 1) * MOBA_BLOCK
        for h in range(N_HEADS):
            hs = slice(h * LANE, (h + 1) * LANE)
            q = q_ref[:, hs]
            s = _dot_nt(q.astype(BF16), k_ref[0:nk, hs]) * scale
            pieces = []
            if own > 0:
                bs = _dot_nt(q, km_ref[:, hs], precision=lax.Precision.HIGHEST)
                sel = _topk_block_cols(bs, blk < own)
                pieces = [jnp.where(sel[j], s[:, j * MOBA_BLOCK:(j + 1) * MOBA_BLOCK], -jnp.inf)
                          for j in range(own)]
            pieces.append(jnp.where(tri, s[:, own * MOBA_BLOCK:nk], -jnp.inf))
            s = jnp.concatenate(pieces, axis=-1) if own > 0 else pieces[0]
            m = jnp.max(s, -1, keepdims=True)
            p = jnp.exp(s - m)
            den = jnp.sum(p, -1, keepdims=True)
            o_ref[:, hs] = (_dot(p.astype(BF16), v_ref[0:nk, hs]) / den).astype(o_ref.dtype)

    for own in range(nb):
        pl.when(i == own)(functools.partial(attend, own))


def _attn_prompt(q, kb, vb, kmean, B, T, W):
    nt = T // TM
    assert TM == MOBA_BLOCK
    return pl.pallas_call(
        _attn_p_kernel,
        grid=(B, nt),
        in_specs=[pl.BlockSpec((TM, W), lambda b, i: (b * nt + i, 0)),
                  pl.BlockSpec((T, W), lambda b, i: (b, 0)),
                  pl.BlockSpec((T, W), lambda b, i: (b, 0)),
                  pl.BlockSpec((nt, W), lambda b, i: (b, 0))],
        out_specs=pl.BlockSpec((TM, W), lambda b, i: (b * nt + i, 0)),
        out_shape=jax.ShapeDtypeStruct((B * T, W), BF16),
        compiler_params=_cp(("arbitrary", "arbitrary"), 40),
        name="attn_prompt",
    )(q, kb, vb, kmean)


def _gmlp_p_kernel(u_ref, v_ref, lg_ref, lb_ref, ws_ref, bt_ref, y_ref, gv_ref, vbuf):
    i = pl.program_id(1)
    vbuf[...] = _layer_norm(v_ref[...], lg_ref[...], lb_ref[...])
    row = lax.broadcasted_iota(jnp.int32, (GMLP_CHUNK, GMLP_CHUNK), 0)
    col = lax.broadcasted_iota(jnp.int32, (GMLP_CHUNK, GMLP_CHUNK), 1)
    tri = col <= row
    for g in range(GMLP_GROUPS):
        c0 = g * LANE
        w = jnp.where(tri, ws_ref[g], 0.0).astype(BF16)
        for r0 in range(0, TM, GMLP_CHUNK):
            mixed = _dot(w, vbuf[r0:r0 + GMLP_CHUNK, c0:c0 + LANE].astype(BF16)) + bt_ref[:, g:g + 1]
            y_ref[r0:r0 + GMLP_CHUNK, c0:c0 + LANE] = (
                u_ref[r0:r0 + GMLP_CHUNK, c0:c0 + LANE] * mixed).astype(y_ref.dtype)

    @pl.when(i == pl.num_programs(1) - 1)
    def _():
        gv_ref[...] = vbuf[TM - GMLP_CHUNK:TM, :]


def _gmlp_prompt(proj, B, T, W, lg, lb, ws, bt):
    nt = T // TM
    return pl.pallas_call(
        _gmlp_p_kernel,
        grid=(B, nt),
        in_specs=[pl.BlockSpec((TM, W), lambda b, i: (b * nt + i, 6)),
                  pl.BlockSpec((TM, W), lambda b, i: (b * nt + i, 7)),
                  pl.BlockSpec((1, W), lambda b, i: (0, 0)),
                  pl.BlockSpec((1, W), lambda b, i: (0, 0)),
                  pl.BlockSpec((GMLP_GROUPS, GMLP_CHUNK, GMLP_CHUNK), lambda b, i: (0, 0, 0)),
                  pl.BlockSpec((GMLP_CHUNK, GMLP_GROUPS), lambda b, i: (0, 0))],
        out_specs=[pl.BlockSpec((TM, W), lambda b, i: (b * nt + i, 0)),
                   pl.BlockSpec((None, GMLP_CHUNK, W), lambda b, i: (b, 0, 0))],
        out_shape=[jax.ShapeDtypeStruct((B * T, W), BF16),
                   jax.ShapeDtypeStruct((B, GMLP_CHUNK, W), F32)],
        scratch_shapes=[pltpu.VMEM((TM, W), F32)],
        compiler_params=_cp(("arbitrary", "arbitrary"), 32),
        name="gmlp_prompt",
    )(proj, proj, lg, lb, ws, bt)


def _sample_mix_kernel(p_ref, sc_ref, sp_ref, cw_ref, cb_ref, clg_ref, clb_ref, pw_ref, ps_ref,
                       qg_ref, kg_ref, cos_ref, sa_ref, sb_ref, glg_ref, glb_ref, ws_ref, gb_ref,
                       ya_ref, yb_ref, yd_ref, q_ref, k_ref, v_ref, cn_ref, pn_ref, gv_ref):
    ts, sb_rows, _ = p_ref.shape
    W = ya_ref.shape[2]
    n_conv = CONV_W - 1

    def col(c):
        return slice(c * W, (c + 1) * W)

    z = [p_ref[t, :, col(0)] * _sigmoid(p_ref[t, :, col(1)]) for t in range(ts)]

    def zext(r):
        return sc_ref[r] if r < n_conv else z[r - n_conv]

    for t in range(ts):
        acc = jnp.broadcast_to(cb_ref[...], (sb_rows, W))
        for j in range(CONV_W):
            acc = acc + cw_ref[j:j + 1, :] * zext(t + j)
        y = _layer_norm(acc, clg_ref[...], clb_ref[...])
        ya_ref[t] = (y * _sigmoid(y)).astype(ya_ref.dtype)
    for r in range(n_conv):
        cn_ref[r] = zext(r + ts)

    def pext(r, c0):
        if r < POOL_KEEP:
            return sp_ref[r, :, c0:c0 + LANE]
        return p_ref[r - POOL_KEEP, :, 2 * W + c0:2 * W + c0 + LANE]

    for t in range(ts):
        for gi, win in enumerate(POOL_WINDOWS):
            c0 = gi * LANE
            s = pext(POOL_KEEP + t, c0)
            for k in range(1, win):
                s = s + pext(POOL_KEEP + t - k, c0)
            d = s / float(win) - pext(POOL_KEEP + t, c0)
            yg = _dot(d.astype(BF16), pw_ref[gi].astype(BF16)) * ps_ref[:, c0:c0 + LANE]
            yb_ref[t, :, c0:c0 + LANE] = yg.astype(yb_ref.dtype)
    for r in range(POOL_KEEP):
        pn_ref[r] = sp_ref[r + ts] if r + ts < POOL_KEEP else p_ref[r + ts - POOL_KEEP, :, col(2)]

    for t in range(ts):
        cos = cos_ref[t:t + 1, :]
        sa = sa_ref[t:t + 1, :]
        sb = sb_ref[t:t + 1, :]
        for h in range(N_HEADS):
            c0 = h * LANE
            q_ref[t, :, c0:c0 + LANE] = _norm_rope_head(
                p_ref[t, :, 3 * W + c0:3 * W + c0 + LANE], qg_ref[...], cos, sa, sb)
            k_ref[t, :, c0:c0 + LANE] = _norm_rope_head(
                p_ref[t, :, 4 * W + c0:4 * W + c0 + LANE], kg_ref[...], cos, sa, sb)
        v_ref[t] = p_ref[t, :, col(5)]

    vn = [_layer_norm(p_ref[t, :, col(7)], glg_ref[...], glb_ref[...]) for t in range(ts)]
    for t in range(ts):
        gv_ref[t] = vn[t]
        for g in range(GMLP_GROUPS):
            c0 = g * LANE
            mixed = jnp.broadcast_to(gb_ref[g:g + 1, t:t + 1], (sb_rows, LANE))
            for s_ in range(t + 1):
                mixed = mixed + ws_ref[g, t:t + 1, s_:s_ + 1] * vn[s_][:, c0:c0 + LANE]
            yd_ref[t, :, c0:c0 + LANE] = (
                p_ref[t, :, 6 * W + c0:6 * W + c0 + LANE] * mixed).astype(yd_ref.dtype)


SAMPLE_SEQ_BLOCK = 32


def _sample_mix(proj_s, nb, ts, W, sc_t, sp_t, cw, cb, clg, clb, pw, ps, qg, kg, cos, sa, sb,
                glg, glb, ws, gb):
    G = len(POOL_WINDOWS)
    sblk = SAMPLE_SEQ_BLOCK
    full2 = lambda shape: pl.BlockSpec(shape, lambda i: (0, 0))
    full3 = lambda shape: pl.BlockSpec(shape, lambda i: (0, 0, 0))
    seq3 = lambda rows, c: pl.BlockSpec((rows, sblk, c), lambda i: (0, i, 0))
    tok = lambda dt: jax.ShapeDtypeStruct((ts, nb, W), dt)
    return pl.pallas_call(
        _sample_mix_kernel,
        grid=(nb // sblk,),
        in_specs=[seq3(ts, 8 * W), seq3(CONV_W - 1, W), seq3(POOL_KEEP, W),
                  full2((CONV_W, W)), full2((1, W)), full2((1, W)), full2((1, W)),
                  full3((G, LANE, LANE)), full2((1, W)),
                  full2((1, LANE)), full2((1, LANE)),
                  full2((8, LANE)), full2((8, LANE)), full2((8, LANE)),
                  full2((1, W)), full2((1, W)),
                  full3((GMLP_GROUPS, GMLP_CHUNK, GMLP_CHUNK)), full2((GMLP_GROUPS, GMLP_CHUNK))],
        out_specs=[seq3(ts, W)] * 6 + [seq3(CONV_W - 1, W), seq3(POOL_KEEP, W), seq3(ts, W)],
        out_shape=[tok(BF16), tok(BF16), tok(BF16), tok(F32), tok(F32), tok(F32),
                   jax.ShapeDtypeStruct((CONV_W - 1, nb, W), F32),
                   jax.ShapeDtypeStruct((POOL_KEEP, nb, W), F32), tok(F32)],
        compiler_params=_cp(("arbitrary",), 40),
        name="sample_mix",
    )(proj_s, sc_t, sp_t, cw, cb, clg, clb, pw, ps, qg, kg, cos, sa, sb, glg, glb, ws, gb)


def _attn_s_kernel(pt_ref, q_ref, kn_ref, vn_ref, *refs, n_pages, ts):
    kp = refs[:n_pages]
    vp = refs[n_pages:2 * n_pages]
    o_ref = refs[2 * n_pages]
    kbuf, vbuf = refs[2 * n_pages + 1:]
    nh = N_HEADS
    prow = kp[0].shape[0]
    R, hd = q_ref.shape
    blk_rows = MOBA_BLOCK * nh
    ppb = blk_rows // prow
    nblk = n_pages // ppb
    scale = hd ** -0.5
    q = q_ref[...]
    rhead = lax.broadcasted_iota(jnp.int32, (R, 1), 0) >> 3
    t_row = lax.broadcasted_iota(jnp.int32, (R, 1), 0) & 7
    blkid = lax.broadcasted_iota(jnp.int32, (R, nblk), 1)
    bs = jnp.zeros((R, nblk), F32)
    for j in range(nblk):
        acc8 = None
        for u in range(ppb):
            pg = j * ppb + u
            kpg = kp[pg][...]
            kbuf[pg * prow:(pg + 1) * prow, :] = kpg.astype(BF16)
            vbuf[pg * prow:(pg + 1) * prow, :] = vp[pg][...].astype(BF16)
            part = jnp.sum(kpg.reshape(prow // 8, 8, hd), axis=0)
            acc8 = part if acc8 is None else acc8 + part
        kmean = (acc8[0:nh] + acc8[nh:2 * nh]) * (1.0 / MOBA_BLOCK)
        km_rows = jnp.concatenate([jnp.broadcast_to(kmean[h:h + 1], (8, hd)) for h in range(nh)], axis=0)
        bs = jnp.where(blkid == j, jnp.sum(q * km_rows, -1, keepdims=True), bs)
    sel = _topk_block_cols(bs, blkid >= 0)
    qb = q.astype(BF16)
    s_all = _dot_nt(qb, kbuf[...]) * scale
    own_head = (lax.broadcasted_iota(jnp.int32, (1, blk_rows), 1) & (nh - 1)) == rhead
    s_past = jnp.concatenate(
        [jnp.where(own_head & sel[j], s_all[:, j * blk_rows:(j + 1) * blk_rows], -jnp.inf)
         for j in range(nblk)], axis=-1)
    nc = kn_ref.shape[0]
    ocol = lax.broadcasted_iota(jnp.int32, (1, nc), 1)
    s_own = _dot_nt(qb, kn_ref[...].astype(BF16)) * scale
    s_own = jnp.where(((ocol & (nh - 1)) == rhead) & ((ocol >> 2) <= t_row), s_own, -jnp.inf)
    m = jnp.maximum(jnp.max(s_past, -1, keepdims=True), jnp.max(s_own, -1, keepdims=True))
    p_past = jnp.exp(s_past - m)
    p_own = jnp.exp(s_own - m)
    den = jnp.sum(p_past, -1, keepdims=True) + jnp.sum(p_own, -1, keepdims=True)
    acc = _dot(p_past.astype(BF16), vbuf[...]) + _dot(p_own.astype(BF16), vn_ref[...].astype(BF16))
    o_ref[...] = acc / den


def _attn_sample(pt_flat, q_hr, k_new, v_new, ck, cv, l, nb, ts, n_pages):
    prow, hd = ck.shape[2:]
    R = q_hr.shape[1]
    nc = k_new.shape[1]
    assert N_HEADS == 4 and nc % 8 == 0

    def page_spec(j):
        return pl.BlockSpec((None, None, prow, hd), lambda b, pt: (l, pt[b * n_pages + j], 0, 0))

    in_specs = ([pl.BlockSpec((None, R, hd), lambda b, pt: (b, 0, 0)),
                 pl.BlockSpec((None, nc, hd), lambda b, pt: (b, 0, 0)),
                 pl.BlockSpec((None, nc, hd), lambda b, pt: (b, 0, 0))]
                + [page_spec(j) for j in range(n_pages)] + [page_spec(j) for j in range(n_pages)])
    return pl.pallas_call(
        functools.partial(_attn_s_kernel, n_pages=n_pages, ts=ts),
        grid_spec=pltpu.PrefetchScalarGridSpec(
            num_scalar_prefetch=1, grid=(nb,), in_specs=in_specs,
            out_specs=pl.BlockSpec((None, R, hd), lambda b, pt: (b, 0, 0)),
            scratch_shapes=[pltpu.VMEM((n_pages * prow, hd), BF16), pltpu.VMEM((n_pages * prow, hd), BF16)]),
        out_shape=jax.ShapeDtypeStruct((nb, R, hd), F32),
        compiler_params=_cp(("arbitrary",), 48),
        name="attn_sample",
    )(pt_flat, q_hr, k_new, v_new, *([ck] * n_pages), *([cv] * n_pages))


def _merge_kernel(*refs, n_prompt_tiles):
    yp_refs, ys_refs = refs[0:4], refs[4:8]
    w_refs, g_refs = refs[8:12], refs[12:16]
    o_ref, wbf_ref = refs[16], refs[17]
    m = pl.program_id(1)

    @pl.when(m == 0)
    def _():
        for b in range(N_BRANCH):
            wbf_ref[b] = w_refs[b][...].astype(BF16)

    acc = None
    for b in range(N_BRANCH):
        y = jnp.where(m < n_prompt_tiles, yp_refs[b][...], ys_refs[b][...])
        term = _sigmoid(g_refs[b][...]) * _dot(y, wbf_ref[b])
        acc = term if acc is None else acc + term
    o_ref[...] = acc.astype(o_ref.dtype)


def _merge(ys_prompt, ys_sample, ws, proj, l, W, D):
    M = proj.shape[0]
    tm, tn = 512, 1024
    npb = D // tn
    gate0 = 8 * W // tn
    npt = ys_prompt[0].shape[0] // tm
    assert ys_prompt[0].shape[0] % tm == 0 and ys_sample[0].shape[0] % tm == 0
    yp_spec = pl.BlockSpec((tm, W), lambda n, m: (jnp.minimum(m, npt - 1), 0))
    ys_spec = pl.BlockSpec((tm, W), lambda n, m: (jnp.maximum(m - npt, 0), 0))
    w_spec = pl.BlockSpec((None, W, tn), lambda n, m: (l, 0, n))
    g_specs = [pl.BlockSpec((tm, tn), functools.partial(lambda n, m, b: (m, gate0 + b * npb + n), b=b))
               for b in range(N_BRANCH)]
    return pl.pallas_call(
        functools.partial(_merge_kernel, n_prompt_tiles=npt),
        grid=(npb, M // tm),
        in_specs=[yp_spec] * 4 + [ys_spec] * 4 + [w_spec] * 4 + g_specs,
        out_specs=pl.BlockSpec((tm, tn), lambda n, m: (m, n)),
        out_shape=jax.ShapeDtypeStruct((M, D), BF16),
        scratch_shapes=[pltpu.VMEM((N_BRANCH, W, tn), BF16)],
        compiler_params=_cp(("arbitrary", "arbitrary"), 48),
        name="merge",
    )(*ys_prompt, *ys_sample, *ws, proj, proj, proj, proj)


def _wo_kernel(a_ref, w_ref, x_ref, g_ref, o_ref, wbf_ref):
    @pl.when(pl.program_id(1) == 0)
    def _():
        wbf_ref[...] = w_ref[...].astype(BF16)

    o_ref[...] = x_ref[...] + g_ref[...] * _dot(a_ref[...], wbf_ref[...])


def _wo_residual(merged, w_o, l, x, mr, gate_chunk, seq_of_tile):
    M, D = x.shape
    tn = 1024
    ncb = D // tn
    return pl.pallas_call(
        _wo_kernel,
        grid=(ncb, M // TM),
        in_specs=[pl.BlockSpec((TM, D), lambda n, m: (m, 0)),
                  pl.BlockSpec((None, D, tn), lambda n, m: (l, 0, n)),
                  pl.BlockSpec((TM, tn), lambda n, m: (m, n)),
                  pl.BlockSpec((None, TM, tn), lambda n, m: (seq_of_tile(m), 0, gate_chunk * ncb + n))],
        out_specs=pl.BlockSpec((TM, tn), lambda n, m: (m, n)),
        out_shape=jax.ShapeDtypeStruct((M, D), F32),
        scratch_shapes=[pltpu.VMEM((D, tn), BF16)],
        compiler_params=_cp(("arbitrary", "arbitrary"), 40),
        name="wo_residual",
    )(merged, w_o, x, mr)


def _pack_bf16_pairs(x):
    n = x.shape[1] // 2
    lo = lax.bitcast_convert_type(x[:, :n].astype(BF16).astype(F32), jnp.uint32)
    hi = lax.bitcast_convert_type(x[:, n:].astype(BF16).astype(F32), jnp.uint32)
    return hi | (lo >> 16)


def _unpack_bf16_pairs(w):
    lo = lax.bitcast_convert_type(w << 16, F32).astype(BF16)
    hi = lax.bitcast_convert_type(w & jnp.uint32(0xFFFF0000), F32).astype(BF16)
    return lo, hi


def _norm_router_kernel(x_ref, g_ref, sc_ref, sh_ref, rw_ref, rb_ref,
                        h_ref, idx_ref, gate_ref, rank_ref, cnt_ref, run_ref):
    i = pl.program_id(0)
    E = rw_ref.shape[1]

    @pl.when(i == 0)
    def _():
        run_ref[...] = jnp.zeros(run_ref.shape, F32)

    x = x_ref[...]
    ms = jnp.mean(x * x, -1, keepdims=True)
    y = x * lax.rsqrt(ms + EPS) * g_ref[...]
    h = y * (1.0 + sc_ref[...]) + sh_ref[...]
    h_ref[...] = _pack_bf16_pairs(h)
    logits = jnp.dot(h, rw_ref[...], precision=lax.Precision.HIGHEST,
                     preferred_element_type=F32) + rb_ref[...]

    eidx = lax.broadcasted_iota(jnp.int32, (TM, E), 1).astype(F32)
    kcol = lax.broadcasted_iota(jnp.int32, (TM, TOP_K), 1)
    r_i = lax.broadcasted_iota(jnp.int32, (TM, TM), 0)
    c_i = lax.broadcasted_iota(jnp.int32, (TM, TM), 1)
    before = jnp.where(c_i < r_i, 1.0, 0.0).astype(BF16)
    work = logits
    vals, hots = [], []
    idx_out = jnp.zeros((TM, TOP_K), F32)
    rank_out = jnp.zeros((TM, TOP_K), F32)
    run = run_ref[...]
    for k in range(TOP_K):
        mx = jnp.max(work, -1, keepdims=True)
        am = jnp.min(jnp.where(work == mx, eidx, float(E)), -1, keepdims=True)
        hot = eidx == am
        work = jnp.where(hot, -jnp.inf, work)
        hot_f = jnp.where(hot, 1.0, 0.0)
        earlier = _dot(before, hot_f.astype(BF16))
        rank = jnp.sum(hot_f * (earlier + run), -1, keepdims=True)
        run = run + jnp.sum(hot_f, axis=0, keepdims=True)
        vals.append(mx)
        idx_out = jnp.where(kcol == k, am, idx_out)
        rank_out = jnp.where(kcol == k, rank, rank_out)
    run_ref[...] = run
    cnt_ref[...] = run
    den = jnp.zeros((TM, 1), F32)
    gate_out = jnp.zeros((TM, TOP_K), F32)
    ex = [jnp.exp(v - vals[0]) for v in vals]
    for e_ in ex:
        den = den + e_
    for k in range(TOP_K):
        gate_out = jnp.where(kcol == k, ex[k] / den, gate_out)
    idx_ref[...] = idx_out.astype(jnp.int32)
    rank_ref[...] = rank_out.astype(jnp.int32)
    gate_ref[...] = gate_out


def _norm_router(x, g, mr, sc_chunk, sh_chunk, seq_of_tile, rw, rb):
    M, D = x.shape
    E = rw.shape[1]
    tok = lambda c: pl.BlockSpec((TM, c), lambda i: (i, 0))
    return pl.pallas_call(
        _norm_router_kernel,
        grid=(M // TM,),
        in_specs=[tok(D),
                  pl.BlockSpec((1, D), lambda i: (0, 0)),
                  pl.BlockSpec((None, TM, D), lambda i: (seq_of_tile(i), 0, sc_chunk)),
                  pl.BlockSpec((None, TM, D), lambda i: (seq_of_tile(i), 0, sh_chunk)),
                  pl.BlockSpec((D, E), lambda i: (0, 0)),
                  pl.BlockSpec((1, E), lambda i: (0, 0))],
        out_specs=[tok(D // 2), tok(TOP_K), tok(TOP_K), tok(TOP_K), pl.BlockSpec((1, E), lambda i: (0, 0))],
        out_shape=[jax.ShapeDtypeStruct((M, D // 2), jnp.uint32), jax.ShapeDtypeStruct((M, TOP_K), jnp.int32),
                   jax.ShapeDtypeStruct((M, TOP_K), F32), jax.ShapeDtypeStruct((M, TOP_K), jnp.int32),
                   jax.ShapeDtypeStruct((1, E), F32)],
        scratch_shapes=[pltpu.VMEM((1, E), F32)],
        compiler_params=_cp(("arbitrary",), 40),
        name="norm_router",
    )(x, g, mr, mr, rw, rb)


def _row_copy(src_hbm, dst, src_row, dst_row, sem):
    return pltpu.make_async_copy(src_hbm.at[pl.ds(src_row, 1)], dst.at[pl.ds(dst_row, 1)], sem)


DISPATCH_TM = 512


def _dispatch_kernel(dest_ref, h_ref, xb_init_hbm, xb_hbm, sem):
    del xb_init_hbm
    i = pl.program_id(0)

    def issue(r, c):
        tok = i * DISPATCH_TM + r
        for k in range(TOP_K):
            _row_copy(h_ref, xb_hbm, r, dest_ref[tok * TOP_K + k], sem).start(priority=k % 2)
        return c

    lax.fori_loop(0, DISPATCH_TM, issue, 0, unroll=8)
    for k in range(TOP_K):
        pltpu.make_async_copy(h_ref, xb_hbm.at[pl.ds(0, DISPATCH_TM)], sem).wait()


def _dispatch(dest_flat, h2p, cap):
    M, Dp = h2p.shape
    any_spec = pl.BlockSpec(memory_space=pl.ANY)
    return pl.pallas_call(
        _dispatch_kernel,
        grid_spec=pltpu.PrefetchScalarGridSpec(
            num_scalar_prefetch=1, grid=(M // DISPATCH_TM,),
            in_specs=[pl.BlockSpec((DISPATCH_TM, Dp), lambda i, d: (i, 0)), any_spec], out_specs=any_spec,
            scratch_shapes=[pltpu.SemaphoreType.DMA(())]),
        out_shape=jax.ShapeDtypeStruct((cap, Dp), jnp.uint32),
        input_output_aliases={2: 0},
        compiler_params=_cp(("arbitrary",), 16),
        name="moe_dispatch",
    )(dest_flat, h2p, jnp.zeros((cap, Dp), jnp.uint32))


def _expert_weight_run(be_ref, nu_ref, nxt_ref, n_panels, copies, consume, slot_ref):
    n = pl.program_id(0)
    m = pl.program_id(1)
    e = be_ref[m]
    prev = be_ref[jnp.maximum(m - 1, 0)]

    @pl.when((m < nu_ref[0]) & ((m == 0) | (e != prev)))
    def _():
        @pl.when((n == 0) & (m == 0))
        def _():
            slot_ref[0] = 0
            for c in copies(0, e, n):
                c.start(priority=1)

        slot = slot_ref[0]
        for c in copies(slot, e, n):
            c.wait()
        consume(slot)
        ne = nxt_ref[m]

        @pl.when(ne >= 0)
        def _():
            for c in copies(1 - slot, ne, n):
                c.start(priority=1)

        @pl.when((ne < 0) & (n + 1 < n_panels))
        def _():
            for c in copies(1 - slot, be_ref[0], n + 1):
                c.start(priority=1)

        slot_ref[0] = 1 - slot


def _moe_up_kernel(be_ref, nu_ref, nxt_ref, x_ref, w_hbm, bg_ref, bl_ref, o_ref,
                   wbuf, wgb, wlb, slot_ref, sem, *, row0, tn, nf):
    m = pl.program_id(1)

    def copies(slot, e, n):
        return [pltpu.make_async_copy(w_hbm.at[row0 + e, :, pl.ds(pl.multiple_of((half * nf + n) * tn, tn), tn)],
                                      wbuf.at[slot, half], sem.at[slot, half]) for half in range(2)]

    def consume(slot):
        wgb[...] = wbuf[slot, 0].astype(BF16)
        wlb[...] = wbuf[slot, 1].astype(BF16)

    _expert_weight_run(be_ref, nu_ref, nxt_ref, nf, copies, consume, slot_ref)

    @pl.when(m < nu_ref[0])
    def _():
        x_lo, x_hi = _unpack_bf16_pairs(x_ref[...])
        n = x_lo.shape[1]
        hg = _dot(x_lo, wgb[0:n, :]) + _dot(x_hi, wgb[n:2 * n, :]) + bg_ref[...]
        hl = _dot(x_lo, wlb[0:n, :]) + _dot(x_hi, wlb[n:2 * n, :]) + bl_ref[...]
        hg = jnp.minimum(hg, SWIGLU_LIMIT)
        hl = jnp.clip(hl, -SWIGLU_LIMIT, SWIGLU_LIMIT)
        o_ref[...] = (hg * _sigmoid(SWIGLU_ALPHA * hg) * (hl + 1.0)).astype(o_ref.dtype)

    @pl.when(m >= nu_ref[0])
    def _():
        o_ref[...] = jnp.zeros(o_ref.shape, o_ref.dtype)


def _moe_up(blk_expert, n_used, nxt_expert, xb, w1, b1, l, E):
    cap, Dp = xb.shape
    D = w1.shape[1]
    F = w1.shape[2] // 2
    tn = 1024
    nf = F // tn
    n_blocks = cap // MOE_BM
    return pl.pallas_call(
        functools.partial(_moe_up_kernel, row0=l * E, tn=tn, nf=nf),
        grid_spec=pltpu.PrefetchScalarGridSpec(
            num_scalar_prefetch=3, grid=(nf, n_blocks),
            in_specs=[pl.BlockSpec((MOE_BM, Dp), lambda n, m, be, nu, nx: (m, 0)),
                      pl.BlockSpec(memory_space=pl.ANY),
                      pl.BlockSpec((None, 1, tn), lambda n, m, be, nu, nx: (l * E + be[m], 0, n)),
                      pl.BlockSpec((None, 1, tn), lambda n, m, be, nu, nx: (l * E + be[m], 0, nf + n))],
            out_specs=pl.BlockSpec((MOE_BM, tn), lambda n, m, be, nu, nx: (m, n)),
            scratch_shapes=[pltpu.VMEM((2, 2, D, tn), F32), pltpu.VMEM((D, tn), BF16), pltpu.VMEM((D, tn), BF16),
                            pltpu.SMEM((1,), jnp.int32), pltpu.SemaphoreType.DMA((2, 2))]),
        out_shape=jax.ShapeDtypeStruct((cap, F), BF16),
        compiler_params=_cp(("arbitrary", "arbitrary"), 56),
        name="moe_up",
    )(blk_expert, n_used, nxt_expert, xb, w1, b1, b1)


def _moe_down_kernel(be_ref, nu_ref, nxt_ref, a_ref, w_hbm, b_ref, o_ref, wbuf, wb, slot_ref, sem, *, row0):
    m = pl.program_id(1)

    def copies(slot, e, n):
        return [pltpu.make_async_copy(w_hbm.at[row0 + e], wbuf.at[slot], sem.at[slot])]

    def consume(slot):
        wb[...] = wbuf[slot].astype(BF16)

    _expert_weight_run(be_ref, nu_ref, nxt_ref, 1, copies, consume, slot_ref)

    @pl.when(m < nu_ref[0])
    def _():
        o_ref[...] = _dot(a_ref[...], wb[...]) + b_ref[...]

    @pl.when(m >= nu_ref[0])
    def _():
        o_ref[...] = jnp.zeros(o_ref.shape, o_ref.dtype)


def _moe_down(blk_expert, n_used, nxt_expert, act, w2, b2, l, E):
    cap, F = act.shape
    D = w2.shape[2]
    n_blocks = cap // MOE_BM
    return pl.pallas_call(
        functools.partial(_moe_down_kernel, row0=l * E),
        grid_spec=pltpu.PrefetchScalarGridSpec(
            num_scalar_prefetch=3, grid=(1, n_blocks),
            in_specs=[pl.BlockSpec((MOE_BM, F), lambda n, m, be, nu, nx: (m, 0)),
                      pl.BlockSpec(memory_space=pl.ANY),
                      pl.BlockSpec((None, 1, D), lambda n, m, be, nu, nx: (l * E + be[m], 0, 0))],
            out_specs=pl.BlockSpec((MOE_BM, D), lambda n, m, be, nu, nx: (m, 0)),
            scratch_shapes=[pltpu.VMEM((2, F, D), F32), pltpu.VMEM((F, D), BF16),
                            pltpu.SMEM((1,), jnp.int32), pltpu.SemaphoreType.DMA((2,))]),
        out_shape=jax.ShapeDtypeStruct((cap, D), F32),
        compiler_params=_cp(("arbitrary", "arbitrary"), 56),
        name="moe_down",
    )(blk_expert, n_used, nxt_expert, act, w2, b2)


COMBINE_TM = 128


def _combine_kernel(slot_ref, y_hbm, x_ref, g_ref, gate_ref, o_ref, buf, sem):
    i = pl.program_id(0)

    def gather(tile, s):
        def body(r, c):
            for k in range(TOP_K):
                row = slot_ref[(tile * COMBINE_TM + r) * TOP_K + k]
                _row_copy(y_hbm, buf.at[s, k], row, r, sem.at[s]).start(priority=k % 2)
            return c

        lax.fori_loop(0, COMBINE_TM, body, 0, unroll=8)

    @pl.when(i == 0)
    def _():
        gather(0, 0)

    @pl.when(i + 1 < pl.num_programs(0))
    def _():
        gather(i + 1, (i + 1) & 1)

    s = i & 1
    for k in range(TOP_K):
        pltpu.make_async_copy(y_hbm.at[pl.ds(0, COMBINE_TM)], buf.at[s, k], sem.at[s]).wait()
    y = gate_ref[:, 0:1] * buf[s, 0]
    for k in range(1, TOP_K):
        y = y + gate_ref[:, k:k + 1] * buf[s, k]
    o_ref[...] = x_ref[...] + g_ref[...] * y


def _combine(slot_of, yb, x, mr, gate_chunk, seq_of_tile, gate):
    M, D = x.shape
    per = TM // COMBINE_TM
    return pl.pallas_call(
        _combine_kernel,
        grid_spec=pltpu.PrefetchScalarGridSpec(
            num_scalar_prefetch=1, grid=(M // COMBINE_TM,),
            in_specs=[pl.BlockSpec(memory_space=pl.ANY),
                      pl.BlockSpec((COMBINE_TM, D), lambda i, s: (i, 0)),
                      pl.BlockSpec((None, COMBINE_TM, D), lambda i, s: (seq_of_tile(i // per), 0, gate_chunk)),
                      pl.BlockSpec((COMBINE_TM, TOP_K), lambda i, s: (i, 0))],
            out_specs=pl.BlockSpec((COMBINE_TM, D), lambda i, s: (i, 0)),
            scratch_shapes=[pltpu.VMEM((2, TOP_K, COMBINE_TM, D), F32), pltpu.SemaphoreType.DMA((2,))]),
        out_shape=jax.ShapeDtypeStruct((M, D), F32),
        compiler_params=_cp(("arbitrary",), 32),
        name="moe_combine",
    )(slot_of, yb, x, mr, gate)


def _slot_tables(top_idx, rank, counts, n_blocks):
    E = counts.shape[0]
    padded = ((counts + MOE_BM - 1) // MOE_BM) * MOE_BM
    pad_end = jnp.cumsum(padded)
    pad_start = pad_end - padded
    onehot = top_idx[..., None] == jnp.arange(E, dtype=jnp.int32)
    dest = jnp.sum(jnp.where(onehot, pad_start, 0), -1) + rank
    blk_start = jnp.arange(n_blocks, dtype=jnp.int32) * MOE_BM
    blk_expert = jnp.minimum(jnp.sum(blk_start[:, None] >= pad_end[None, :], -1), E - 1).astype(jnp.int32)
    n_used = (pad_end[-1] // MOE_BM).astype(jnp.int32).reshape(1)
    ids = jnp.arange(E, dtype=jnp.int32)
    later = (ids[None, :] > ids[:, None]) & (counts[None, :] > 0)
    nxt_of_expert = jnp.min(jnp.where(later, ids[None, :], E), axis=1)
    nxt_of_expert = jnp.where(nxt_of_expert == E, -1, nxt_of_expert)
    nxt_expert = jnp.sum(jnp.where(blk_expert[:, None] == ids[None, :], nxt_of_expert[None, :], 0), -1)
    return dest.reshape(-1).astype(jnp.int32), blk_expert, n_used, nxt_expert.astype(jnp.int32)


def _rope_tables(pos, hd):
    rot = hd // 4
    half = rot // 2
    inv = jnp.power(jnp.float32(ROPE_THETA), -jnp.arange(half, dtype=F32) * 2.0 / rot)
    ang = pos.astype(F32)[:, None] * inv[None, :]
    cos, sin = jnp.cos(ang), jnp.sin(ang)
    n = pos.shape[0]
    c = jnp.concatenate([cos, cos, jnp.ones((n, hd - rot), F32)], 1)
    sa = jnp.concatenate([-sin, jnp.zeros((n, hd - half), F32)], 1)
    sb = jnp.concatenate([jnp.zeros((n, half), F32), sin, jnp.zeros((n, hd - rot), F32)], 1)
    return c, sa, sb


def kernel(x_prompt, x_sample, cache_k, cache_v, state_conv, state_pool, page_table, c_prompt, c_sample,
           w_ada, b_ada, norm1_g, norm2_g, w_in, conv_w, conv_b, conv_ln_g, conv_ln_b, conv_out,
           pool_w, pool_scale, pool_out, q_norm_g, k_norm_g, attn_out, gmlp_ln_g, gmlp_ln_b,
           gmlp_ws, gmlp_b, gmlp_out, w_o, router_w, router_b, exp_w1, exp_b1, exp_w2, exp_b2):
    B, T, D = x_prompt.shape
    NB, TS, _ = x_sample.shape
    L = w_ada.shape[0]
    W = D // N_BRANCH
    HD = W // N_HEADS
    E = router_w.shape[2]
    n_pool, PAGE = cache_k.shape[1], cache_k.shape[2]
    n_pages = page_table.shape[1]
    past = n_pages * PAGE
    M_p, M_s = B * T, NB * TS
    M = M_p + M_s
    assert HD == LANE and W == N_HEADS * LANE and T % TM == 0 and TM % NB == 0 and NB % 8 == 0
    assert M % DISPATCH_TM == 0 and M_s % TM == 0 and TS <= 8 and past % MOBA_BLOCK == 0
    tiles_per_seq = T // TM

    def seq_of_tile(i):
        return jnp.minimum(i // tiles_per_seq, B)

    x = jnp.concatenate([x_prompt.reshape(M_p, D), x_sample.transpose(1, 0, 2).reshape(M_s, D)], 0)
    n_c = B + NB
    c_all = jnp.concatenate([c_prompt, c_sample, jnp.zeros((-n_c % 8, D), F32)], 0)
    mod_all = _ada(c_all, w_ada, b_ada)

    cos_p, sa_p, sb_p = _rope_tables(jnp.arange(T), HD)
    cos_s, sa_s, sb_s = [jnp.pad(t, ((0, 8 - TS), (0, 0))) for t in _rope_tables(past + jnp.arange(TS), HD)]
    pt_flat = page_table.reshape(-1).astype(jnp.int32)
    ck = cache_k.reshape(L, n_pool, PAGE * N_HEADS, HD)
    cv = cache_v.reshape(L, n_pool, PAGE * N_HEADS, HD)
    w1 = exp_w1.reshape(L * E, D, exp_w1.shape[3])
    b1 = exp_b1.reshape(L * E, 1, exp_b1.shape[2])
    w2 = exp_w2.reshape(L * E, exp_w2.shape[2], D)
    b2 = exp_b2.reshape(L * E, 1, D)

    def to_bt(a_t):
        return a_t.reshape(TS, NB, -1).transpose(1, 0, 2)

    outs = {k: [] for k in ('kp', 'vp', 'ks', 'vs', 'cp', 'cs', 'pp', 'ps', 'gp', 'gs')}
    for l in range(L):
        mod = mod_all[l]
        mr = jnp.concatenate([jnp.broadcast_to(mod[:B, None, :], (B, TM, 6 * D)),
                              jnp.tile(mod[B:B + NB], (TM // NB, 1))[None]], 0)
        h = _norm_mod(x, norm1_g[l][None], mr, 1, 0, seq_of_tile, BF16)
        tm_proj = next(t for t in (1088, 1024, 512, 256) if M % t == 0)
        proj = _panel_mm(h, w_in, l, tm_proj, 1024)

        row = lambda v: v[l][None]
        ya_p, conv_p = _conv_prompt(proj, B, T, W, conv_w[l], row(conv_b), row(conv_ln_g), row(conv_ln_b))
        yb_p, pool_p = _pool_prompt(proj, B, T, W, pool_w[l], row(pool_scale))
        q_p, k_p, v_p, kb_p, vb_p, kmean_p = _qkv_prompt(proj, B, T, W, row(q_norm_g), row(k_norm_g),
                                                        cos_p, sa_p, sb_p)
        yc_p = _attn_prompt(q_p, kb_p, vb_p, kmean_p, B, T, W)
        yd_p, gv_p = _gmlp_prompt(proj, B, T, W, row(gmlp_ln_g), row(gmlp_ln_b), gmlp_ws[l],
                                  jnp.transpose(gmlp_b[l]))

        proj_s = proj[M_p:, :8 * W].reshape(TS, NB, 8 * W)
        (ya_s, yb_s, yd_s, q_s, k_s, v_s, conv_s, pool_s, gv_s) = _sample_mix(
            proj_s, NB, TS, W, state_conv[l].transpose(1, 0, 2), state_pool[l].transpose(1, 0, 2),
            conv_w[l], row(conv_b), row(conv_ln_g), row(conv_ln_b), pool_w[l], row(pool_scale),
            row(q_norm_g), row(k_norm_g), cos_s, sa_s, sb_s, row(gmlp_ln_g), row(gmlp_ln_b),
            gmlp_ws[l], gmlp_b[l])
        q_hr = jnp.pad(to_bt(q_s).reshape(NB, TS, N_HEADS, HD).transpose(0, 2, 1, 3),
                       ((0, 0), (0, 0), (0, 8 - TS), (0, 0))).reshape(NB, N_HEADS * 8, HD)
        k_new = to_bt(k_s).reshape(NB, TS * N_HEADS, HD)
        v_new = to_bt(v_s).reshape(NB, TS * N_HEADS, HD)
        att_s = _attn_sample(pt_flat, q_hr, k_new, v_new, ck, cv, l, NB, TS, n_pages)
        yc_s = (att_s.reshape(NB, N_HEADS, 8, HD)[:, :, :TS].transpose(2, 0, 1, 3)
                .reshape(M_s, W).astype(BF16))

        merged = _merge((ya_p, yb_p, yc_p, yd_p), [s_.reshape(M_s, W) for s_ in (ya_s, yb_s, yc_s, yd_s)],
                        (conv_out, pool_out, attn_out, gmlp_out), proj, l, W, D)
        x = _wo_residual(merged, w_o, l, x, mr, 2, seq_of_tile)
        h2p, top_idx, gate, rank, counts = _norm_router(x, norm2_g[l][None], mr, 4, 3, seq_of_tile,
                                                         router_w[l], router_b[l][None])
        n_blocks = -(-M * TOP_K // MOE_BM) + E
        dest, blk_expert, n_used, nxt_expert = _slot_tables(top_idx, rank, counts[0].astype(jnp.int32), n_blocks)
        xb = _dispatch(dest, h2p, n_blocks * MOE_BM)
        act = _moe_up(blk_expert, n_used, nxt_expert, xb, w1, b1, l, E)
        yb = _moe_down(blk_expert, n_used, nxt_expert, act, w2, b2, l, E)
        x = _combine(dest, yb, x, mr, 5, seq_of_tile, gate)

        outs['kp'].append(k_p.reshape(B, T, N_HEADS, HD))
        outs['vp'].append(v_p.reshape(B, T, N_HEADS, HD))
        outs['ks'].append(to_bt(k_s).reshape(NB, TS, N_HEADS, HD))
        outs['vs'].append(to_bt(v_s).reshape(NB, TS, N_HEADS, HD))
        outs['cp'].append(conv_p)
        outs['cs'].append(conv_s.transpose(1, 0, 2))
        outs['pp'].append(pool_p)
        outs['ps'].append(pool_s.transpose(1, 0, 2))
        outs['gp'].append(gv_p)
        outs['gs'].append(to_bt(gv_s))

    y_prompt = x[:M_p].reshape(B, T, D)
    y_sample = to_bt(x[M_p:])
    st = lambda k: jnp.stack(outs[k])
    return (y_prompt, y_sample, st('kp'), st('vp'), st('ks'), st('vs'), st('cp'), st('cs'),
            st('pp'), st('ps'), st('gp'), st('gs'))
```
